```python
import jax, jax.numpy as jnp
from jax import lax
import numpy as np

D_MODEL = 1024
BATCH = 4
SEQ = 8192
DEPTH = 4

HEAD_DIM = 64
N_HEADS_A = 8
N_HEADS_B = 8
N_KV_B = 2
GQA_GROUP = N_HEADS_B // N_KV_B
DILATED_BRANCHES = ((128, 1), (512, 4), (2048, 16))
SWA_WINDOW = 128
BLK = 128
ROPE_THETA = 10000.0
D_FF = 3584
N_EXPERTS = 8
TOP_K = 2
RMS_EPS = 1e-5
WIDTH_A = N_HEADS_A * HEAD_DIM
WIDTH_B = N_HEADS_B * HEAD_DIM
KV_WIDTH_B = N_KV_B * HEAD_DIM
MIX_WIDTH = WIDTH_A + WIDTH_B
IN_SIZES = (WIDTH_A, WIDTH_A, WIDTH_A, WIDTH_B, KV_WIDTH_B, KV_WIDTH_B)
IN_WIDTH = sum(IN_SIZES)
IN_SPLITS = tuple(int(c) for c in np.cumsum(IN_SIZES)[:-1])
N_DENSE = (DEPTH + 1) // 2
N_MOE = DEPTH // 2

kernel_name = "hybrid_dilated_swa_sink_moe_trunk"


def rmsnorm(x, g):
    xf = x.astype(jnp.float32)
    y = xf * lax.rsqrt(jnp.mean(xf * xf, axis=-1, keepdims=True) + RMS_EPS)
    return (y * g.astype(jnp.float32)).astype(x.dtype)


def rope_tables(positions):
    inv = ROPE_THETA ** (-jnp.arange(0, HEAD_DIM, 2, dtype=jnp.float32) / HEAD_DIM)
    ang = positions.astype(jnp.float32)[:, None] * inv[None, :]
    ang = jnp.concatenate([ang, ang], axis=-1)
    return jnp.cos(ang), jnp.sin(ang)


def apply_rope(t, cos, sin):
    t1, t2 = jnp.split(t, 2, axis=-1)
    rot = jnp.concatenate([-t2, t1], axis=-1)
    return t * cos.astype(t.dtype) + rot * sin.astype(t.dtype)


def banded_attention(q, k, v, max_dist, sink=None):
    *lead, G, L, Dh = q.shape
    nb = L // BLK
    qb = q.reshape(*lead, G, nb, BLK, Dh)

    def band(t):
        tb = t.reshape(*lead, nb, BLK, Dh)
        prev = jnp.concatenate([jnp.zeros_like(tb[..., :1, :, :]), tb[..., :-1, :, :]], axis=-3)
        return jnp.concatenate([prev, tb], axis=-2)

    s = jnp.einsum('...gnqd,...nkd->...gnqk', qb, band(k),
                   preferred_element_type=jnp.float32) * (Dh ** -0.5)
    dist = (BLK + jnp.arange(BLK))[:, None] - jnp.arange(2 * BLK)[None, :]
    in_band = (dist >= 0) & (dist <= max_dist)
    has_prev = (jnp.arange(nb) > 0)[:, None, None] | (jnp.arange(2 * BLK) >= BLK)[None, None, :]
    mask = in_band[None] & has_prev
    s = jnp.where(mask, s, -jnp.inf)
    m = jnp.max(s, axis=-1)
    if sink is not None:
        sk = sink.astype(jnp.float32)[..., None, None]
        m = jnp.maximum(m, sk)
    p = jnp.exp(s - m[..., None])
    l = jnp.sum(p, axis=-1)
    if sink is not None:
        l = l + jnp.exp(sk - m)
    o = jnp.einsum('...gnqk,...nkd->...gnqd', p.astype(v.dtype), band(v),
                   preferred_element_type=jnp.float32) / l[..., None]
    lse = m + jnp.log(l)
    return o.reshape(*lead, G, L, Dh).astype(q.dtype), lse.reshape(*lead, G, L)


def dilated_branch(q, k, v, window, dilation):
    B, H, S, Dh = q.shape
    seg = dilation * BLK
    Sp = -(-S // seg) * seg

    def by_stride(t):
        t = jnp.pad(t, ((0, 0), (0, 0), (0, Sp - S), (0, 0)))
        return t.reshape(B, H, Sp // dilation, dilation, Dh).transpose(0, 1, 3, 2, 4)

    o, lse = banded_attention(by_stride(q)[..., None, :, :], by_stride(k), by_stride(v),
                              max_dist=window // dilation)
    o = o[:, :, :, 0].transpose(0, 1, 3, 2, 4).reshape(B, H, Sp, Dh)[:, :, :S]
    lse = lse[:, :, :, 0].transpose(0, 1, 3, 2).reshape(B, H, Sp)[:, :, :S]
    return o, lse


def dilated_mixer(q, k, v):
    outs, lses = zip(*[dilated_branch(q, k, v, w, d) for (w, d) in DILATED_BRANCHES])
    wts = jax.nn.softmax(jnp.stack(lses, axis=0), axis=0)
    o = jnp.einsum('rbhs,rbhsd->bhsd', wts, jnp.stack(outs, axis=0).astype(jnp.float32))
    return o.astype(q.dtype)


def swiglu(h, wg, wu, wd):
    return (jax.nn.silu(h @ wg) * (h @ wu)) @ wd


def moe_swiglu(h, router, wg, wu, wd):
    logits = (h @ router).astype(jnp.float32)
    top_vals, top_idx = lax.top_k(logits, TOP_K)
    gates = jax.nn.softmax(top_vals, axis=-1)
    gate_e = jnp.sum(jax.nn.one_hot(top_idx, N_EXPERTS, dtype=jnp.float32) * gates[..., None], axis=-2)
    gate_e = gate_e.astype(h.dtype)
    out = jnp.zeros_like(h)
    for e in range(N_EXPERTS):
        out = out + gate_e[..., e:e + 1] * swiglu(h, wg[e], wu[e], wd[e])
    return out


def heads(t, n):
    B, S, _ = t.shape
    return t.reshape(B, S, n, HEAD_DIM).transpose(0, 2, 1, 3)


def merge_heads(t):
    B, H, S, Dh = t.shape
    return t.transpose(0, 2, 1, 3).reshape(B, S, H * Dh)


def setup_inputs(seed: int = 0) -> dict:
    key = jax.random.key(seed)
    ks = jax.random.split(key, 20)
    f32 = jnp.float32
    nrm = lambda k, shape, fan_in: jax.random.normal(k, shape, f32) * (fan_in ** -0.5)
    gain = lambda k, shape: 1.0 + 0.02 * jax.random.normal(k, shape, f32)
    return {
        "x": jax.random.normal(ks[0], (BATCH, SEQ, D_MODEL), f32),
        "positions": jnp.arange(SEQ, dtype=jnp.int32),
        "attn_norm": gain(ks[1], (DEPTH, D_MODEL)),
        "w_in": nrm(ks[2], (DEPTH, D_MODEL, IN_WIDTH), D_MODEL),
        "mix_norm_a": gain(ks[3], (DEPTH, WIDTH_A)),
        "mix_norm_b": gain(ks[4], (DEPTH, WIDTH_B)),
        "sinks": 0.5 * jax.random.normal(ks[5], (DEPTH, N_HEADS_B), f32),
        "w_out": nrm(ks[6], (DEPTH, MIX_WIDTH, D_MODEL), MIX_WIDTH) * 0.5,
        "ffn_norm": gain(ks[7], (DEPTH, D_MODEL)),
        "dense_w_gate": nrm(ks[8], (N_DENSE, D_MODEL, D_FF), D_MODEL),
        "dense_w_up": nrm(ks[9], (N_DENSE, D_MODEL, D_FF), D_MODEL),
        "dense_w_down": nrm(ks[10], (N_DENSE, D_FF, D_MODEL), D_FF) * 0.5,
        "router": nrm(ks[11], (N_MOE, D_MODEL, N_EXPERTS), D_MODEL),
        "moe_w_gate": nrm(ks[12], (N_MOE, N_EXPERTS, D_MODEL, D_FF), D_MODEL),
        "moe_w_up": nrm(ks[13], (N_MOE, N_EXPERTS, D_MODEL, D_FF), D_MODEL),
        "moe_w_down": nrm(ks[14], (N_MOE, N_EXPERTS, D_FF, D_MODEL), D_FF) * 0.5,
        "final_norm": gain(ks[15], (D_MODEL,)),
    }


def reference(x, positions, attn_norm, w_in, mix_norm_a, mix_norm_b, sinks, w_out,
              ffn_norm, dense_w_gate, dense_w_up, dense_w_down, router,
              moe_w_gate, moe_w_up, moe_w_down, final_norm):
    B, S, _ = x.shape
    cos, sin = rope_tables(positions)
    for i in range(DEPTH):
        h = rmsnorm(x, attn_norm[i])
        proj = h @ w_in[i]
        qa, ka, va, qb, kb, vb = jnp.split(proj, IN_SPLITS, axis=-1)
        qa = apply_rope(heads(qa, N_HEADS_A), cos, sin)
        ka = apply_rope(heads(ka, N_HEADS_A), cos, sin)
        oa = dilated_mixer(qa, ka, heads(va, N_HEADS_A))
        qb = apply_rope(heads(qb, N_HEADS_B), cos, sin).reshape(B, N_KV_B, GQA_GROUP, S, HEAD_DIM)
        kb = apply_rope(heads(kb, N_KV_B), cos, sin)
        ob, _ = banded_attention(qb, kb, heads(vb, N_KV_B), SWA_WINDOW - 1,
                                 sink=sinks[i].reshape(N_KV_B, GQA_GROUP))
        ob = ob.reshape(B, N_HEADS_B, S, HEAD_DIM)
        mix = jnp.concatenate([rmsnorm(merge_heads(oa), mix_norm_a[i]),
                               rmsnorm(merge_heads(ob), mix_norm_b[i])], axis=-1)
        x = x + mix @ w_out[i]
        h = rmsnorm(x, ffn_norm[i])
        j = i // 2
        if i % 2 == 0:
            x = x + swiglu(h, dense_w_gate[j], dense_w_up[j], dense_w_down[j])
        else:
            x = x + moe_swiglu(h, router[j], moe_w_gate[j], moe_w_up[j], moe_w_down[j])
    return rmsnorm(x, final_norm)
```

```python
import functools

import numpy as np
import jax
import jax.numpy as jnp
from jax import lax
from jax.experimental import pallas as pl
from jax.experimental.pallas import tpu as pltpu

D_MODEL = 1024
HEAD_DIM = 64
N_HEADS_A = 8
N_HEADS_B = 8
N_KV_B = 2
DILATED_BRANCHES = ((128, 1), (512, 4), (2048, 16))
SWA_WINDOW = 128
BLK = 128
ROPE_THETA = 10000.0
D_FF = 3584
N_EXPERTS = 8
RMS_EPS = 1e-5
WIDTH_A = N_HEADS_A * HEAD_DIM
WIDTH_B = N_HEADS_B * HEAD_DIM
KV_WIDTH_B = N_KV_B * HEAD_DIM

LANES = 128
NEG_BIG = -1e30
VMEM_LIMIT = 56 * 1024 * 1024

F32 = jnp.float32
BF16 = jnp.bfloat16


def _params(sem):
    return pltpu.CompilerParams(dimension_semantics=sem, vmem_limit_bytes=VMEM_LIMIT)


def _rms(x, g):
    return x * lax.rsqrt(jnp.mean(x * x, axis=-1, keepdims=True) + RMS_EPS) * g


def _rope_table_kernel(pos_ref, inv_ref, cos_ref, sina_ref, sinb_ref):
    ang = pos_ref[...].astype(F32) * inv_ref[...]
    c = jnp.cos(ang)
    s = jnp.sin(ang)
    lane = lax.broadcasted_iota(jnp.int32, ang.shape, 1)
    upper = (lane & (HEAD_DIM - 1)) >= HEAD_DIM // 2
    cos_ref[...] = c
    sina_ref[...] = jnp.where(upper, s, 0.0)
    sinb_ref[...] = jnp.where(upper, 0.0, -s)


def _rope_tables(positions):
    seq = positions.shape[0]
    inv = ROPE_THETA ** (-jnp.arange(0, HEAD_DIM, 2, dtype=F32) / HEAD_DIM)
    inv = jnp.tile(inv, LANES // (HEAD_DIM // 2))[None, :]
    rows = 1024
    tab = jax.ShapeDtypeStruct((seq, LANES), F32)
    return pl.pallas_call(
        _rope_table_kernel,
        grid=(seq // rows,),
        in_specs=[pl.BlockSpec((rows, 1), lambda i: (i, 0)),
                  pl.BlockSpec((1, LANES), lambda i: (0, 0))],
        out_specs=[pl.BlockSpec((rows, LANES), lambda i: (i, 0))] * 3,
        out_shape=[tab, tab, tab],
        compiler_params=_params(("arbitrary",)),
        name="rope_tables",
    )(positions.reshape(seq, 1), inv)


_SEGS = (("qa", WIDTH_A, True, HEAD_DIM ** -0.5), ("ka", WIDTH_A, True, 1.0),
         ("va", WIDTH_A, False, 1.0), ("qb", WIDTH_B, True, HEAD_DIM ** -0.5),
         ("kb", 2 * KV_WIDTH_B, True, 1.0), ("vb", 2 * KV_WIDTH_B, False, 1.0))


def _inproj_kernel(x_ref, g_ref, w_ref, cos_ref, sina_ref, sinb_ref, *out_refs):
    h = _rms(x_ref[...], g_ref[...]).astype(BF16)
    cos = cos_ref[...]
    sina = sina_ref[...]
    sinb = sinb_ref[...]
    col = 0
    for (_, width, roped, scale), o_ref in zip(_SEGS, out_refs):
        t = jnp.dot(h, w_ref[:, col:col + width], preferred_element_type=F32)
        col += width
        for j in range(width // LANES):
            tj = t[:, j * LANES:(j + 1) * LANES]
            if roped:
                tj = (tj * cos + pltpu.roll(tj, HEAD_DIM // 2, 1) * sina
                      + pltpu.roll(tj, LANES - HEAD_DIM // 2, 1) * sinb)
                if scale != 1.0:
                    tj = tj * scale
            o_ref[:, j * LANES:(j + 1) * LANES] = tj.astype(o_ref.dtype)


def _inproj(x2, g, w_ext, cos, sina, sinb, seq):
    tokens = x2.shape[0]
    tm = 512
    spt = seq // tm
    widths = [s[1] for s in _SEGS]
    total = sum(widths)
    return pl.pallas_call(
        _inproj_kernel,
        grid=(tokens // tm,),
        in_specs=[pl.BlockSpec((tm, D_MODEL), lambda i: (i, 0)),
                  pl.BlockSpec((1, D_MODEL), lambda i: (0, 0)),
                  pl.BlockSpec((D_MODEL, total), lambda i: (0, 0)),
                  pl.BlockSpec((tm, LANES), lambda i: (i % spt, 0)),
                  pl.BlockSpec((tm, LANES), lambda i: (i % spt, 0)),
                  pl.BlockSpec((tm, LANES), lambda i: (i % spt, 0))],
        out_specs=[pl.BlockSpec((tm, w), lambda i: (i, 0)) for w in widths],
        out_shape=[jax.ShapeDtypeStruct((tokens, w), BF16) for w in widths],
        compiler_params=_params(("arbitrary",)),
        name="inproj_rope",
    )(x2, g, w_ext, cos, sina, sinb)


def _band_bias(max_dist):
    p = np.arange(BLK)[:, None]
    c = np.arange(2 * BLK)[None, :]
    first = c <= p
    dist = BLK + p - c
    main = (dist >= 0) & (dist <= max_dist)
    bias = np.where(np.stack([first, main]), 0.0, NEG_BIG).astype(np.float32)
    return jnp.asarray(np.concatenate([bias, bias], axis=1))


def _band_attn_kernel(*refs, n_blocks, use_sink):
    if use_sink:
        sink_ref, q_ref, k_ref, v_ref, bias_ref, o_ref, lse_ref = refs
    else:
        q_ref, k_ref, v_ref, bias_ref, o_ref, lse_ref = refs
    lane = lax.broadcasted_iota(jnp.int32, (BLK, LANES), 1)
    low = lane < HEAD_DIM
    if use_sink:
        c = pl.program_id(1)
        row = lax.broadcasted_iota(jnp.int32, (2 * BLK, 1), 0)
        sink = jnp.where(row < BLK, sink_ref[2 * c], sink_ref[2 * c + 1])

    def body(n, carry):
        q0 = pl.multiple_of(n * BLK, BLK)
        k0 = pl.multiple_of(jnp.maximum(n - 1, 0) * BLK, BLK)
        q = q_ref[0, pl.ds(q0, BLK), :]
        zero = jnp.zeros_like(q)
        qq = jnp.concatenate([jnp.where(low, q, zero), jnp.where(low, zero, q)], axis=0)
        kb = k_ref[0, pl.ds(k0, 2 * BLK), :]
        vb = v_ref[0, pl.ds(k0, 2 * BLK), :]
        s = lax.dot_general(qq, kb, (((1,), (1,)), ((), ())),
                            preferred_element_type=F32)
        s = s + bias_ref[jnp.minimum(n, 1)]
        m = jnp.max(s, axis=1, keepdims=True)
        if use_sink:
            m = jnp.maximum(m, sink)
        p = jnp.exp(s - m)
        l = jnp.sum(p, axis=1, keepdims=True)
        if use_sink:
            l = l + jnp.exp(sink - m)
        o = jnp.dot(p.astype(BF16), vb, preferred_element_type=F32)
        o = o * (1.0 / l)
        o_ref[0, pl.ds(q0, BLK), :] = jnp.where(low, o[:BLK], o[BLK:]).astype(o_ref.dtype)
        lse = m + jnp.log(l)
        lse_ref[0, pl.ds(q0, BLK), :] = jnp.where(
            low, jnp.broadcast_to(lse[:BLK], (BLK, LANES)),
            jnp.broadcast_to(lse[BLK:], (BLK, LANES)))
        return carry

    lax.fori_loop(0, n_blocks, body, 0)


def _band_attention(q, k, v, max_dist, kv_block_of, sinks=None):
    batch, length, wq = q.shape
    wk = k.shape[2]
    n_blocks = length // BLK
    use_sink = sinks is not None
    kernel = functools.partial(_band_attn_kernel, n_blocks=n_blocks, use_sink=use_sink)
    in_specs = [pl.BlockSpec((1, length, LANES), lambda b, c, *_: (b, 0, c)),
                pl.BlockSpec((1, length, LANES), lambda b, c, *_: (b, 0, kv_block_of(c))),
                pl.BlockSpec((1, length, LANES), lambda b, c, *_: (b, 0, kv_block_of(c))),
                pl.BlockSpec((2, 2 * BLK, 2 * BLK), lambda b, c, *_: (0, 0, 0))]
    out_specs = [pl.BlockSpec((1, length, LANES), lambda b, c, *_: (b, 0, c)),
                 pl.BlockSpec((1, length, LANES), lambda b, c, *_: (b, 0, c))]
    out_shape = [jax.ShapeDtypeStruct((batch, length, wq), BF16),
                 jax.ShapeDtypeStruct((batch, length, wq), F32)]
    grid_spec = pltpu.PrefetchScalarGridSpec(
        num_scalar_prefetch=1 if use_sink else 0,
        grid=(batch, wq // LANES), in_specs=in_specs, out_specs=out_specs)
    args = ((sinks,) if use_sink else ()) + (q, k, v, _band_bias(max_dist))
    del wk
    return pl.pallas_call(
        kernel, grid_spec=grid_spec, out_shape=out_shape,
        compiler_params=_params(("arbitrary", "arbitrary")),
        name="band_attn_sink" if use_sink else f"band_attn_L{length}",
    )(*args)


def _mix(o_refs, lse_refs):
    lses = [r[...] for r in lse_refs]
    mx = functools.reduce(jnp.maximum, lses)
    es = [jnp.exp(l - mx) for l in lses]
    num = functools.reduce(lambda a, b: a + b,
                           [e * o[...].astype(F32) for e, o in zip(es, o_refs)])
    return num / functools.reduce(lambda a, b: a + b, es)


def _top2_gates(logits):
    lane = lax.broadcasted_iota(jnp.int32, logits.shape, 1)
    v1 = jnp.max(logits, axis=1, keepdims=True)
    i1 = jnp.min(jnp.where(logits == v1, lane, LANES), axis=1, keepdims=True)
    rest = jnp.where(lane == i1, NEG_BIG, logits)
    v2 = jnp.max(rest, axis=1, keepdims=True)
    i2 = jnp.min(jnp.where(rest == v2, lane, LANES), axis=1, keepdims=True)
    e2 = jnp.exp(v2 - v1)
    g1 = 1.0 / (1.0 + e2)
    g2 = e2 / (1.0 + e2)
    return jnp.where(lane == i1, g1, 0.0) + jnp.where(lane == i2, g2, 0.0)


def _outproj_kernel(*refs, with_router):
    (o1, o4, o16, l1, l4, l16, ob_ref, x_ref, ga_ref, gb_ref, w_ref, gf_ref) = refs[:12]
    rest = refs[12:]
    if with_router:
        r_ref, xo_ref, ho_ref, gate_ref = rest
    else:
        xo_ref, ho_ref = rest
    oa = _mix((o1, o4, o16), (l1, l4, l16))
    na = _rms(oa, ga_ref[...]).astype(BF16)
    nb = _rms(ob_ref[...].astype(F32), gb_ref[...]).astype(BF16)
    xn = (x_ref[...] + jnp.dot(na, w_ref[:WIDTH_A, :], preferred_element_type=F32)
          + jnp.dot(nb, w_ref[WIDTH_A:, :], preferred_element_type=F32))
    xo_ref[...] = xn
    hf = _rms(xn, gf_ref[...])
    ho_ref[...] = hf.astype(BF16)
    if with_router:
        logits = jnp.dot(hf, r_ref[...], preferred_element_type=F32,
                         precision=lax.Precision.HIGHEST)
        lane = lax.broadcasted_iota(jnp.int32, logits.shape, 1)
        logits = jnp.where(lane < N_EXPERTS, logits, NEG_BIG)
        gate_ref[...] = _top2_gates(logits)


def _outproj(os_, lses, ob, x2, ga, gb, w_out, gf, router_pad=None):
    tokens = x2.shape[0]
    tm = 512
    with_router = router_pad is not None
    row = lambda w: pl.BlockSpec((tm, w), lambda i: (i, 0))
    const = lambda a, b: pl.BlockSpec((a, b), lambda i: (0, 0))
    in_specs = ([row(WIDTH_A)] * 3 + [row(WIDTH_A)] * 3 + [row(WIDTH_B), row(D_MODEL),
                const(1, WIDTH_A), const(1, WIDTH_B), const(D_MODEL, D_MODEL), const(1, D_MODEL)])
    out_specs = [row(D_MODEL), row(D_MODEL)]
    out_shape = [jax.ShapeDtypeStruct((tokens, D_MODEL), F32),
                 jax.ShapeDtypeStruct((tokens, D_MODEL), BF16)]
    args = list(os_) + list(lses) + [ob, x2, ga, gb, w_out, gf]
    if with_router:
        in_specs.append(const(D_MODEL, LANES))
        out_specs.append(row(LANES))
        out_shape.append(jax.ShapeDtypeStruct((tokens, LANES), F32))
        args.append(router_pad)
    return pl.pallas_call(
        functools.partial(_outproj_kernel, with_router=with_router),
        grid=(tokens // tm,), in_specs=in_specs, out_specs=out_specs, out_shape=out_shape,
        compiler_params=_params(("arbitrary",)),
        name="mix_outproj_router" if with_router else "mix_outproj",
    )(*args)


def _silu(x):
    return x * (1.0 / (1.0 + jnp.exp(-x)))


def _ffn_kernel(*refs, gated, final):
    if gated:
        h_ref, gate_ref, wg_ref, wu_ref, wd_ref, x_ref = refs[:6]
        rest = refs[6:]
    else:
        h_ref, wg_ref, wu_ref, wd_ref, x_ref = refs[:5]
        rest = refs[5:]
    if final:
        gfin_ref, o_ref, acc_ref = rest
    else:
        o_ref, acc_ref = rest
    if gated:
        e = pl.program_id(1)
        f = pl.program_id(2)
        first = (e == 0) & (f == 0)
        last = (e == pl.num_programs(1) - 1) & (f == pl.num_programs(2) - 1)
    else:
        f = pl.program_id(1)
        first = f == 0
        last = f == pl.num_programs(1) - 1

    @pl.when(first)
    def _():
        acc_ref[...] = x_ref[...]

    h = h_ref[...]
    wg = wg_ref[0] if gated else wg_ref[...]
    wu = wu_ref[0] if gated else wu_ref[...]
    wd = wd_ref[0] if gated else wd_ref[...]
    g = jnp.dot(h, wg, preferred_element_type=F32)
    u = jnp.dot(h, wu, preferred_element_type=F32)
    a = _silu(g) * u
    if gated:
        gates = gate_ref[...]
        lane = lax.broadcasted_iota(jnp.int32, gates.shape, 1)
        a = a * jnp.sum(jnp.where(lane == e, gates, 0.0), axis=1, keepdims=True)
    acc_ref[...] += jnp.dot(a.astype(BF16), wd, preferred_element_type=F32)

    @pl.when(last)
    def _():
        if final:
            o_ref[...] = _rms(acc_ref[...], gfin_ref[...])
        else:
            o_ref[...] = acc_ref[...]


def _ffn(h, x2, wg, wu, wd, gates=None, final_gain=None):
    tokens = x2.shape[0]
    tm, tf = 1024, 512
    gated = gates is not None
    final = final_gain is not None
    nf = D_FF // tf
    if gated:
        grid = (tokens // tm, N_EXPERTS, nf)
        tok = lambda w: pl.BlockSpec((tm, w), lambda i, e, f: (i, 0))
        in_specs = [tok(D_MODEL), tok(LANES),
                    pl.BlockSpec((1, D_MODEL, tf), lambda i, e, f: (e, 0, f)),
                    pl.BlockSpec((1, D_MODEL, tf), lambda i, e, f: (e, 0, f)),
                    pl.BlockSpec((1, tf, D_MODEL), lambda i, e, f: (e, f, 0)),
                    tok(D_MODEL)]
        args = [h, gates, wg, wu, wd, x2]
        const = pl.BlockSpec((1, D_MODEL), lambda i, e, f: (0, 0))
        sem = ("arbitrary",) * 3
    else:
        grid = (tokens // tm, nf)
        tok = lambda w: pl.BlockSpec((tm, w), lambda i, f: (i, 0))
        in_specs = [tok(D_MODEL),
                    pl.BlockSpec((D_MODEL, tf), lambda i, f: (0, f)),
                    pl.BlockSpec((D_MODEL, tf), lambda i, f: (0, f)),
                    pl.BlockSpec((tf, D_MODEL), lambda i, f: (f, 0)),
                    tok(D_MODEL)]
        args = [h, wg, wu, wd, x2]
        const = pl.BlockSpec((1, D_MODEL), lambda i, f: (0, 0))
        sem = ("arbitrary",) * 2
    if final:
        in_specs.append(const)
        args.append(final_gain)
    return pl.pallas_call(
        functools.partial(_ffn_kernel, gated=gated, final=final),
        grid=grid, in_specs=in_specs, out_specs=tok(D_MODEL),
        out_shape=jax.ShapeDtypeStruct((tokens, D_MODEL), F32),
        scratch_shapes=[pltpu.VMEM((tm, D_MODEL), F32)],
        compiler_params=_params(sem),
        name="moe_swiglu" if gated else "dense_swiglu",
    )(*args)


def _widen_w_in(w):
    a = 3 * WIDTH_A + WIDTH_B
    kb = w[:, a:a + KV_WIDTH_B]
    vb = w[:, a + KV_WIDTH_B:]
    dup = lambda t: jnp.concatenate(
        [t[:, h * HEAD_DIM:(h + 1) * HEAD_DIM] for h in range(N_KV_B) for _ in range(2)], axis=1)
    return jnp.concatenate([w[:, :a], dup(kb), dup(vb)], axis=1).astype(BF16)


def kernel(x, positions, attn_norm, w_in, mix_norm_a, mix_norm_b, sinks, w_out, ffn_norm,
           dense_w_gate, dense_w_up, dense_w_down, router, moe_w_gate, moe_w_up, moe_w_down,
           final_norm):
    batch, seq, _ = x.shape
    depth = w_in.shape[0]
    tokens = batch * seq
    cos, sina, sinb = _rope_tables(positions)
    x2 = x.reshape(tokens, D_MODEL)
    for i in range(depth):
        qa, ka, va, qb, kb, vb = _inproj(x2, attn_norm[i][None, :], _widen_w_in(w_in[i]),
                                         cos, sina, sinb, seq)
        os_, lses = [], []
        for _, d in DILATED_BRANCHES:
            view = lambda t: t.reshape(batch, seq // d, d * WIDTH_A)
            o, lse = _band_attention(view(qa), view(ka), view(va), BLK, lambda c: c)
            os_.append(o.reshape(tokens, WIDTH_A))
            lses.append(lse.reshape(tokens, WIDTH_A))
        ob, _ = _band_attention(qb.reshape(batch, seq, WIDTH_B),
                                kb.reshape(batch, seq, 2 * KV_WIDTH_B),
                                vb.reshape(batch, seq, 2 * KV_WIDTH_B),
                                SWA_WINDOW - 1, lambda c: c // 2, sinks=sinks[i])
        ob = ob.reshape(tokens, WIDTH_B)
        j = i // 2
        last = i == depth - 1
        if i % 2 == 0:
            x2, h = _outproj(os_, lses, ob, x2, mix_norm_a[i][None, :], mix_norm_b[i][None, :],
                             w_out[i].astype(BF16), ffn_norm[i][None, :])
            x2 = _ffn(h, x2, dense_w_gate[j].astype(BF16), dense_w_up[j].astype(BF16),
                      dense_w_down[j].astype(BF16),
                      final_gain=final_norm[None, :] if last else None)
        else:
            router_pad = jnp.pad(router[j], ((0, 0), (0, LANES - N_EXPERTS)))
            x2, h, gates = _outproj(os_, lses, ob, x2, mix_norm_a[i][None, :],
                                    mix_norm_b[i][None, :], w_out[i].astype(BF16),
                                    ffn_norm[i][None, :], router_pad)
            x2 = _ffn(h, x2, moe_w_gate[j].astype(BF16), moe_w_up[j].astype(BF16),
                      moe_w_down[j].astype(BF16), gates=gates,
                      final_gain=final_norm[None, :] if last else None)
    return x2.reshape(batch, seq, D_MODEL)
```

```python
import functools

import numpy as np
import jax
import jax.numpy as jnp
from jax import lax
from jax.experimental import pallas as pl
from jax.experimental.pallas import tpu as pltpu

D_MODEL = 1024
HEAD_DIM = 64
N_HEADS_A = 8
N_HEADS_B = 8
N_KV_B = 2
DILATED_BRANCHES = ((128, 1), (512, 4), (2048, 16))
SWA_WINDOW = 128
BLK = 128
ROPE_THETA = 10000.0
D_FF = 3584
N_EXPERTS = 8
RMS_EPS = 1e-5
WIDTH_A = N_HEADS_A * HEAD_DIM
WIDTH_B = N_HEADS_B * HEAD_DIM
KV_WIDTH_B = N_KV_B * HEAD_DIM

LANES = 128
NEG_BIG = -1e30
VMEM_LIMIT = 56 * 1024 * 1024

F32 = jnp.float32
BF16 = jnp.bfloat16


def _params(sem):
    return pltpu.CompilerParams(dimension_semantics=sem, vmem_limit_bytes=VMEM_LIMIT)


def _rms(x, g):
    return x * lax.rsqrt(jnp.mean(x * x, axis=-1, keepdims=True) + RMS_EPS) * g


def _rope_table_kernel(pos_ref, inv_ref, cos_ref, sina_ref, sinb_ref):
    ang = pos_ref[...].astype(F32) * inv_ref[...]
    c = jnp.cos(ang)
    s = jnp.sin(ang)
    lane = lax.broadcasted_iota(jnp.int32, ang.shape, 1)
    upper = (lane & (HEAD_DIM - 1)) >= HEAD_DIM // 2
    cos_ref[...] = c
    sina_ref[...] = jnp.where(upper, s, 0.0)
    sinb_ref[...] = jnp.where(upper, 0.0, -s)


def _rope_tables(positions):
    seq = positions.shape[0]
    inv = ROPE_THETA ** (-jnp.arange(0, HEAD_DIM, 2, dtype=F32) / HEAD_DIM)
    inv = jnp.tile(inv, LANES // (HEAD_DIM // 2))[None, :]
    rows = 1024
    tab = jax.ShapeDtypeStruct((seq, LANES), F32)
    return pl.pallas_call(
        _rope_table_kernel,
        grid=(seq // rows,),
        in_specs=[pl.BlockSpec((rows, 1), lambda i: (i, 0)),
                  pl.BlockSpec((1, LANES), lambda i: (0, 0))],
        out_specs=[pl.BlockSpec((rows, LANES), lambda i: (i, 0))] * 3,
        out_shape=[tab, tab, tab],
        compiler_params=_params(("arbitrary",)),
        name="rope_tables",
    )(positions.reshape(seq, 1), inv)


_SEGS = (("qa", WIDTH_A, True, HEAD_DIM ** -0.5), ("ka", WIDTH_A, True, 1.0),
         ("va", WIDTH_A, False, 1.0), ("qb", WIDTH_B, True, HEAD_DIM ** -0.5),
         ("kb", 2 * KV_WIDTH_B, True, 1.0), ("vb", 2 * KV_WIDTH_B, False, 1.0))


def _inproj_kernel(x_ref, g_ref, w_ref, cos_ref, sina_ref, sinb_ref, *out_refs):
    h = _rms(x_ref[...], g_ref[...]).astype(BF16)
    cos = cos_ref[...]
    sina = sina_ref[...]
    sinb = sinb_ref[...]
    col = 0
    for (_, width, roped, scale), o_ref in zip(_SEGS, out_refs):
        t = jnp.dot(h, w_ref[:, col:col + width], preferred_element_type=F32)
        col += width
        for j in range(width // LANES):
            tj = t[:, j * LANES:(j + 1) * LANES]
            if roped:
                tj = (tj * cos + pltpu.roll(tj, HEAD_DIM // 2, 1) * sina
                      + pltpu.roll(tj, LANES - HEAD_DIM // 2, 1) * sinb)
                if scale != 1.0:
                    tj = tj * scale
            o_ref[:, j * LANES:(j + 1) * LANES] = tj.astype(o_ref.dtype)


def _inproj(x2, g, w_ext, cos, sina, sinb, seq):
    tokens = x2.shape[0]
    tm = 512
    spt = seq // tm
    widths = [s[1] for s in _SEGS]
    total = sum(widths)
    return pl.pallas_call(
        _inproj_kernel,
        grid=(tokens // tm,),
        in_specs=[pl.BlockSpec((tm, D_MODEL), lambda i: (i, 0)),
                  pl.BlockSpec((1, D_MODEL), lambda i: (0, 0)),
                  pl.BlockSpec((D_MODEL, total), lambda i: (0, 0)),
                  pl.BlockSpec((tm, LANES), lambda i: (i % spt, 0)),
                  pl.BlockSpec((tm, LANES), lambda i: (i % spt, 0)),
                  pl.BlockSpec((tm, LANES), lambda i: (i % spt, 0))],
        out_specs=[pl.BlockSpec((tm, w), lambda i: (i, 0)) for w in widths],
        out_shape=[jax.ShapeDtypeStruct((tokens, w), BF16) for w in widths],
        compiler_params=_params(("arbitrary",)),
        name="inproj_rope",
    )(x2, g, w_ext, cos, sina, sinb)


def _band_bias(max_dist):
    p = np.arange(BLK)[:, None]
    c = np.arange(2 * BLK)[None, :]
    first = c <= p
    dist = BLK + p - c
    main = (dist >= 0) & (dist <= max_dist)
    bias = np.where(np.stack([first, main]), 0.0, NEG_BIG).astype(np.float32)
    return jnp.asarray(np.concatenate([bias, bias], axis=1))


def _low_lanes():
    return lax.broadcasted_iota(jnp.int32, (BLK, LANES), 1) < HEAD_DIM


def _band_block(q, kb, vb, bias, sink=None):
    low = _low_lanes()
    zero = jnp.zeros_like(q)
    qq = jnp.concatenate([jnp.where(low, q, zero), jnp.where(low, zero, q)], axis=0)
    s = lax.dot_general(qq, kb, (((1,), (1,)), ((), ())), preferred_element_type=F32) + bias
    m = jnp.max(s, axis=1, keepdims=True)
    if sink is not None:
        m = jnp.maximum(m, sink)
    p = jnp.exp(s - m)
    l = jnp.sum(p, axis=1, keepdims=True)
    if sink is not None:
        l = l + jnp.exp(sink - m)
    o = jnp.dot(p.astype(BF16), vb, preferred_element_type=F32) * (1.0 / l)
    lse = m + jnp.log(l)
    o_pair = jnp.where(low, o[:BLK], o[BLK:])
    lse_pair = jnp.where(low, jnp.broadcast_to(lse[:BLK], (BLK, LANES)),
                         jnp.broadcast_to(lse[BLK:], (BLK, LANES)))
    return o_pair, lse_pair


def _merge(o_a, lse_a, o_b, lse_b, want_lse=True):
    mx = jnp.maximum(lse_a, lse_b)
    ea = jnp.exp(lse_a - mx)
    eb = jnp.exp(lse_b - mx)
    den = ea + eb
    o = (ea * o_a + eb * o_b) * (1.0 / den)
    return (o, mx + jnp.log(den)) if want_lse else (o, None)


_CLASSES = 4
_LOOP_UNROLL = 8


def _dilated_kernel(q_ref, k_ref, v_ref, bias_ref, o_ref,
                    q4, k4, v4, tmp, so4, sl4, il_o, il_l, *, seq):
    cls = seq // _CLASSES
    nb4 = cls // BLK
    nb16 = seq // 16 // BLK
    sh4, sh16 = nb4.bit_length() - 1, nb16.bit_length() - 1
    chunk = 512
    shc = (cls // chunk).bit_length() - 1

    for src, dst in ((q_ref, q4), (k_ref, k4), (v_ref, v4)):
        def widen(i, c, src=src):
            r0 = pl.multiple_of(i * chunk, chunk)
            tmp[pl.ds(r0, chunk), :] = src[0, pl.ds(r0, chunk), :].astype(F32)
            return c
        lax.fori_loop(0, seq // chunk, widen, 0)

        def regroup(i, c, dst=dst):
            r = lax.shift_right_logical(i, shc)
            j = i & (cls // chunk - 1)
            d0 = pl.multiple_of(r * cls + j * chunk, chunk)
            dst[pl.ds(d0, chunk), :] = tmp[pl.ds(r + _CLASSES * chunk * j, chunk,
                                                 stride=_CLASSES), :]
            return c
        lax.fori_loop(0, seq // chunk, regroup, 0)

    def dil4(idx, c):
        r = lax.shift_right_logical(idx, sh4)
        n = idx & (nb4 - 1)
        q0 = pl.multiple_of(r * cls + n * BLK, BLK)
        k0 = pl.multiple_of(r * cls + jnp.maximum(n - 1, 0) * BLK, BLK)
        o, lse = _band_block(q4[pl.ds(q0, BLK), :].astype(BF16),
                             k4[pl.ds(k0, 2 * BLK), :].astype(BF16),
                             v4[pl.ds(k0, 2 * BLK), :].astype(BF16),
                             bias_ref[jnp.minimum(n, 1)])
        so4[pl.ds(q0, BLK), :] = o
        sl4[pl.ds(q0, BLK), :] = lse
        return c
    lax.fori_loop(0, _CLASSES * nb4, dil4, 0, unroll=_LOOP_UNROLL)

    def dil16(idx, c):
        n = idx & (nb16 - 1)
        sub = lax.shift_right_logical(idx, sh16) & (_CLASSES - 1)
        r = lax.shift_right_logical(idx, sh16 + 2)
        q0 = r * cls + sub + _CLASSES * BLK * n
        k0 = r * cls + sub + _CLASSES * BLK * jnp.maximum(n - 1, 0)
        qs = pl.ds(q0, BLK, stride=_CLASSES)
        ks = pl.ds(k0, 2 * BLK, stride=_CLASSES)
        o_s, l_s = so4[qs, :], sl4[qs, :]
        o, lse = _band_block(q4[qs, :].astype(BF16), k4[ks, :].astype(BF16),
                             v4[ks, :].astype(BF16), bias_ref[jnp.minimum(n, 1)])
        o, lse = _merge(o_s, l_s, o, lse)
        so4[qs, :] = o
        sl4[qs, :] = lse
        return c
    lax.fori_loop(0, 16 * nb16, dil16, 0, unroll=_LOOP_UNROLL)

    piece = BLK // _CLASSES

    def dil1(n, c):
        q0 = pl.multiple_of(n * BLK, BLK)
        k0 = pl.multiple_of(jnp.maximum(n - 1, 0) * BLK, BLK)
        for r in range(_CLASSES):
            s0 = pl.multiple_of(r * cls + n * piece, piece)
            il_o[pl.ds(r, piece, stride=_CLASSES), :] = so4[pl.ds(s0, piece), :]
            il_l[pl.ds(r, piece, stride=_CLASSES), :] = sl4[pl.ds(s0, piece), :]
        o_s, l_s = il_o[...], il_l[...]
        o, lse = _band_block(q_ref[0, pl.ds(q0, BLK), :], k_ref[0, pl.ds(k0, 2 * BLK), :],
                             v_ref[0, pl.ds(k0, 2 * BLK), :], bias_ref[jnp.minimum(n, 1)])
        o, _ = _merge(o_s, l_s, o, lse, want_lse=False)
        o_ref[0, pl.ds(q0, BLK), :] = o.astype(o_ref.dtype)
        return c
    lax.fori_loop(0, seq // BLK, dil1, 0, unroll=_LOOP_UNROLL)


def _dilated_attention(q, k, v):
    assert DILATED_BRANCHES == ((128, 1), (512, 4), (2048, 16))
    batch, seq, width = q.shape
    assert seq % (16 * BLK) == 0 and (seq & (seq - 1)) == 0
    blk = pl.BlockSpec((1, seq, LANES), lambda b, c: (b, 0, c))
    rows = pltpu.VMEM((seq, LANES), F32)
    small = pltpu.VMEM((BLK, LANES), F32)
    return pl.pallas_call(
        functools.partial(_dilated_kernel, seq=seq),
        grid=(batch, width // LANES),
        in_specs=[blk, blk, blk, pl.BlockSpec((2, 2 * BLK, 2 * BLK), lambda b, c: (0, 0, 0))],
        out_specs=blk,
        out_shape=jax.ShapeDtypeStruct((batch, seq, width), BF16),
        scratch_shapes=[rows] * 6 + [small] * 2,
        compiler_params=_params(("arbitrary", "arbitrary")),
        name="dilated_attn",
    )(q, k, v, _band_bias(BLK))


def _swa_kernel(sink_ref, q_ref, k_ref, v_ref, bias_ref, o_ref, *, n_blocks):
    c = pl.program_id(1)
    row = lax.broadcasted_iota(jnp.int32, (2 * BLK, 1), 0)
    sink = jnp.where(row < BLK, sink_ref[2 * c], sink_ref[2 * c + 1])

    def body(n, carry):
        q0 = pl.multiple_of(n * BLK, BLK)
        k0 = pl.multiple_of(jnp.maximum(n - 1, 0) * BLK, BLK)
        o, _ = _band_block(q_ref[0, pl.ds(q0, BLK), :], k_ref[0, pl.ds(k0, 2 * BLK), :],
                           v_ref[0, pl.ds(k0, 2 * BLK), :], bias_ref[jnp.minimum(n, 1)], sink)
        o_ref[0, pl.ds(q0, BLK), :] = o.astype(o_ref.dtype)
        return carry
    lax.fori_loop(0, n_blocks, body, 0, unroll=_LOOP_UNROLL)


def _swa_attention(q, k, v, sinks):
    batch, seq, width = q.shape
    qblk = pl.BlockSpec((1, seq, LANES), lambda b, c, s: (b, 0, c))
    kblk = pl.BlockSpec((1, seq, LANES), lambda b, c, s: (b, 0, c // 2))
    grid_spec = pltpu.PrefetchScalarGridSpec(
        num_scalar_prefetch=1, grid=(batch, width // LANES),
        in_specs=[qblk, kblk, kblk,
                  pl.BlockSpec((2, 2 * BLK, 2 * BLK), lambda b, c, s: (0, 0, 0))],
        out_specs=qblk)
    return pl.pallas_call(
        functools.partial(_swa_kernel, n_blocks=seq // BLK), grid_spec=grid_spec,
        out_shape=jax.ShapeDtypeStruct((batch, seq, width), BF16),
        compiler_params=_params(("arbitrary", "arbitrary")),
        name="swa_sink_attn",
    )(sinks, q, k, v, _band_bias(SWA_WINDOW - 1))


def _top2_gates(logits):
    lane = lax.broadcasted_iota(jnp.int32, logits.shape, 1)
    v1 = jnp.max(logits, axis=1, keepdims=True)
    i1 = jnp.min(jnp.where(logits == v1, lane, LANES), axis=1, keepdims=True)
    rest = jnp.where(lane == i1, NEG_BIG, logits)
    v2 = jnp.max(rest, axis=1, keepdims=True)
    i2 = jnp.min(jnp.where(rest == v2, lane, LANES), axis=1, keepdims=True)
    e2 = jnp.exp(v2 - v1)
    g1 = 1.0 / (1.0 + e2)
    g2 = e2 / (1.0 + e2)
    return jnp.where(lane == i1, g1, 0.0) + jnp.where(lane == i2, g2, 0.0)


def _outproj_kernel(*refs, with_router):
    oa_ref, ob_ref, x_ref, ga_ref, gb_ref, w_ref, gf_ref = refs[:7]
    rest = refs[7:]
    if with_router:
        r_ref, xo_ref, ho_ref, gate_ref = rest
    else:
        xo_ref, ho_ref = rest
    na = _rms(oa_ref[...].astype(F32), ga_ref[...]).astype(BF16)
    nb = _rms(ob_ref[...].astype(F32), gb_ref[...]).astype(BF16)
    xn = (x_ref[...] + jnp.dot(na, w_ref[:WIDTH_A, :], preferred_element_type=F32)
          + jnp.dot(nb, w_ref[WIDTH_A:, :], preferred_element_type=F32))
    xo_ref[...] = xn
    hf = _rms(xn, gf_ref[...])
    ho_ref[...] = hf.astype(BF16)
    if with_router:
        logits = jnp.dot(hf, r_ref[...], preferred_element_type=F32,
                         precision=lax.Precision.HIGHEST)
        lane = lax.broadcasted_iota(jnp.int32, logits.shape, 1)
        logits = jnp.where(lane < N_EXPERTS, logits, NEG_BIG)
        gate_ref[...] = _top2_gates(logits)


def _outproj(oa, ob, x2, ga, gb, w_out, gf, router_pad=None):
    tokens = x2.shape[0]
    tm = 512
    with_router = router_pad is not None
    row = lambda w: pl.BlockSpec((tm, w), lambda i: (i, 0))
    const = lambda a, b: pl.BlockSpec((a, b), lambda i: (0, 0))
    in_specs = [row(WIDTH_A), row(WIDTH_B), row(D_MODEL), const(1, WIDTH_A), const(1, WIDTH_B),
                const(D_MODEL, D_MODEL), const(1, D_MODEL)]
    out_specs = [row(D_MODEL), row(D_MODEL)]
    out_shape = [jax.ShapeDtypeStruct((tokens, D_MODEL), F32),
                 jax.ShapeDtypeStruct((tokens, D_MODEL), BF16)]
    args = [oa, ob, x2, ga, gb, w_out, gf]
    if with_router:
        in_specs.append(const(D_MODEL, LANES))
        out_specs.append(row(LANES))
        out_shape.append(jax.ShapeDtypeStruct((tokens, LANES), F32))
        args.append(router_pad)
    return pl.pallas_call(
        functools.partial(_outproj_kernel, with_router=with_router),
        grid=(tokens // tm,), in_specs=in_specs, out_specs=out_specs, out_shape=out_shape,
        compiler_params=_params(("arbitrary",)),
        name="mix_outproj_router" if with_router else "mix_outproj",
    )(*args)


def _silu(x):
    return x * (1.0 / (1.0 + jnp.exp(-x)))


def _ffn_kernel(*refs, gated, final):
    if gated:
        h_ref, gate_ref, wg_ref, wu_ref, wd_ref, x_ref = refs[:6]
        rest = refs[6:]
    else:
        h_ref, wg_ref, wu_ref, wd_ref, x_ref = refs[:5]
        rest = refs[5:]
    if final:
        gfin_ref, o_ref, acc_ref = rest
    else:
        o_ref, acc_ref = rest
    if gated:
        e = pl.program_id(1)
        f = pl.program_id(2)
        first = (e == 0) & (f == 0)
        last = (e == pl.num_programs(1) - 1) & (f == pl.num_programs(2) - 1)
    else:
        f = pl.program_id(1)
        first = f == 0
        last = f == pl.num_programs(1) - 1

    @pl.when(first)
    def _():
        acc_ref[...] = x_ref[...]

    h = h_ref[...]
    wg = wg_ref[0] if gated else wg_ref[...]
    wu = wu_ref[0] if gated else wu_ref[...]
    wd = wd_ref[0] if gated else wd_ref[...]
    g = jnp.dot(h, wg, preferred_element_type=F32)
    u = jnp.dot(h, wu, preferred_element_type=F32)
    a = _silu(g) * u
    if gated:
        gates = gate_ref[...]
        lane = lax.broadcasted_iota(jnp.int32, gates.shape, 1)
        a = a * jnp.sum(jnp.where(lane == e, gates, 0.0), axis=1, keepdims=True)
    acc_ref[...] += jnp.dot(a.astype(BF16), wd, preferred_element_type=F32)

    @pl.when(last)
    def _():
        if final:
            o_ref[...] = _rms(acc_ref[...], gfin_ref[...])
        else:
            o_ref[...] = acc_ref[...]


def _ffn(h, x2, wg, wu, wd, gates=None, final_gain=None):
    tokens = x2.shape[0]
    tm, tf = 1024, 512
    gated = gates is not None
    final = final_gain is not None
    nf = D_FF // tf
    if gated:
        grid = (tokens // tm, N_EXPERTS, nf)
        tok = lambda w: pl.BlockSpec((tm, w), lambda i, e, f: (i, 0))
        in_specs = [tok(D_MODEL), tok(LANES),
                    pl.BlockSpec((1, D_MODEL, tf), lambda i, e, f: (e, 0, f)),
                    pl.BlockSpec((1, D_MODEL, tf), lambda i, e, f: (e, 0, f)),
                    pl.BlockSpec((1, tf, D_MODEL), lambda i, e, f: (e, f, 0)),
                    tok(D_MODEL)]
        args = [h, gates, wg, wu, wd, x2]
        const = pl.BlockSpec((1, D_MODEL), lambda i, e, f: (0, 0))
        sem = ("arbitrary",) * 3
    else:
        grid = (tokens // tm, nf)
        tok = lambda w: pl.BlockSpec((tm, w), lambda i, f: (i, 0))
        in_specs = [tok(D_MODEL),
                    pl.BlockSpec((D_MODEL, tf), lambda i, f: (0, f)),
                    pl.BlockSpec((D_MODEL, tf), lambda i, f: (0, f)),
                    pl.BlockSpec((tf, D_MODEL), lambda i, f: (f, 0)),
                    tok(D_MODEL)]
        args = [h, wg, wu, wd, x2]
        const = pl.BlockSpec((1, D_MODEL), lambda i, f: (0, 0))
        sem = ("arbitrary",) * 2
    if final:
        in_specs.append(const)
        args.append(final_gain)
    return pl.pallas_call(
        functools.partial(_ffn_kernel, gated=gated, final=final),
        grid=grid, in_specs=in_specs, out_specs=tok(D_MODEL),
        out_shape=jax.ShapeDtypeStruct((tokens, D_MODEL), F32),
        scratch_shapes=[pltpu.VMEM((tm, D_MODEL), F32)],
        compiler_params=_params(sem),
        name="moe_swiglu" if gated else "dense_swiglu",
    )(*args)


def _widen_w_in(w):
    a = 3 * WIDTH_A + WIDTH_B
    kb = w[:, a:a + KV_WIDTH_B]
    vb = w[:, a + KV_WIDTH_B:]
    dup = lambda t: jnp.concatenate(
        [t[:, h * HEAD_DIM:(h + 1) * HEAD_DIM] for h in range(N_KV_B) for _ in range(2)], axis=1)
    return jnp.concatenate([w[:, :a], dup(kb), dup(vb)], axis=1).astype(BF16)


def kernel(x, positions, attn_norm, w_in, mix_norm_a, mix_norm_b, sinks, w_out, ffn_norm,
           dense_w_gate, dense_w_up, dense_w_down, router, moe_w_gate, moe_w_up, moe_w_down,
           final_norm):
    batch, seq, _ = x.shape
    depth = w_in.shape[0]
    tokens = batch * seq
    cos, sina, sinb = _rope_tables(positions)
    x2 = x.reshape(tokens, D_MODEL)
    seq3 = lambda t: t.reshape(batch, seq, t.shape[-1])
    for i in range(depth):
        qa, ka, va, qb, kb, vb = _inproj(x2, attn_norm[i][None, :], _widen_w_in(w_in[i]),
                                         cos, sina, sinb, seq)
        oa = _dilated_attention(seq3(qa), seq3(ka), seq3(va)).reshape(tokens, WIDTH_A)
        ob = _swa_attention(seq3(qb), seq3(kb), seq3(vb), sinks[i]).reshape(tokens, WIDTH_B)
        j = i // 2
        last = i == depth - 1
        norms = (mix_norm_a[i][None, :], mix_norm_b[i][None, :], w_out[i].astype(BF16),
                 ffn_norm[i][None, :])
        if i % 2 == 0:
            x2, h = _outproj(oa, ob, x2, *norms)
            x2 = _ffn(h, x2, dense_w_gate[j].astype(BF16), dense_w_up[j].astype(BF16),
                      dense_w_down[j].astype(BF16),
                      final_gain=final_norm[None, :] if last else None)
        else:
            router_pad = jnp.pad(router[j], ((0, 0), (0, LANES - N_EXPERTS)))
            x2, h, gates = _outproj(oa, ob, x2, *norms, router_pad)
            x2 = _ffn(h, x2, moe_w_gate[j].astype(BF16), moe_w_up[j].astype(BF16),
                      moe_w_down[j].astype(BF16), gates=gates,
                      final_gain=final_norm[None, :] if last else None)
    return x2.reshape(batch, seq, D_MODEL)
```

```python
import functools

import numpy as np
import jax
import jax.numpy as jnp
from jax import lax
from jax.experimental import pallas as pl
from jax.experimental.pallas import tpu as pltpu

D_MODEL = 1024
HEAD_DIM = 64
N_HEADS_A = 8
N_HEADS_B = 8
N_KV_B = 2
DILATED_BRANCHES = ((128, 1), (512, 4), (2048, 16))
SWA_WINDOW = 128
BLK = 128
ROPE_THETA = 10000.0
D_FF = 3584
N_EXPERTS = 8
RMS_EPS = 1e-5
WIDTH_A = N_HEADS_A * HEAD_DIM
WIDTH_B = N_HEADS_B * HEAD_DIM
KV_WIDTH_B = N_KV_B * HEAD_DIM

LANES = 128
NEG_BIG = -1e30
VMEM_LIMIT = 56 * 1024 * 1024

F32 = jnp.float32
BF16 = jnp.bfloat16


def _params(sem):
    return pltpu.CompilerParams(dimension_semantics=sem, vmem_limit_bytes=VMEM_LIMIT)


def _rms(x, g):
    return x * lax.rsqrt(jnp.mean(x * x, axis=-1, keepdims=True) + RMS_EPS) * g


def _rope_table_kernel(pos_ref, inv_ref, cos_ref, sina_ref, sinb_ref):
    ang = pos_ref[...].astype(F32) * inv_ref[...]
    c = jnp.cos(ang)
    s = jnp.sin(ang)
    lane = lax.broadcasted_iota(jnp.int32, ang.shape, 1)
    upper = (lane & (HEAD_DIM - 1)) >= HEAD_DIM // 2
    cos_ref[...] = c
    sina_ref[...] = jnp.where(upper, s, 0.0)
    sinb_ref[...] = jnp.where(upper, 0.0, -s)


def _rope_tables(positions):
    seq = positions.shape[0]
    inv = ROPE_THETA ** (-jnp.arange(0, HEAD_DIM, 2, dtype=F32) / HEAD_DIM)
    inv = jnp.tile(inv, LANES // (HEAD_DIM // 2))[None, :]
    rows = 1024
    tab = jax.ShapeDtypeStruct((seq, LANES), F32)
    return pl.pallas_call(
        _rope_table_kernel,
        grid=(seq // rows,),
        in_specs=[pl.BlockSpec((rows, 1), lambda i: (i, 0)),
                  pl.BlockSpec((1, LANES), lambda i: (0, 0))],
        out_specs=[pl.BlockSpec((rows, LANES), lambda i: (i, 0))] * 3,
        out_shape=[tab, tab, tab],
        compiler_params=_params(("arbitrary",)),
        name="rope_tables",
    )(positions.reshape(seq, 1), inv)


_SEGS = (("qa", WIDTH_A, True, HEAD_DIM ** -0.5), ("ka", WIDTH_A, True, 1.0),
         ("va", WIDTH_A, False, 1.0), ("qb", WIDTH_B, True, HEAD_DIM ** -0.5),
         ("kb", 2 * KV_WIDTH_B, True, 1.0), ("vb", 2 * KV_WIDTH_B, False, 1.0))


def _inproj_kernel(x_ref, g_ref, w_ref, cos_ref, sina_ref, sinb_ref, *out_refs):
    h = _rms(x_ref[...], g_ref[...]).astype(BF16)
    cos = cos_ref[...]
    sina = sina_ref[...]
    sinb = sinb_ref[...]
    col = 0
    for (_, width, roped, scale), o_ref in zip(_SEGS, out_refs):
        t = jnp.dot(h, w_ref[:, col:col + width], preferred_element_type=F32)
        col += width
        for j in range(width // LANES):
            tj = t[:, j * LANES:(j + 1) * LANES]
            if roped:
                tj = (tj * cos + pltpu.roll(tj, HEAD_DIM // 2, 1) * sina
                      + pltpu.roll(tj, LANES - HEAD_DIM // 2, 1) * sinb)
                if scale != 1.0:
                    tj = tj * scale
            o_ref[:, j * LANES:(j + 1) * LANES] = tj.astype(o_ref.dtype)


def _inproj(x2, g, w_ext, cos, sina, sinb, seq):
    tokens = x2.shape[0]
    tm = 512
    spt = seq // tm
    widths = [s[1] for s in _SEGS]
    total = sum(widths)
    return pl.pallas_call(
        _inproj_kernel,
        grid=(tokens // tm,),
        in_specs=[pl.BlockSpec((tm, D_MODEL), lambda i: (i, 0)),
                  pl.BlockSpec((1, D_MODEL), lambda i: (0, 0)),
                  pl.BlockSpec((D_MODEL, total), lambda i: (0, 0)),
                  pl.BlockSpec((tm, LANES), lambda i: (i % spt, 0)),
                  pl.BlockSpec((tm, LANES), lambda i: (i % spt, 0)),
                  pl.BlockSpec((tm, LANES), lambda i: (i % spt, 0))],
        out_specs=[pl.BlockSpec((tm, w), lambda i: (i, 0)) for w in widths],
        out_shape=[jax.ShapeDtypeStruct((tokens, w), BF16) for w in widths],
        compiler_params=_params(("arbitrary",)),
        name="inproj_rope",
    )(x2, g, w_ext, cos, sina, sinb)


def _band_bias(max_dist):
    p = np.arange(BLK)[:, None]
    c = np.arange(2 * BLK)[None, :]
    first = c <= p
    dist = BLK + p - c
    main = (dist >= 0) & (dist <= max_dist)
    bias = np.where(np.stack([first, main]), 0.0, NEG_BIG).astype(np.float32)
    return jnp.asarray(np.concatenate([bias, bias], axis=1))


def _low_lanes():
    return lax.broadcasted_iota(jnp.int32, (BLK, LANES), 1) < HEAD_DIM


def _band_block(q, kb, vb, bias, sink=None):
    low = _low_lanes()
    zero = jnp.zeros_like(q)
    qq = jnp.concatenate([jnp.where(low, q, zero), jnp.where(low, zero, q)], axis=0)
    s = lax.dot_general(qq, kb, (((1,), (1,)), ((), ())), preferred_element_type=F32) + bias
    m = jnp.max(s, axis=1, keepdims=True)
    if sink is not None:
        m = jnp.maximum(m, sink)
    p = jnp.exp(s - m)
    l = jnp.sum(p, axis=1, keepdims=True)
    if sink is not None:
        l = l + jnp.exp(sink - m)
    o = jnp.dot(p.astype(BF16), vb, preferred_element_type=F32) * (1.0 / l)
    lse = m + jnp.log(l)
    o_pair = jnp.where(low, o[:BLK], o[BLK:])
    lse_pair = jnp.where(low, jnp.broadcast_to(lse[:BLK], (BLK, LANES)),
                         jnp.broadcast_to(lse[BLK:], (BLK, LANES)))
    return o_pair, lse_pair


def _merge(o_a, lse_a, o_b, lse_b, want_lse=True):
    mx = jnp.maximum(lse_a, lse_b)
    ea = jnp.exp(lse_a - mx)
    eb = jnp.exp(lse_b - mx)
    den = ea + eb
    o = (ea * o_a + eb * o_b) * (1.0 / den)
    return (o, mx + jnp.log(den)) if want_lse else (o, None)


_CLASSES = 4
_LOOP_UNROLL = 8


def _dilated_kernel(q_ref, k_ref, v_ref, bias_ref, o_ref,
                    q4, k4, v4, tmp, so4, sl4, il_o, il_l, *, seq):
    cls = seq // _CLASSES
    nb4 = cls // BLK
    nb16 = seq // 16 // BLK
    sh4, sh16 = nb4.bit_length() - 1, nb16.bit_length() - 1
    chunk = 512
    shc = (cls // chunk).bit_length() - 1

    for src, dst in ((q_ref, q4), (k_ref, k4), (v_ref, v4)):
        def widen(i, c, src=src):
            r0 = pl.multiple_of(i * chunk, chunk)
            tmp[pl.ds(r0, chunk), :] = src[0, pl.ds(r0, chunk), :].astype(F32)
            return c
        lax.fori_loop(0, seq // chunk, widen, 0)

        def regroup(i, c, dst=dst):
            r = lax.shift_right_logical(i, shc)
            j = i & (cls // chunk - 1)
            d0 = pl.multiple_of(r * cls + j * chunk, chunk)
            dst[pl.ds(d0, chunk), :] = tmp[pl.ds(r + _CLASSES * chunk * j, chunk,
                                                 stride=_CLASSES), :]
            return c
        lax.fori_loop(0, seq // chunk, regroup, 0)

    def dil4(idx, c):
        r = lax.shift_right_logical(idx, sh4)
        n = idx & (nb4 - 1)
        q0 = pl.multiple_of(r * cls + n * BLK, BLK)
        k0 = pl.multiple_of(r * cls + jnp.maximum(n - 1, 0) * BLK, BLK)
        o, lse = _band_block(q4[pl.ds(q0, BLK), :].astype(BF16),
                             k4[pl.ds(k0, 2 * BLK), :].astype(BF16),
                             v4[pl.ds(k0, 2 * BLK), :].astype(BF16),
                             bias_ref[jnp.minimum(n, 1)])
        so4[pl.ds(q0, BLK), :] = o
        sl4[pl.ds(q0, BLK), :] = lse
        return c
    lax.fori_loop(0, _CLASSES * nb4, dil4, 0, unroll=_LOOP_UNROLL)

    def dil16(idx, c):
        n = idx & (nb16 - 1)
        sub = lax.shift_right_logical(idx, sh16) & (_CLASSES - 1)
        r = lax.shift_right_logical(idx, sh16 + 2)
        q0 = r * cls + sub + _CLASSES * BLK * n
        k0 = r * cls + sub + _CLASSES * BLK * jnp.maximum(n - 1, 0)
        qs = pl.ds(q0, BLK, stride=_CLASSES)
        ks = pl.ds(k0, 2 * BLK, stride=_CLASSES)
        o_s, l_s = so4[qs, :], sl4[qs, :]
        o, lse = _band_block(q4[qs, :].astype(BF16), k4[ks, :].astype(BF16),
                             v4[ks, :].astype(BF16), bias_ref[jnp.minimum(n, 1)])
        o, lse = _merge(o_s, l_s, o, lse)
        so4[qs, :] = o
        sl4[qs, :] = lse
        return c
    lax.fori_loop(0, 16 * nb16, dil16, 0, unroll=_LOOP_UNROLL)

    piece = BLK // _CLASSES

    def dil1(n, c):
        q0 = pl.multiple_of(n * BLK, BLK)
        k0 = pl.multiple_of(jnp.maximum(n - 1, 0) * BLK, BLK)
        for r in range(_CLASSES):
            s0 = pl.multiple_of(r * cls + n * piece, piece)
            il_o[pl.ds(r, piece, stride=_CLASSES), :] = so4[pl.ds(s0, piece), :]
            il_l[pl.ds(r, piece, stride=_CLASSES), :] = sl4[pl.ds(s0, piece), :]
        o_s, l_s = il_o[...], il_l[...]
        o, lse = _band_block(q_ref[0, pl.ds(q0, BLK), :], k_ref[0, pl.ds(k0, 2 * BLK), :],
                             v_ref[0, pl.ds(k0, 2 * BLK), :], bias_ref[jnp.minimum(n, 1)])
        o, _ = _merge(o_s, l_s, o, lse, want_lse=False)
        o_ref[0, pl.ds(q0, BLK), :] = o.astype(o_ref.dtype)
        return c
    lax.fori_loop(0, seq // BLK, dil1, 0, unroll=_LOOP_UNROLL)


def _dilated_attention(q, k, v):
    assert DILATED_BRANCHES == ((128, 1), (512, 4), (2048, 16))
    batch, seq, width = q.shape
    assert seq % (16 * BLK) == 0 and (seq & (seq - 1)) == 0
    blk = pl.BlockSpec((1, seq, LANES), lambda b, c: (b, 0, c))
    rows = pltpu.VMEM((seq, LANES), F32)
    small = pltpu.VMEM((BLK, LANES), F32)
    return pl.pallas_call(
        functools.partial(_dilated_kernel, seq=seq),
        grid=(batch, width // LANES),
        in_specs=[blk, blk, blk, pl.BlockSpec((2, 2 * BLK, 2 * BLK), lambda b, c: (0, 0, 0))],
        out_specs=blk,
        out_shape=jax.ShapeDtypeStruct((batch, seq, width), BF16),
        scratch_shapes=[rows] * 6 + [small] * 2,
        compiler_params=_params(("arbitrary", "arbitrary")),
        name="dilated_attn",
    )(q, k, v, _band_bias(BLK))


def _swa_kernel(sink_ref, q_ref, k_ref, v_ref, bias_ref, o_ref, *, n_blocks):
    c = pl.program_id(1)
    row = lax.broadcasted_iota(jnp.int32, (2 * BLK, 1), 0)
    sink = jnp.where(row < BLK, sink_ref[2 * c], sink_ref[2 * c + 1])

    def body(n, carry):
        q0 = pl.multiple_of(n * BLK, BLK)
        k0 = pl.multiple_of(jnp.maximum(n - 1, 0) * BLK, BLK)
        o, _ = _band_block(q_ref[0, pl.ds(q0, BLK), :], k_ref[0, pl.ds(k0, 2 * BLK), :],
                           v_ref[0, pl.ds(k0, 2 * BLK), :], bias_ref[jnp.minimum(n, 1)], sink)
        o_ref[0, pl.ds(q0, BLK), :] = o.astype(o_ref.dtype)
        return carry
    lax.fori_loop(0, n_blocks, body, 0, unroll=_LOOP_UNROLL)


def _swa_attention(q, k, v, sinks):
    batch, seq, width = q.shape
    qblk = pl.BlockSpec((1, seq, LANES), lambda b, c, s: (b, 0, c))
    kblk = pl.BlockSpec((1, seq, LANES), lambda b, c, s: (b, 0, c // 2))
    grid_spec = pltpu.PrefetchScalarGridSpec(
        num_scalar_prefetch=1, grid=(batch, width // LANES),
        in_specs=[qblk, kblk, kblk,
                  pl.BlockSpec((2, 2 * BLK, 2 * BLK), lambda b, c, s: (0, 0, 0))],
        out_specs=qblk)
    return pl.pallas_call(
        functools.partial(_swa_kernel, n_blocks=seq // BLK), grid_spec=grid_spec,
        out_shape=jax.ShapeDtypeStruct((batch, seq, width), BF16),
        compiler_params=_params(("arbitrary", "arbitrary")),
        name="swa_sink_attn",
    )(sinks, q, k, v, _band_bias(SWA_WINDOW - 1))


_R_E1, _R_E2, _R_G1, _R_G2, _R_RANK1, _R_RANK2 = range(6)


def _route(logits, count_ref):
    tm = logits.shape[0]
    lane = lax.broadcasted_iota(jnp.int32, logits.shape, 1)
    v1 = jnp.max(logits, axis=1, keepdims=True)
    i1 = jnp.min(jnp.where(logits == v1, lane, LANES), axis=1, keepdims=True)
    rest = jnp.where(lane == i1, NEG_BIG, logits)
    v2 = jnp.max(rest, axis=1, keepdims=True)
    i2 = jnp.min(jnp.where(rest == v2, lane, LANES), axis=1, keepdims=True)
    e2 = jnp.exp(v2 - v1)
    g1 = 1.0 / (1.0 + e2)
    g2 = e2 / (1.0 + e2)
    sel = (lane == i1) | (lane == i2)
    r = lax.broadcasted_iota(jnp.int32, (tm, tm), 0)
    c = lax.broadcasted_iota(jnp.int32, (tm, tm), 1)
    tri = jnp.where(c < r, 1.0, 0.0).astype(BF16)
    before = jnp.dot(tri, jnp.where(sel, 1.0, 0.0).astype(BF16),
                     preferred_element_type=F32) + count_ref[...]
    rank1 = jnp.sum(jnp.where(lane == i1, before, 0.0), axis=1, keepdims=True)
    rank2 = jnp.sum(jnp.where(lane == i2, before, 0.0), axis=1, keepdims=True)
    count_ref[...] += jnp.sum(jnp.where(sel, 1.0, 0.0), axis=0, keepdims=True)
    rec = jnp.zeros(logits.shape, F32)
    for ln, col in ((_R_E1, i1.astype(F32)), (_R_E2, i2.astype(F32)), (_R_G1, g1), (_R_G2, g2),
                    (_R_RANK1, rank1), (_R_RANK2, rank2)):
        rec = jnp.where(lane == ln, col, rec)
    return rec


def _outproj_kernel(*refs, with_router):
    oa_ref, ob_ref, x_ref, ga_ref, gb_ref, w_ref, gf_ref = refs[:7]
    rest = refs[7:]
    if with_router:
        r_ref, xo_ref, ho_ref, route_ref, count_ref = rest
    else:
        xo_ref, ho_ref = rest
    na = _rms(oa_ref[...].astype(F32), ga_ref[...]).astype(BF16)
    nb = _rms(ob_ref[...].astype(F32), gb_ref[...]).astype(BF16)
    xn = (x_ref[...] + jnp.dot(na, w_ref[:WIDTH_A, :], preferred_element_type=F32)
          + jnp.dot(nb, w_ref[WIDTH_A:, :], preferred_element_type=F32))
    xo_ref[...] = xn
    hf = _rms(xn, gf_ref[...])
    ho_ref[...] = hf.astype(ho_ref.dtype)
    if with_router:
        @pl.when(pl.program_id(0) == 0)
        def _():
            count_ref[...] = jnp.zeros_like(count_ref)
        logits = jnp.dot(hf, r_ref[...], preferred_element_type=F32,
                         precision=lax.Precision.HIGHEST)
        lane = lax.broadcasted_iota(jnp.int32, logits.shape, 1)
        logits = jnp.where(lane < N_EXPERTS, logits, NEG_BIG)
        route_ref[...] = _route(logits, count_ref)


def _outproj(oa, ob, x2, ga, gb, w_out, gf, router_pad=None):
    tokens = x2.shape[0]
    tm = 512
    with_router = router_pad is not None
    row = lambda w: pl.BlockSpec((tm, w), lambda i: (i, 0))
    const = lambda a, b: pl.BlockSpec((a, b), lambda i: (0, 0))
    in_specs = [row(WIDTH_A), row(WIDTH_B), row(D_MODEL), const(1, WIDTH_A), const(1, WIDTH_B),
                const(D_MODEL, D_MODEL), const(1, D_MODEL)]
    out_specs = [row(D_MODEL), row(D_MODEL)]
    out_shape = [jax.ShapeDtypeStruct((tokens, D_MODEL), F32),
                 jax.ShapeDtypeStruct((tokens, D_MODEL), F32 if with_router else BF16)]
    args = [oa, ob, x2, ga, gb, w_out, gf]
    if with_router:
        in_specs.append(const(D_MODEL, LANES))
        out_specs += [row(LANES), const(1, LANES)]
        out_shape += [jax.ShapeDtypeStruct((tokens, LANES), F32),
                      jax.ShapeDtypeStruct((1, LANES), F32)]
        args.append(router_pad)
    return pl.pallas_call(
        functools.partial(_outproj_kernel, with_router=with_router),
        grid=(tokens // tm,), in_specs=in_specs, out_specs=out_specs, out_shape=out_shape,
        compiler_params=_params(("arbitrary",)),
        name="mix_outproj_router" if with_router else "mix_outproj",
    )(*args)


def _silu(x):
    return x * (1.0 / (1.0 + jnp.exp(-x)))


FFN_TM = 1024
FFN_TF = 512
ROW_TM = 512


def _swiglu_chunk(h, wg, wu, wd):
    g = jnp.dot(h, wg, preferred_element_type=F32)
    u = jnp.dot(h, wu, preferred_element_type=F32)
    return jnp.dot((_silu(g) * u).astype(BF16), wd, preferred_element_type=F32)


def _ffn_kernel(*refs, final):
    h_ref, wg_ref, wu_ref, wd_ref, x_ref = refs[:5]
    if final:
        gfin_ref, o_ref, acc_ref = refs[5:]
    else:
        o_ref, acc_ref = refs[5:]
    f = pl.program_id(1)

    @pl.when(f == 0)
    def _():
        acc_ref[...] = x_ref[...]

    acc_ref[...] += _swiglu_chunk(h_ref[...], wg_ref[...], wu_ref[...], wd_ref[...])

    @pl.when(f == pl.num_programs(1) - 1)
    def _():
        o_ref[...] = _rms(acc_ref[...], gfin_ref[...]) if final else acc_ref[...]


def _ffn(h, x2, wg, wu, wd, final_gain=None):
    tokens = x2.shape[0]
    tm, tf = FFN_TM, FFN_TF
    final = final_gain is not None
    tok = lambda w: pl.BlockSpec((tm, w), lambda i, f: (i, 0))
    in_specs = [tok(D_MODEL),
                pl.BlockSpec((D_MODEL, tf), lambda i, f: (0, f)),
                pl.BlockSpec((D_MODEL, tf), lambda i, f: (0, f)),
                pl.BlockSpec((tf, D_MODEL), lambda i, f: (f, 0)),
                tok(D_MODEL)]
    args = [h, wg, wu, wd, x2]
    if final:
        in_specs.append(pl.BlockSpec((1, D_MODEL), lambda i, f: (0, 0)))
        args.append(final_gain)
    return pl.pallas_call(
        functools.partial(_ffn_kernel, final=final),
        grid=(tokens // tm, D_FF // tf), in_specs=in_specs, out_specs=tok(D_MODEL),
        out_shape=jax.ShapeDtypeStruct((tokens, D_MODEL), F32),
        scratch_shapes=[pltpu.VMEM((tm, D_MODEL), F32)],
        compiler_params=_params(("arbitrary", "arbitrary")),
        name="dense_swiglu",
    )(*args)


def _row_copy(src_ref, src_row, dst_ref, dst_row, sem):
    return pltpu.make_async_copy(src_ref.at[pl.ds(src_row, 1)], dst_ref.at[pl.ds(dst_row, 1)], sem)


def _dispatch_kernel(slot_ref, h_ref, init_ref, out_ref, sem):
    del init_ref
    tm = h_ref.shape[0]

    def issue(r, c):
        _row_copy(h_ref, r, out_ref, slot_ref[0, 0, 2 * r], sem).start()
        _row_copy(h_ref, r, out_ref, slot_ref[0, 0, 2 * r + 1], sem).start()
        return c
    lax.fori_loop(0, tm, issue, 0, unroll=8)
    for _ in range(2):
        pltpu.make_async_copy(h_ref, out_ref.at[pl.ds(0, tm)], sem).wait()


def _dispatch(h, slots, n_slots):
    tokens = h.shape[0]
    tm = ROW_TM
    slots3 = slots.reshape(tokens // tm, 1, 2 * tm)
    return pl.pallas_call(
        _dispatch_kernel,
        grid=(tokens // tm,),
        in_specs=[pl.BlockSpec((1, 1, 2 * tm), lambda i: (i, 0, 0), memory_space=pltpu.SMEM),
                  pl.BlockSpec((tm, D_MODEL), lambda i: (i, 0)),
                  pl.BlockSpec(memory_space=pl.ANY)],
        out_specs=pl.BlockSpec(memory_space=pl.ANY),
        out_shape=jax.ShapeDtypeStruct((n_slots, D_MODEL), F32),
        scratch_shapes=[pltpu.SemaphoreType.DMA(())],
        input_output_aliases={2: 0},
        compiler_params=_params(("arbitrary",)),
        name="moe_dispatch",
    )(slots3, h, jnp.zeros((n_slots, D_MODEL), F32))


def _grouped_kernel(texp_ref, nt_ref, h_ref, wg_ref, wu_ref, wd_ref, y_ref, hb_ref, acc_ref):
    del texp_ref
    i = pl.program_id(0)
    f = pl.program_id(1)

    @pl.when(i < nt_ref[0])
    def _():
        @pl.when(f == 0)
        def _():
            hb_ref[...] = h_ref[...].astype(BF16)
            acc_ref[...] = jnp.zeros_like(acc_ref)

        acc_ref[...] += _swiglu_chunk(hb_ref[...], wg_ref[0], wu_ref[0], wd_ref[0])

        @pl.when(f == pl.num_programs(1) - 1)
        def _():
            y_ref[...] = acc_ref[...]

    @pl.when((i >= nt_ref[0]) & (f == 0))
    def _():
        y_ref[...] = jnp.zeros_like(y_ref)


def _grouped_swiglu(hs, tile_expert, n_tiles_used, wg, wu, wd):
    n_slots = hs.shape[0]
    tm, tf = FFN_TM, FFN_TF
    nf = D_FF // tf
    last = lambda i, nt: jnp.minimum(i, nt[0] - 1)
    fcol = lambda i, f, nt: jnp.where(i < nt[0], f, nf - 1)
    grid_spec = pltpu.PrefetchScalarGridSpec(
        num_scalar_prefetch=2, grid=(n_slots // tm, nf),
        in_specs=[pl.BlockSpec((tm, D_MODEL), lambda i, f, te, nt: (last(i, nt), 0)),
                  pl.BlockSpec((1, D_MODEL, tf), lambda i, f, te, nt: (te[i], 0, fcol(i, f, nt))),
                  pl.BlockSpec((1, D_MODEL, tf), lambda i, f, te, nt: (te[i], 0, fcol(i, f, nt))),
                  pl.BlockSpec((1, tf, D_MODEL), lambda i, f, te, nt: (te[i], fcol(i, f, nt), 0))],
        out_specs=pl.BlockSpec((tm, D_MODEL), lambda i, f, te, nt: (i, 0)),
        scratch_shapes=[pltpu.VMEM((tm, D_MODEL), BF16), pltpu.VMEM((tm, D_MODEL), F32)])
    return pl.pallas_call(
        _grouped_kernel, grid_spec=grid_spec,
        out_shape=jax.ShapeDtypeStruct((n_slots, D_MODEL), F32),
        compiler_params=_params(("arbitrary", "arbitrary")),
        name="moe_grouped_swiglu",
    )(tile_expert, n_tiles_used, hs, wg, wu, wd)


def _combine_kernel(*refs, final):
    slot_ref, route_ref, x_ref, y_ref = refs[:4]
    if final:
        gfin_ref, o_ref, buf_ref, sem = refs[4:]
    else:
        o_ref, buf_ref, sem = refs[4:]
    tm = x_ref.shape[0]

    def issue(r, c):
        _row_copy(y_ref, slot_ref[0, 0, 2 * r], buf_ref.at[0], r, sem).start()
        _row_copy(y_ref, slot_ref[0, 0, 2 * r + 1], buf_ref.at[1], r, sem).start()
        return c
    lax.fori_loop(0, tm, issue, 0, unroll=8)
    for k in range(2):
        pltpu.make_async_copy(y_ref.at[pl.ds(0, tm)], buf_ref.at[k], sem).wait()
    route = route_ref[...]
    out = (x_ref[...] + route[:, _R_G1:_R_G1 + 1] * buf_ref[0]
           + route[:, _R_G2:_R_G2 + 1] * buf_ref[1])
    o_ref[...] = _rms(out, gfin_ref[...]) if final else out


def _combine(y, slots, route, x2, final_gain=None):
    tokens = x2.shape[0]
    tm = ROW_TM
    final = final_gain is not None
    slots3 = slots.reshape(tokens // tm, 1, 2 * tm)
    in_specs = [pl.BlockSpec((1, 1, 2 * tm), lambda i: (i, 0, 0), memory_space=pltpu.SMEM),
                pl.BlockSpec((tm, LANES), lambda i: (i, 0)),
                pl.BlockSpec((tm, D_MODEL), lambda i: (i, 0)),
                pl.BlockSpec(memory_space=pl.ANY)]
    args = [slots3, route, x2, y]
    if final:
        in_specs.append(pl.BlockSpec((1, D_MODEL), lambda i: (0, 0)))
        args.append(final_gain)
    return pl.pallas_call(
        functools.partial(_combine_kernel, final=final),
        grid=(tokens // tm,), in_specs=in_specs,
        out_specs=pl.BlockSpec((tm, D_MODEL), lambda i: (i, 0)),
        out_shape=jax.ShapeDtypeStruct((tokens, D_MODEL), F32),
        scratch_shapes=[pltpu.VMEM((2, tm, D_MODEL), F32), pltpu.SemaphoreType.DMA(())],
        compiler_params=_params(("arbitrary",)),
        name="moe_combine",
    )(*args)


def _routing_tables(route, counts, tokens):
    tm = FFN_TM
    n_tiles = 2 * tokens // tm + N_EXPERTS - 1
    cnt = counts[0, :N_EXPERTS].astype(jnp.int32)
    tiles = (cnt + tm - 1) // tm
    tile_end = jnp.cumsum(tiles)
    start = (tile_end - tiles) * tm
    e = route[:, _R_E1:_R_E2 + 1].astype(jnp.int32)
    rank = route[:, _R_RANK1:_R_RANK2 + 1].astype(jnp.int32)
    slots = start[e] + rank
    tile_expert = jnp.minimum(
        jnp.searchsorted(tile_end, jnp.arange(n_tiles, dtype=jnp.int32), side="right"),
        N_EXPERTS - 1).astype(jnp.int32)
    return slots, tile_expert, tile_end[-1:].astype(jnp.int32), n_tiles * tm


def _moe(h, route, counts, x2, wg, wu, wd, final_gain=None):
    slots, tile_expert, n_used, n_slots = _routing_tables(route, counts, x2.shape[0])
    hs = _dispatch(h, slots, n_slots)
    y = _grouped_swiglu(hs, tile_expert, n_used, wg, wu, wd)
    return _combine(y, slots, route, x2, final_gain)


def _widen_w_in(w):
    a = 3 * WIDTH_A + WIDTH_B
    kb = w[:, a:a + KV_WIDTH_B]
    vb = w[:, a + KV_WIDTH_B:]
    dup = lambda t: jnp.concatenate(
        [t[:, h * HEAD_DIM:(h + 1) * HEAD_DIM] for h in range(N_KV_B) for _ in range(2)], axis=1)
    return jnp.concatenate([w[:, :a], dup(kb), dup(vb)], axis=1).astype(BF16)


def kernel(x, positions, attn_norm, w_in, mix_norm_a, mix_norm_b, sinks, w_out, ffn_norm,
           dense_w_gate, dense_w_up, dense_w_down, router, moe_w_gate, moe_w_up, moe_w_down,
           final_norm):
    batch, seq, _ = x.shape
    depth = w_in.shape[0]
    tokens = batch * seq
    cos, sina, sinb = _rope_tables(positions)
    x2 = x.reshape(tokens, D_MODEL)
    seq3 = lambda t: t.reshape(batch, seq, t.shape[-1])
    for i in range(depth):
        qa, ka, va, qb, kb, vb = _inproj(x2, attn_norm[i][None, :], _widen_w_in(w_in[i]),
                                         cos, sina, sinb, seq)
        oa = _dilated_attention(seq3(qa), seq3(ka), seq3(va)).reshape(tokens, WIDTH_A)
        ob = _swa_attention(seq3(qb), seq3(kb), seq3(vb), sinks[i]).reshape(tokens, WIDTH_B)
        j = i // 2
        last = i == depth - 1
        norms = (mix_norm_a[i][None, :], mix_norm_b[i][None, :], w_out[i].astype(BF16),
                 ffn_norm[i][None, :])
        if i % 2 == 0:
            x2, h = _outproj(oa, ob, x2, *norms)
            x2 = _ffn(h, x2, dense_w_gate[j].astype(BF16), dense_w_up[j].astype(BF16),
                      dense_w_down[j].astype(BF16),
                      final_gain=final_norm[None, :] if last else None)
        else:
            router_pad = jnp.pad(router[j], ((0, 0), (0, LANES - N_EXPERTS)))
            x2, h, route, counts = _outproj(oa, ob, x2, *norms, router_pad)
            x2 = _moe(h, route, counts, x2, moe_w_gate[j].astype(BF16),
                      moe_w_up[j].astype(BF16), moe_w_down[j].astype(BF16),
                      final_gain=final_norm[None, :] if last else None)
    return x2.reshape(batch, seq, D_MODEL)
```

```python
import functools

import numpy as np
import jax
import jax.numpy as jnp
from jax import lax
from jax.experimental import pallas as pl
from jax.experimental.pallas import tpu as pltpu

D_MODEL = 1024
HEAD_DIM = 64
N_HEADS_A = 8
N_HEADS_B = 8
N_KV_B = 2
DILATED_BRANCHES = ((128, 1), (512, 4), (2048, 16))
SWA_WINDOW = 128
BLK = 128
ROPE_THETA = 10000.0
D_FF = 3584
N_EXPERTS = 8
RMS_EPS = 1e-5
WIDTH_A = N_HEADS_A * HEAD_DIM
WIDTH_B = N_HEADS_B * HEAD_DIM
KV_WIDTH_B = N_KV_B * HEAD_DIM

LANES = 128
NEG_BIG = -1e30
VMEM_LIMIT = 56 * 1024 * 1024

F32 = jnp.float32
BF16 = jnp.bfloat16


def _params(sem):
    return pltpu.CompilerParams(dimension_semantics=sem, vmem_limit_bytes=VMEM_LIMIT)


def _rms(x, g):
    return x * lax.rsqrt(jnp.mean(x * x, axis=-1, keepdims=True) + RMS_EPS) * g


def _rope_table_kernel(pos_ref, inv_ref, cos_ref, sina_ref, sinb_ref):
    ang = pos_ref[...].astype(F32) * inv_ref[...]
    c = jnp.cos(ang)
    s = jnp.sin(ang)
    lane = lax.broadcasted_iota(jnp.int32, ang.shape, 1)
    upper = (lane & (HEAD_DIM - 1)) >= HEAD_DIM // 2
    cos_ref[...] = c
    sina_ref[...] = jnp.where(upper, s, 0.0)
    sinb_ref[...] = jnp.where(upper, 0.0, -s)


def _rope_tables(positions):
    seq = positions.shape[0]
    inv = ROPE_THETA ** (-jnp.arange(0, HEAD_DIM, 2, dtype=F32) / HEAD_DIM)
    inv = jnp.tile(inv, LANES // (HEAD_DIM // 2))[None, :]
    rows = 1024
    tab = jax.ShapeDtypeStruct((seq, LANES), F32)
    return pl.pallas_call(
        _rope_table_kernel,
        grid=(seq // rows,),
        in_specs=[pl.BlockSpec((rows, 1), lambda i: (i, 0)),
                  pl.BlockSpec((1, LANES), lambda i: (0, 0))],
        out_specs=[pl.BlockSpec((rows, LANES), lambda i: (i, 0))] * 3,
        out_shape=[tab, tab, tab],
        compiler_params=_params(("arbitrary",)),
        name="rope_tables",
    )(positions.reshape(seq, 1), inv)


_SEGS = (("qa", WIDTH_A, True, HEAD_DIM ** -0.5), ("ka", WIDTH_A, True, 1.0),
         ("va", WIDTH_A, False, 1.0), ("qb", WIDTH_B, True, HEAD_DIM ** -0.5),
         ("kb", 2 * KV_WIDTH_B, True, 1.0), ("vb", 2 * KV_WIDTH_B, False, 1.0))


def _inproj_kernel(x_ref, g_ref, w_ref, cos_ref, sina_ref, sinb_ref, *out_refs):
    h = _rms(x_ref[...], g_ref[...]).astype(BF16)
    cos = cos_ref[...]
    sina = sina_ref[...]
    sinb = sinb_ref[...]
    col = 0
    for (_, width, roped, scale), o_ref in zip(_SEGS, out_refs):
        t = jnp.dot(h, w_ref[:, col:col + width], preferred_element_type=F32)
        col += width
        for j in range(width // LANES):
            tj = t[:, j * LANES:(j + 1) * LANES]
            if roped:
                tj = (tj * cos + pltpu.roll(tj, HEAD_DIM // 2, 1) * sina
                      + pltpu.roll(tj, LANES - HEAD_DIM // 2, 1) * sinb)
                if scale != 1.0:
                    tj = tj * scale
            o_ref[:, j * LANES:(j + 1) * LANES] = tj.astype(o_ref.dtype)


def _inproj(x2, g, w_ext, cos, sina, sinb, seq):
    tokens = x2.shape[0]
    tm = 512
    spt = seq // tm
    widths = [s[1] for s in _SEGS]
    total = sum(widths)
    return pl.pallas_call(
        _inproj_kernel,
        grid=(tokens // tm,),
        in_specs=[pl.BlockSpec((tm, D_MODEL), lambda i: (i, 0)),
                  pl.BlockSpec((1, D_MODEL), lambda i: (0, 0)),
                  pl.BlockSpec((D_MODEL, total), lambda i: (0, 0)),
                  pl.BlockSpec((tm, LANES), lambda i: (i % spt, 0)),
                  pl.BlockSpec((tm, LANES), lambda i: (i % spt, 0)),
                  pl.BlockSpec((tm, LANES), lambda i: (i % spt, 0))],
        out_specs=[pl.BlockSpec((tm, w), lambda i: (i, 0)) for w in widths],
        out_shape=[jax.ShapeDtypeStruct((tokens, w), BF16) for w in widths],
        compiler_params=_params(("arbitrary",)),
        name="inproj_rope",
    )(x2, g, w_ext, cos, sina, sinb)


def _band_bias(max_dist):
    p = np.arange(BLK)[:, None]
    c = np.arange(2 * BLK)[None, :]
    first = c <= p
    dist = BLK + p - c
    main = (dist >= 0) & (dist <= max_dist)
    bias = np.where(np.stack([first, main]), 0.0, NEG_BIG).astype(np.float32)
    return jnp.asarray(np.concatenate([bias, bias], axis=1))


def _low_lanes():
    return lax.broadcasted_iota(jnp.int32, (BLK, LANES), 1) < HEAD_DIM


def _band_block(q, kb, vb, bias, sink=None):
    low = _low_lanes()
    zero = jnp.zeros_like(q)
    qq = jnp.concatenate([jnp.where(low, q, zero), jnp.where(low, zero, q)], axis=0)
    s = lax.dot_general(qq, kb, (((1,), (1,)), ((), ())), preferred_element_type=F32) + bias
    m = jnp.max(s, axis=1, keepdims=True)
    if sink is not None:
        m = jnp.maximum(m, sink)
    p = jnp.exp(s - m)
    l = jnp.sum(p, axis=1, keepdims=True)
    if sink is not None:
        l = l + jnp.exp(sink - m)
    o = jnp.dot(p.astype(BF16), vb, preferred_element_type=F32) * (1.0 / l)
    lse = m + jnp.log(l)
    o_pair = jnp.where(low, o[:BLK], o[BLK:])
    lse_pair = jnp.where(low, jnp.broadcast_to(lse[:BLK], (BLK, LANES)),
                         jnp.broadcast_to(lse[BLK:], (BLK, LANES)))
    return o_pair, lse_pair


def _merge(o_a, lse_a, o_b, lse_b, want_lse=True):
    mx = jnp.maximum(lse_a, lse_b)
    ea = jnp.exp(lse_a - mx)
    eb = jnp.exp(lse_b - mx)
    den = ea + eb
    o = (ea * o_a + eb * o_b) * (1.0 / den)
    return (o, mx + jnp.log(den)) if want_lse else (o, None)


_CLASSES = 4
_LOOP_UNROLL = 16


def _dilated_kernel(q_ref, k_ref, v_ref, bias_ref, o_ref,
                    q4, k4, v4, tmp, so4, sl4, il_o, il_l, *, seq):
    cls = seq // _CLASSES
    nb4 = cls // BLK
    nb16 = seq // 16 // BLK
    sh4, sh16 = nb4.bit_length() - 1, nb16.bit_length() - 1
    chunk = 512
    shc = (cls // chunk).bit_length() - 1

    for src, dst in ((q_ref, q4), (k_ref, k4), (v_ref, v4)):
        def widen(i, c, src=src):
            r0 = pl.multiple_of(i * chunk, chunk)
            tmp[pl.ds(r0, chunk), :] = src[0, pl.ds(r0, chunk), :].astype(F32)
            return c
        lax.fori_loop(0, seq // chunk, widen, 0)

        def regroup(i, c, dst=dst):
            r = lax.shift_right_logical(i, shc)
            j = i & (cls // chunk - 1)
            d0 = pl.multiple_of(r * cls + j * chunk, chunk)
            dst[pl.ds(d0, chunk), :] = tmp[pl.ds(r + _CLASSES * chunk * j, chunk,
                                                 stride=_CLASSES), :]
            return c
        lax.fori_loop(0, seq // chunk, regroup, 0)

    def dil4(idx, c):
        r = lax.shift_right_logical(idx, sh4)
        n = idx & (nb4 - 1)
        q0 = pl.multiple_of(r * cls + n * BLK, BLK)
        k0 = pl.multiple_of(r * cls + jnp.maximum(n - 1, 0) * BLK, BLK)
        o, lse = _band_block(q4[pl.ds(q0, BLK), :].astype(BF16),
                             k4[pl.ds(k0, 2 * BLK), :].astype(BF16),
                             v4[pl.ds(k0, 2 * BLK), :].astype(BF16),
                             bias_ref[jnp.minimum(n, 1)])
        so4[pl.ds(q0, BLK), :] = o
        sl4[pl.ds(q0, BLK), :] = lse
        return c
    lax.fori_loop(0, _CLASSES * nb4, dil4, 0, unroll=_LOOP_UNROLL)

    def dil16(idx, c):
        n = idx & (nb16 - 1)
        sub = lax.shift_right_logical(idx, sh16) & (_CLASSES - 1)
        r = lax.shift_right_logical(idx, sh16 + 2)
        q0 = r * cls + sub + _CLASSES * BLK * n
        k0 = r * cls + sub + _CLASSES * BLK * jnp.maximum(n - 1, 0)
        qs = pl.ds(q0, BLK, stride=_CLASSES)
        ks = pl.ds(k0, 2 * BLK, stride=_CLASSES)
        o_s, l_s = so4[qs, :], sl4[qs, :]
        o, lse = _band_block(q4[qs, :].astype(BF16), k4[ks, :].astype(BF16),
                             v4[ks, :].astype(BF16), bias_ref[jnp.minimum(n, 1)])
        o, lse = _merge(o_s, l_s, o, lse)
        so4[qs, :] = o
        sl4[qs, :] = lse
        return c
    lax.fori_loop(0, 16 * nb16, dil16, 0, unroll=_LOOP_UNROLL)

    piece = BLK // _CLASSES

    def dil1(n, c):
        q0 = pl.multiple_of(n * BLK, BLK)
        k0 = pl.multiple_of(jnp.maximum(n - 1, 0) * BLK, BLK)
        for r in range(_CLASSES):
            s0 = pl.multiple_of(r * cls + n * piece, piece)
            il_o[pl.ds(r, piece, stride=_CLASSES), :] = so4[pl.ds(s0, piece), :]
            il_l[pl.ds(r, piece, stride=_CLASSES), :] = sl4[pl.ds(s0, piece), :]
        o_s, l_s = il_o[...], il_l[...]
        o, lse = _band_block(q_ref[0, pl.ds(q0, BLK), :], k_ref[0, pl.ds(k0, 2 * BLK), :],
                             v_ref[0, pl.ds(k0, 2 * BLK), :], bias_ref[jnp.minimum(n, 1)])
        o, _ = _merge(o_s, l_s, o, lse, want_lse=False)
        o_ref[0, pl.ds(q0, BLK), :] = o.astype(o_ref.dtype)
        return c
    lax.fori_loop(0, seq // BLK, dil1, 0, unroll=_LOOP_UNROLL)


def _dilated_attention(q, k, v):
    assert DILATED_BRANCHES == ((128, 1), (512, 4), (2048, 16))
    batch, seq, width = q.shape
    assert seq % (16 * BLK) == 0 and (seq & (seq - 1)) == 0
    blk = pl.BlockSpec((1, seq, LANES), lambda b, c: (b, 0, c))
    rows = pltpu.VMEM((seq, LANES), F32)
    small = pltpu.VMEM((BLK, LANES), F32)
    return pl.pallas_call(
        functools.partial(_dilated_kernel, seq=seq),
        grid=(batch, width // LANES),
        in_specs=[blk, blk, blk, pl.BlockSpec((2, 2 * BLK, 2 * BLK), lambda b, c: (0, 0, 0))],
        out_specs=blk,
        out_shape=jax.ShapeDtypeStruct((batch, seq, width), BF16),
        scratch_shapes=[rows] * 6 + [small] * 2,
        compiler_params=_params(("arbitrary", "arbitrary")),
        name="dilated_attn",
    )(q, k, v, _band_bias(BLK))


def _swa_kernel(sink_ref, q_ref, k_ref, v_ref, bias_ref, o_ref, *, n_blocks):
    c = pl.program_id(1)
    row = lax.broadcasted_iota(jnp.int32, (2 * BLK, 1), 0)
    sink = jnp.where(row < BLK, sink_ref[2 * c], sink_ref[2 * c + 1])

    def body(n, carry):
        q0 = pl.multiple_of(n * BLK, BLK)
        k0 = pl.multiple_of(jnp.maximum(n - 1, 0) * BLK, BLK)
        o, _ = _band_block(q_ref[0, pl.ds(q0, BLK), :], k_ref[0, pl.ds(k0, 2 * BLK), :],
                           v_ref[0, pl.ds(k0, 2 * BLK), :], bias_ref[jnp.minimum(n, 1)], sink)
        o_ref[0, pl.ds(q0, BLK), :] = o.astype(o_ref.dtype)
        return carry
    lax.fori_loop(0, n_blocks, body, 0, unroll=_LOOP_UNROLL)


def _swa_attention(q, k, v, sinks):
    batch, seq, width = q.shape
    qblk = pl.BlockSpec((1, seq, LANES), lambda b, c, s: (b, 0, c))
    kblk = pl.BlockSpec((1, seq, LANES), lambda b, c, s: (b, 0, c // 2))
    grid_spec = pltpu.PrefetchScalarGridSpec(
        num_scalar_prefetch=1, grid=(batch, width // LANES),
        in_specs=[qblk, kblk, kblk,
                  pl.BlockSpec((2, 2 * BLK, 2 * BLK), lambda b, c, s: (0, 0, 0))],
        out_specs=qblk)
    return pl.pallas_call(
        functools.partial(_swa_kernel, n_blocks=seq // BLK), grid_spec=grid_spec,
        out_shape=jax.ShapeDtypeStruct((batch, seq, width), BF16),
        compiler_params=_params(("arbitrary", "arbitrary")),
        name="swa_sink_attn",
    )(sinks, q, k, v, _band_bias(SWA_WINDOW - 1))


_R_E1, _R_E2, _R_G1, _R_G2, _R_RANK1, _R_RANK2 = range(6)


def _route(logits, count_ref):
    tm = logits.shape[0]
    lane = lax.broadcasted_iota(jnp.int32, logits.shape, 1)
    v1 = jnp.max(logits, axis=1, keepdims=True)
    i1 = jnp.min(jnp.where(logits == v1, lane, LANES), axis=1, keepdims=True)
    rest = jnp.where(lane == i1, NEG_BIG, logits)
    v2 = jnp.max(rest, axis=1, keepdims=True)
    i2 = jnp.min(jnp.where(rest == v2, lane, LANES), axis=1, keepdims=True)
    e2 = jnp.exp(v2 - v1)
    g1 = 1.0 / (1.0 + e2)
    g2 = e2 / (1.0 + e2)
    sel = (lane == i1) | (lane == i2)
    r = lax.broadcasted_iota(jnp.int32, (tm, tm), 0)
    c = lax.broadcasted_iota(jnp.int32, (tm, tm), 1)
    tri = jnp.where(c < r, 1.0, 0.0).astype(BF16)
    before = jnp.dot(tri, jnp.where(sel, 1.0, 0.0).astype(BF16),
                     preferred_element_type=F32) + count_ref[...]
    rank1 = jnp.sum(jnp.where(lane == i1, before, 0.0), axis=1, keepdims=True)
    rank2 = jnp.sum(jnp.where(lane == i2, before, 0.0), axis=1, keepdims=True)
    count_ref[...] += jnp.sum(jnp.where(sel, 1.0, 0.0), axis=0, keepdims=True)
    rec = jnp.zeros(logits.shape, F32)
    for ln, col in ((_R_E1, i1.astype(F32)), (_R_E2, i2.astype(F32)), (_R_G1, g1), (_R_G2, g2),
                    (_R_RANK1, rank1), (_R_RANK2, rank2)):
        rec = jnp.where(lane == ln, col, rec)
    return rec


def _outproj_kernel(*refs, with_router):
    oa_ref, ob_ref, x_ref, ga_ref, gb_ref, w_ref, gf_ref = refs[:7]
    rest = refs[7:]
    if with_router:
        r_ref, xo_ref, ho_ref, route_ref, count_ref = rest
    else:
        xo_ref, ho_ref = rest
    na = _rms(oa_ref[...].astype(F32), ga_ref[...]).astype(BF16)
    nb = _rms(ob_ref[...].astype(F32), gb_ref[...]).astype(BF16)
    xn = (x_ref[...] + jnp.dot(na, w_ref[:WIDTH_A, :], preferred_element_type=F32)
          + jnp.dot(nb, w_ref[WIDTH_A:, :], preferred_element_type=F32))
    xo_ref[...] = xn
    hf = _rms(xn, gf_ref[...])
    ho_ref[...] = hf.astype(ho_ref.dtype)
    if with_router:
        @pl.when(pl.program_id(0) == 0)
        def _():
            count_ref[...] = jnp.zeros_like(count_ref)
        h_hi = hf.astype(BF16)
        h_lo = (hf - h_hi.astype(F32)).astype(BF16)
        t = jnp.dot(h_hi, r_ref[...], preferred_element_type=F32)
        logits = (t[:, :LANES] + t[:, LANES:]
                  + jnp.dot(h_lo, r_ref[:, :LANES], preferred_element_type=F32))
        lane = lax.broadcasted_iota(jnp.int32, logits.shape, 1)
        logits = jnp.where(lane < N_EXPERTS, logits, NEG_BIG)
        route_ref[...] = _route(logits, count_ref)


def _outproj(oa, ob, x2, ga, gb, w_out, gf, router_pad=None):
    tokens = x2.shape[0]
    tm = 512
    with_router = router_pad is not None
    row = lambda w: pl.BlockSpec((tm, w), lambda i: (i, 0))
    const = lambda a, b: pl.BlockSpec((a, b), lambda i: (0, 0))
    in_specs = [row(WIDTH_A), row(WIDTH_B), row(D_MODEL), const(1, WIDTH_A), const(1, WIDTH_B),
                const(D_MODEL, D_MODEL), const(1, D_MODEL)]
    out_specs = [row(D_MODEL), row(D_MODEL)]
    out_shape = [jax.ShapeDtypeStruct((tokens, D_MODEL), F32),
                 jax.ShapeDtypeStruct((tokens, D_MODEL), F32 if with_router else BF16)]
    args = [oa, ob, x2, ga, gb, w_out, gf]
    if with_router:
        in_specs.append(const(D_MODEL, 2 * LANES))
        out_specs += [row(LANES), const(1, LANES)]
        out_shape += [jax.ShapeDtypeStruct((tokens, LANES), F32),
                      jax.ShapeDtypeStruct((1, LANES), F32)]
        args.append(router_pad)
    return pl.pallas_call(
        functools.partial(_outproj_kernel, with_router=with_router),
        grid=(tokens // tm,), in_specs=in_specs, out_specs=out_specs, out_shape=out_shape,
        compiler_params=_params(("arbitrary",)),
        name="mix_outproj_router" if with_router else "mix_outproj",
    )(*args)


def _silu(x):
    return x * (1.0 / (1.0 + jnp.exp(-x)))


FFN_TM = 1024
FFN_TF = 1792
FFN_TC = 512
ROW_TM = 512


def _swiglu_chunk(h, wg_ref, wu_ref, wd_ref, acc_ref):
    for c0 in range(0, FFN_TF, FFN_TC):
        c1 = min(c0 + FFN_TC, FFN_TF)
        g = jnp.dot(h, wg_ref[:, c0:c1], preferred_element_type=F32)
        u = jnp.dot(h, wu_ref[:, c0:c1], preferred_element_type=F32)
        acc_ref[...] += jnp.dot((_silu(g) * u).astype(BF16), wd_ref[c0:c1, :],
                                preferred_element_type=F32)


def _ffn_kernel(*refs, final):
    h_ref, wg_ref, wu_ref, wd_ref, x_ref = refs[:5]
    if final:
        gfin_ref, o_ref, acc_ref = refs[5:]
    else:
        o_ref, acc_ref = refs[5:]
    f = pl.program_id(1)

    @pl.when(f == 0)
    def _():
        acc_ref[...] = x_ref[...]

    _swiglu_chunk(h_ref[...], wg_ref, wu_ref, wd_ref, acc_ref)

    @pl.when(f == pl.num_programs(1) - 1)
    def _():
        o_ref[...] = _rms(acc_ref[...], gfin_ref[...]) if final else acc_ref[...]


def _ffn(h, x2, wg, wu, wd, final_gain=None):
    tokens = x2.shape[0]
    tm, tf = FFN_TM, FFN_TF
    final = final_gain is not None
    tok = lambda w: pl.BlockSpec((tm, w), lambda i, f: (i, 0))
    in_specs = [tok(D_MODEL),
                pl.BlockSpec((D_MODEL, tf), lambda i, f: (0, f)),
                pl.BlockSpec((D_MODEL, tf), lambda i, f: (0, f)),
                pl.BlockSpec((tf, D_MODEL), lambda i, f: (f, 0)),
                tok(D_MODEL)]
    args = [h, wg, wu, wd, x2]
    if final:
        in_specs.append(pl.BlockSpec((1, D_MODEL), lambda i, f: (0, 0)))
        args.append(final_gain)
    return pl.pallas_call(
        functools.partial(_ffn_kernel, final=final),
        grid=(tokens // tm, D_FF // tf), in_specs=in_specs, out_specs=tok(D_MODEL),
        out_shape=jax.ShapeDtypeStruct((tokens, D_MODEL), F32),
        scratch_shapes=[pltpu.VMEM((tm, D_MODEL), F32)],
        compiler_params=_params(("arbitrary", "arbitrary")),
        name="dense_swiglu",
    )(*args)


def _row_copy(src_ref, src_row, dst_ref, dst_row, sem):
    return pltpu.make_async_copy(src_ref.at[pl.ds(src_row, 1)], dst_ref.at[pl.ds(dst_row, 1)], sem)


def _dispatch_kernel(tend_ref, slot_ref, h_ref, out_ref, zero_ref, sem):
    tm = h_ref.shape[0]
    n_tiles = out_ref.shape[0] // FFN_TM

    @pl.when(pl.program_id(0) == 0)
    def _():
        zero_ref[...] = jnp.zeros_like(zero_ref)

        def zero_tile(t):
            return pltpu.make_async_copy(zero_ref, out_ref.at[pl.ds(t * FFN_TM, FFN_TM)], sem)

        def has_tiles(e):
            return tend_ref[e] > (tend_ref[e - 1] if e else 0)

        for act in ("start", "wait"):
            for e in range(N_EXPERTS):
                @pl.when(has_tiles(e))
                def _(e=e, act=act):
                    getattr(zero_tile(tend_ref[e] - 1), act)()
            for k in range(N_EXPERTS - 1):
                t = tend_ref[N_EXPERTS - 1] + k

                @pl.when(t < n_tiles)
                def _(t=t, act=act):
                    getattr(zero_tile(t), act)()

    def issue(r, c):
        _row_copy(h_ref, r, out_ref, slot_ref[0, 0, 2 * r], sem).start()
        _row_copy(h_ref, r, out_ref, slot_ref[0, 0, 2 * r + 1], sem).start(priority=1)
        return c
    lax.fori_loop(0, tm, issue, 0, unroll=8)
    for _ in range(2):
        pltpu.make_async_copy(h_ref, out_ref.at[pl.ds(0, tm)], sem).wait()


def _dispatch(h, slots, tile_end, n_slots):
    tokens = h.shape[0]
    tm = ROW_TM
    slots3 = slots.reshape(tokens // tm, 1, 2 * tm)
    grid_spec = pltpu.PrefetchScalarGridSpec(
        num_scalar_prefetch=1, grid=(tokens // tm,),
        in_specs=[pl.BlockSpec((1, 1, 2 * tm), lambda i, te: (i, 0, 0), memory_space=pltpu.SMEM),
                  pl.BlockSpec((tm, D_MODEL), lambda i, te: (i, 0))],
        out_specs=pl.BlockSpec(memory_space=pl.ANY),
        scratch_shapes=[pltpu.VMEM((FFN_TM, D_MODEL), F32), pltpu.SemaphoreType.DMA(())])
    return pl.pallas_call(
        _dispatch_kernel, grid_spec=grid_spec,
        out_shape=jax.ShapeDtypeStruct((n_slots, D_MODEL), F32),
        compiler_params=_params(("arbitrary",)),
        name="moe_dispatch",
    )(tile_end, slots3, h)


def _grouped_kernel(texp_ref, nt_ref, h_ref, wg_ref, wu_ref, wd_ref, y_ref, hb_ref, acc_ref):
    del texp_ref
    i = pl.program_id(0)
    f = pl.program_id(1)

    @pl.when(i < nt_ref[0])
    def _():
        @pl.when(f == 0)
        def _():
            hb_ref[...] = h_ref[...].astype(BF16)
            acc_ref[...] = jnp.zeros_like(acc_ref)

        _swiglu_chunk(hb_ref[...], wg_ref.at[0], wu_ref.at[0], wd_ref.at[0], acc_ref)

        @pl.when(f == pl.num_programs(1) - 1)
        def _():
            y_ref[...] = acc_ref[...]

    @pl.when((i >= nt_ref[0]) & (f == 0))
    def _():
        y_ref[...] = jnp.zeros_like(y_ref)


def _grouped_swiglu(hs, tile_expert, n_tiles_used, wg, wu, wd):
    n_slots = hs.shape[0]
    tm, tf = FFN_TM, FFN_TF
    nf = D_FF // tf
    last = lambda i, nt: jnp.minimum(i, nt[0] - 1)
    fcol = lambda i, f, nt: jnp.where(i < nt[0], f, nf - 1)
    grid_spec = pltpu.PrefetchScalarGridSpec(
        num_scalar_prefetch=2, grid=(n_slots // tm, nf),
        in_specs=[pl.BlockSpec((tm, D_MODEL), lambda i, f, te, nt: (last(i, nt), 0)),
                  pl.BlockSpec((1, D_MODEL, tf), lambda i, f, te, nt: (te[i], 0, fcol(i, f, nt))),
                  pl.BlockSpec((1, D_MODEL, tf), lambda i, f, te, nt: (te[i], 0, fcol(i, f, nt))),
                  pl.BlockSpec((1, tf, D_MODEL), lambda i, f, te, nt: (te[i], fcol(i, f, nt), 0))],
        out_specs=pl.BlockSpec((tm, D_MODEL), lambda i, f, te, nt: (i, 0)),
        scratch_shapes=[pltpu.VMEM((tm, D_MODEL), BF16), pltpu.VMEM((tm, D_MODEL), F32)])
    return pl.pallas_call(
        _grouped_kernel, grid_spec=grid_spec,
        out_shape=jax.ShapeDtypeStruct((n_slots, D_MODEL), F32),
        compiler_params=_params(("arbitrary", "arbitrary")),
        name="moe_grouped_swiglu",
    )(tile_expert, n_tiles_used, hs, wg, wu, wd)


def _combine_kernel(*refs, final):
    slot_ref, route_ref, x_ref, y_ref = refs[:4]
    if final:
        gfin_ref, o_ref, buf_ref, sem = refs[4:]
    else:
        o_ref, buf_ref, sem = refs[4:]
    tm = x_ref.shape[0]

    def issue(r, c):
        _row_copy(y_ref, slot_ref[0, 0, 2 * r], buf_ref.at[0], r, sem).start()
        _row_copy(y_ref, slot_ref[0, 0, 2 * r + 1], buf_ref.at[1], r, sem).start(priority=1)
        return c
    lax.fori_loop(0, tm, issue, 0, unroll=8)
    for k in range(2):
        pltpu.make_async_copy(y_ref.at[pl.ds(0, tm)], buf_ref.at[k], sem).wait()
    route = route_ref[...]
    out = (x_ref[...] + route[:, _R_G1:_R_G1 + 1] * buf_ref[0]
           + route[:, _R_G2:_R_G2 + 1] * buf_ref[1])
    o_ref[...] = _rms(out, gfin_ref[...]) if final else out


def _combine(y, slots, route, x2, final_gain=None):
    tokens = x2.shape[0]
    tm = ROW_TM
    final = final_gain is not None
    slots3 = slots.reshape(tokens // tm, 1, 2 * tm)
    in_specs = [pl.BlockSpec((1, 1, 2 * tm), lambda i: (i, 0, 0), memory_space=pltpu.SMEM),
                pl.BlockSpec((tm, LANES), lambda i: (i, 0)),
                pl.BlockSpec((tm, D_MODEL), lambda i: (i, 0)),
                pl.BlockSpec(memory_space=pl.ANY)]
    args = [slots3, route, x2, y]
    if final:
        in_specs.append(pl.BlockSpec((1, D_MODEL), lambda i: (0, 0)))
        args.append(final_gain)
    return pl.pallas_call(
        functools.partial(_combine_kernel, final=final),
        grid=(tokens // tm,), in_specs=in_specs,
        out_specs=pl.BlockSpec((tm, D_MODEL), lambda i: (i, 0)),
        out_shape=jax.ShapeDtypeStruct((tokens, D_MODEL), F32),
        scratch_shapes=[pltpu.VMEM((2, tm, D_MODEL), F32), pltpu.SemaphoreType.DMA(())],
        compiler_params=_params(("arbitrary",)),
        name="moe_combine",
    )(*args)


def _routing_tables(route, counts, tokens):
    tm = FFN_TM
    n_tiles = 2 * tokens // tm + N_EXPERTS - 1
    cnt = counts[0, :N_EXPERTS].astype(jnp.int32)
    tiles = (cnt + tm - 1) // tm
    tile_end = jnp.cumsum(tiles)
    start = (tile_end - tiles) * tm
    e = route[:, _R_E1:_R_E2 + 1].astype(jnp.int32)
    rank = route[:, _R_RANK1:_R_RANK2 + 1].astype(jnp.int32)
    slots = start[e] + rank
    tile_expert = jnp.minimum(
        jnp.searchsorted(tile_end, jnp.arange(n_tiles, dtype=jnp.int32), side="right"),
        N_EXPERTS - 1).astype(jnp.int32)
    return slots, tile_expert, tile_end.astype(jnp.int32), n_tiles * tm


def _moe(h, route, counts, x2, wg, wu, wd, final_gain=None):
    slots, tile_expert, tile_end, n_slots = _routing_tables(route, counts, x2.shape[0])
    hs = _dispatch(h, slots, tile_end, n_slots)
    y = _grouped_swiglu(hs, tile_expert, tile_end[-1:], wg, wu, wd)
    return _combine(y, slots, route, x2, final_gain)


def _widen_w_in(w):
    a = 3 * WIDTH_A + WIDTH_B
    kb = w[:, a:a + KV_WIDTH_B]
    vb = w[:, a + KV_WIDTH_B:]
    dup = lambda t: jnp.concatenate(
        [t[:, h * HEAD_DIM:(h + 1) * HEAD_DIM] for h in range(N_KV_B) for _ in range(2)], axis=1)
    return jnp.concatenate([w[:, :a], dup(kb), dup(vb)], axis=1).astype(BF16)


def kernel(x, positions, attn_norm, w_in, mix_norm_a, mix_norm_b, sinks, w_out, ffn_norm,
           dense_w_gate, dense_w_up, dense_w_down, router, moe_w_gate, moe_w_up, moe_w_down,
           final_norm):
    batch, seq, _ = x.shape
    depth = w_in.shape[0]
    tokens = batch * seq
    cos, sina, sinb = _rope_tables(positions)
    x2 = x.reshape(tokens, D_MODEL)
    seq3 = lambda t: t.reshape(batch, seq, t.shape[-1])
    for i in range(depth):
        qa, ka, va, qb, kb, vb = _inproj(x2, attn_norm[i][None, :], _widen_w_in(w_in[i]),
                                         cos, sina, sinb, seq)
        oa = _dilated_attention(seq3(qa), seq3(ka), seq3(va)).reshape(tokens, WIDTH_A)
        ob = _swa_attention(seq3(qb), seq3(kb), seq3(vb), sinks[i]).reshape(tokens, WIDTH_B)
        j = i // 2
        last = i == depth - 1
        norms = (mix_norm_a[i][None, :], mix_norm_b[i][None, :], w_out[i].astype(BF16),
                 ffn_norm[i][None, :])
        if i % 2 == 0:
            x2, h = _outproj(oa, ob, x2, *norms)
            x2 = _ffn(h, x2, dense_w_gate[j].astype(BF16), dense_w_up[j].astype(BF16),
                      dense_w_down[j].astype(BF16),
                      final_gain=final_norm[None, :] if last else None)
        else:
            r_f32 = jnp.pad(router[j], ((0, 0), (0, LANES - N_EXPERTS)))
            r_hi = r_f32.astype(BF16)
            r_lo = (r_f32 - r_hi.astype(F32)).astype(BF16)
            router_pad = jnp.concatenate([r_hi, r_lo], axis=1)
            x2, h, route, counts = _outproj(oa, ob, x2, *norms, router_pad)
            x2 = _moe(h, route, counts, x2, moe_w_gate[j].astype(BF16),
                      moe_w_up[j].astype(BF16), moe_w_down[j].astype(BF16),
                      final_gain=final_norm[None, :] if last else None)
    return x2.reshape(batch, seq, D_MODEL)
```

```python
import functools

import numpy as np
import jax
import jax.numpy as jnp
from jax import lax
from jax.experimental import pallas as pl
from jax.experimental.pallas import tpu as pltpu

D_MODEL = 1024
HEAD_DIM = 64
N_HEADS_A = 8
N_HEADS_B = 8
N_KV_B = 2
DILATED_BRANCHES = ((128, 1), (512, 4), (2048, 16))
SWA_WINDOW = 128
BLK = 128
ROPE_THETA = 10000.0
D_FF = 3584
N_EXPERTS = 8
RMS_EPS = 1e-5
WIDTH_A = N_HEADS_A * HEAD_DIM
WIDTH_B = N_HEADS_B * HEAD_DIM
KV_WIDTH_B = N_KV_B * HEAD_DIM

LANES = 128
NEG_BIG = -1e30
VMEM_LIMIT = 56 * 1024 * 1024

F32 = jnp.float32
BF16 = jnp.bfloat16


def _params(sem):
    return pltpu.CompilerParams(dimension_semantics=sem, vmem_limit_bytes=VMEM_LIMIT)


def _rms(x, g):
    return x * lax.rsqrt(jnp.mean(x * x, axis=-1, keepdims=True) + RMS_EPS) * g


def _rope_table_kernel(pos_ref, inv_ref, cos_ref, sina_ref, sinb_ref):
    ang = pos_ref[...].astype(F32) * inv_ref[...]
    c = jnp.cos(ang)
    s = jnp.sin(ang)
    lane = lax.broadcasted_iota(jnp.int32, ang.shape, 1)
    upper = (lane & (HEAD_DIM - 1)) >= HEAD_DIM // 2
    cos_ref[...] = c
    sina_ref[...] = jnp.where(upper, s, 0.0)
    sinb_ref[...] = jnp.where(upper, 0.0, -s)


def _rope_tables(positions):
    seq = positions.shape[0]
    inv = ROPE_THETA ** (-jnp.arange(0, HEAD_DIM, 2, dtype=F32) / HEAD_DIM)
    inv = jnp.tile(inv, LANES // (HEAD_DIM // 2))[None, :]
    rows = 1024
    tab = jax.ShapeDtypeStruct((seq, LANES), F32)
    return pl.pallas_call(
        _rope_table_kernel,
        grid=(seq // rows,),
        in_specs=[pl.BlockSpec((rows, 1), lambda i: (i, 0)),
                  pl.BlockSpec((1, LANES), lambda i: (0, 0))],
        out_specs=[pl.BlockSpec((rows, LANES), lambda i: (i, 0))] * 3,
        out_shape=[tab, tab, tab],
        compiler_params=_params(("arbitrary",)),
        name="rope_tables",
    )(positions.reshape(seq, 1), inv)


LOG2E = 1.4426950408889634
_QSCALE = HEAD_DIM ** -0.5 * LOG2E
_SEGS = (("qa", WIDTH_A, True, _QSCALE), ("ka", WIDTH_A, True, 1.0),
         ("va", WIDTH_A, False, 1.0), ("qb", WIDTH_B, True, _QSCALE),
         ("kb", 2 * KV_WIDTH_B, True, 1.0), ("vb", 2 * KV_WIDTH_B, False, 1.0))


def _inproj_kernel(x_ref, g_ref, w_ref, cos_ref, sina_ref, sinb_ref, *out_refs):
    h = _rms(x_ref[...], g_ref[...]).astype(BF16)
    cos = cos_ref[...]
    sina = sina_ref[...]
    sinb = sinb_ref[...]
    col = 0
    for (_, width, roped, scale), o_ref in zip(_SEGS, out_refs):
        t = jnp.dot(h, w_ref[:, col:col + width], preferred_element_type=F32)
        col += width
        for j in range(width // LANES):
            tj = t[:, j * LANES:(j + 1) * LANES]
            if roped:
                tj = (tj * cos + pltpu.roll(tj, HEAD_DIM // 2, 1) * sina
                      + pltpu.roll(tj, LANES - HEAD_DIM // 2, 1) * sinb)
                if scale != 1.0:
                    tj = tj * scale
            o_ref[:, j * LANES:(j + 1) * LANES] = tj.astype(o_ref.dtype)


def _inproj(x2, g, w_ext, cos, sina, sinb, seq):
    tokens = x2.shape[0]
    tm = 512
    spt = seq // tm
    widths = [s[1] for s in _SEGS]
    total = sum(widths)
    return pl.pallas_call(
        _inproj_kernel,
        grid=(tokens // tm,),
        in_specs=[pl.BlockSpec((tm, D_MODEL), lambda i: (i, 0)),
                  pl.BlockSpec((1, D_MODEL), lambda i: (0, 0)),
                  pl.BlockSpec((D_MODEL, total), lambda i: (0, 0)),
                  pl.BlockSpec((tm, LANES), lambda i: (i % spt, 0)),
                  pl.BlockSpec((tm, LANES), lambda i: (i % spt, 0)),
                  pl.BlockSpec((tm, LANES), lambda i: (i % spt, 0))],
        out_specs=[pl.BlockSpec((tm, w), lambda i: (i, 0)) for w in widths],
        out_shape=[jax.ShapeDtypeStruct((tokens, w), BF16) for w in widths],
        compiler_params=_params(("arbitrary",)),
        name="inproj_rope",
    )(x2, g, w_ext, cos, sina, sinb)


def _band_bias(max_dist):
    p = np.arange(BLK)[:, None]
    c = np.arange(2 * BLK)[None, :]
    first = c <= p
    dist = BLK + p - c
    main = (dist >= 0) & (dist <= max_dist)
    bias = np.where(np.stack([first, main]), 0.0, NEG_BIG).astype(np.float32)
    return jnp.asarray(np.concatenate([bias, bias], axis=1))


def _low_lanes():
    return lax.broadcasted_iota(jnp.int32, (BLK, LANES), 1) < HEAD_DIM


def _band_block(q, kb, vb, bias, sink=None):
    low = _low_lanes()
    zero = jnp.zeros_like(q)
    qq = jnp.concatenate([jnp.where(low, q, zero), jnp.where(low, zero, q)], axis=0)
    s = lax.dot_general(qq, kb, (((1,), (1,)), ((), ())), preferred_element_type=F32) + bias
    m = jnp.max(s, axis=1, keepdims=True)
    if sink is not None:
        m = jnp.maximum(m, sink)
    p = jnp.exp2(s - m)
    l = jnp.sum(p, axis=1, keepdims=True)
    if sink is not None:
        l = l + jnp.exp2(sink - m)
    o = jnp.dot(p.astype(BF16), vb, preferred_element_type=F32) * (1.0 / l)
    lse = m + jnp.log2(l)
    o_pair = jnp.where(low, o[:BLK], o[BLK:])
    lse_pair = jnp.where(low, jnp.broadcast_to(lse[:BLK], (BLK, LANES)),
                         jnp.broadcast_to(lse[BLK:], (BLK, LANES)))
    return o_pair, lse_pair


def _merge(o_a, lse_a, o_b, lse_b, want_lse=True):
    mx = jnp.maximum(lse_a, lse_b)
    ea = jnp.exp2(lse_a - mx)
    eb = jnp.exp2(lse_b - mx)
    den = ea + eb
    o = (ea * o_a + eb * o_b) * (1.0 / den)
    return (o, mx + jnp.log2(den)) if want_lse else (o, None)


_CLASSES = 4
_LOOP_UNROLL = 16


def _dilated_kernel(q_ref, k_ref, v_ref, bias_ref, o_ref,
                    q4, k4, v4, tmp, so4, sl4, il_o, il_l, *, seq):
    cls = seq // _CLASSES
    nb4 = cls // BLK
    nb16 = seq // 16 // BLK
    sh4, sh16 = nb4.bit_length() - 1, nb16.bit_length() - 1
    chunk = 512
    shc = (cls // chunk).bit_length() - 1

    for src, dst in ((q_ref, q4), (k_ref, k4), (v_ref, v4)):
        def widen(i, c, src=src):
            r0 = pl.multiple_of(i * chunk, chunk)
            tmp[pl.ds(r0, chunk), :] = src[0, pl.ds(r0, chunk), :].astype(F32)
            return c
        lax.fori_loop(0, seq // chunk, widen, 0)

        def regroup(i, c, dst=dst):
            r = lax.shift_right_logical(i, shc)
            j = i & (cls // chunk - 1)
            d0 = pl.multiple_of(r * cls + j * chunk, chunk)
            dst[pl.ds(d0, chunk), :] = tmp[pl.ds(r + _CLASSES * chunk * j, chunk,
                                                 stride=_CLASSES), :]
            return c
        lax.fori_loop(0, seq // chunk, regroup, 0)

    def dil4(idx, c):
        r = lax.shift_right_logical(idx, sh4)
        n = idx & (nb4 - 1)
        q0 = pl.multiple_of(r * cls + n * BLK, BLK)
        k0 = pl.multiple_of(r * cls + jnp.maximum(n - 1, 0) * BLK, BLK)
        o, lse = _band_block(q4[pl.ds(q0, BLK), :].astype(BF16),
                             k4[pl.ds(k0, 2 * BLK), :].astype(BF16),
                             v4[pl.ds(k0, 2 * BLK), :].astype(BF16),
                             bias_ref[jnp.minimum(n, 1)])
        so4[pl.ds(q0, BLK), :] = o
        sl4[pl.ds(q0, BLK), :] = lse
        return c
    lax.fori_loop(0, _CLASSES * nb4, dil4, 0, unroll=_LOOP_UNROLL)

    def dil16(idx, c):
        n = idx & (nb16 - 1)
        sub = lax.shift_right_logical(idx, sh16) & (_CLASSES - 1)
        r = lax.shift_right_logical(idx, sh16 + 2)
        q0 = r * cls + sub + _CLASSES * BLK * n
        k0 = r * cls + sub + _CLASSES * BLK * jnp.maximum(n - 1, 0)
        qs = pl.ds(q0, BLK, stride=_CLASSES)
        ks = pl.ds(k0, 2 * BLK, stride=_CLASSES)
        o_s, l_s = so4[qs, :], sl4[qs, :]
        o, lse = _band_block(q4[qs, :].astype(BF16), k4[ks, :].astype(BF16),
                             v4[ks, :].astype(BF16), bias_ref[jnp.minimum(n, 1)])
        o, lse = _merge(o_s, l_s, o, lse)
        so4[qs, :] = o
        sl4[qs, :] = lse
        return c
    lax.fori_loop(0, 16 * nb16, dil16, 0, unroll=_LOOP_UNROLL)

    piece = BLK // _CLASSES

    def dil1(n, c):
        q0 = pl.multiple_of(n * BLK, BLK)
        k0 = pl.multiple_of(jnp.maximum(n - 1, 0) * BLK, BLK)
        for r in range(_CLASSES):
            s0 = pl.multiple_of(r * cls + n * piece, piece)
            il_o[pl.ds(r, piece, stride=_CLASSES), :] = so4[pl.ds(s0, piece), :]
            il_l[pl.ds(r, piece, stride=_CLASSES), :] = sl4[pl.ds(s0, piece), :]
        o_s, l_s = il_o[...], il_l[...]
        o, lse = _band_block(q_ref[0, pl.ds(q0, BLK), :], k_ref[0, pl.ds(k0, 2 * BLK), :],
                             v_ref[0, pl.ds(k0, 2 * BLK), :], bias_ref[jnp.minimum(n, 1)])
        o, _ = _merge(o_s, l_s, o, lse, want_lse=False)
        o_ref[0, pl.ds(q0, BLK), :] = o.astype(o_ref.dtype)
        return c
    lax.fori_loop(0, seq // BLK, dil1, 0, unroll=_LOOP_UNROLL)


def _dilated_attention(q, k, v):
    assert DILATED_BRANCHES == ((128, 1), (512, 4), (2048, 16))
    batch, seq, width = q.shape
    assert seq % (16 * BLK) == 0 and (seq & (seq - 1)) == 0
    blk = pl.BlockSpec((1, seq, LANES), lambda b, c: (b, 0, c))
    rows = pltpu.VMEM((seq, LANES), F32)
    small = pltpu.VMEM((BLK, LANES), F32)
    return pl.pallas_call(
        functools.partial(_dilated_kernel, seq=seq),
        grid=(batch, width // LANES),
        in_specs=[blk, blk, blk, pl.BlockSpec((2, 2 * BLK, 2 * BLK), lambda b, c: (0, 0, 0))],
        out_specs=blk,
        out_shape=jax.ShapeDtypeStruct((batch, seq, width), BF16),
        scratch_shapes=[rows] * 6 + [small] * 2,
        compiler_params=_params(("arbitrary", "arbitrary")),
        name="dilated_attn",
    )(q, k, v, _band_bias(BLK))


def _swa_kernel(sink_ref, q_ref, k_ref, v_ref, bias_ref, o_ref, *, n_blocks):
    c = pl.program_id(1)
    row = lax.broadcasted_iota(jnp.int32, (2 * BLK, 1), 0)
    sink = jnp.where(row < BLK, sink_ref[2 * c], sink_ref[2 * c + 1]) * LOG2E

    def body(n, carry):
        q0 = pl.multiple_of(n * BLK, BLK)
        k0 = pl.multiple_of(jnp.maximum(n - 1, 0) * BLK, BLK)
        o, _ = _band_block(q_ref[0, pl.ds(q0, BLK), :], k_ref[0, pl.ds(k0, 2 * BLK), :],
                           v_ref[0, pl.ds(k0, 2 * BLK), :], bias_ref[jnp.minimum(n, 1)], sink)
        o_ref[0, pl.ds(q0, BLK), :] = o.astype(o_ref.dtype)
        return carry
    lax.fori_loop(0, n_blocks, body, 0, unroll=_LOOP_UNROLL)


def _swa_attention(q, k, v, sinks):
    batch, seq, width = q.shape
    qblk = pl.BlockSpec((1, seq, LANES), lambda b, c, s: (b, 0, c))
    kblk = pl.BlockSpec((1, seq, LANES), lambda b, c, s: (b, 0, c // 2))
    grid_spec = pltpu.PrefetchScalarGridSpec(
        num_scalar_prefetch=1, grid=(batch, width // LANES),
        in_specs=[qblk, kblk, kblk,
                  pl.BlockSpec((2, 2 * BLK, 2 * BLK), lambda b, c, s: (0, 0, 0))],
        out_specs=qblk)
    return pl.pallas_call(
        functools.partial(_swa_kernel, n_blocks=seq // BLK), grid_spec=grid_spec,
        out_shape=jax.ShapeDtypeStruct((batch, seq, width), BF16),
        compiler_params=_params(("arbitrary", "arbitrary")),
        name="swa_sink_attn",
    )(sinks, q, k, v, _band_bias(SWA_WINDOW - 1))


_R_E1, _R_E2, _R_G1, _R_G2, _R_RANK1, _R_RANK2 = range(6)


def _route(logits, count_ref):
    tm = logits.shape[0]
    lane = lax.broadcasted_iota(jnp.int32, logits.shape, 1)
    v1 = jnp.max(logits, axis=1, keepdims=True)
    i1 = jnp.min(jnp.where(logits == v1, lane, LANES), axis=1, keepdims=True)
    rest = jnp.where(lane == i1, NEG_BIG, logits)
    v2 = jnp.max(rest, axis=1, keepdims=True)
    i2 = jnp.min(jnp.where(rest == v2, lane, LANES), axis=1, keepdims=True)
    e2 = jnp.exp(v2 - v1)
    g1 = 1.0 / (1.0 + e2)
    g2 = e2 / (1.0 + e2)
    sel = (lane == i1) | (lane == i2)
    r = lax.broadcasted_iota(jnp.int32, (tm, tm), 0)
    c = lax.broadcasted_iota(jnp.int32, (tm, tm), 1)
    tri = jnp.where(c < r, 1.0, 0.0).astype(BF16)
    before = jnp.dot(tri, jnp.where(sel, 1.0, 0.0).astype(BF16),
                     preferred_element_type=F32) + count_ref[...]
    rank1 = jnp.sum(jnp.where(lane == i1, before, 0.0), axis=1, keepdims=True)
    rank2 = jnp.sum(jnp.where(lane == i2, before, 0.0), axis=1, keepdims=True)
    count_ref[...] += jnp.sum(jnp.where(sel, 1.0, 0.0), axis=0, keepdims=True)
    rec = jnp.zeros(logits.shape, F32)
    for ln, col in ((_R_E1, i1.astype(F32)), (_R_E2, i2.astype(F32)), (_R_G1, g1), (_R_G2, g2),
                    (_R_RANK1, rank1), (_R_RANK2, rank2)):
        rec = jnp.where(lane == ln, col, rec)
    return rec


def _outproj_kernel(*refs, with_router):
    oa_ref, ob_ref, x_ref, ga_ref, gb_ref, w_ref, gf_ref = refs[:7]
    rest = refs[7:]
    if with_router:
        r_ref, xo_ref, ho_ref, route_ref, count_ref = rest
    else:
        xo_ref, ho_ref = rest
    na = _rms(oa_ref[...].astype(F32), ga_ref[...]).astype(BF16)
    nb = _rms(ob_ref[...].astype(F32), gb_ref[...]).astype(BF16)
    xn = (x_ref[...] + jnp.dot(na, w_ref[:WIDTH_A, :], preferred_element_type=F32)
          + jnp.dot(nb, w_ref[WIDTH_A:, :], preferred_element_type=F32))
    xo_ref[...] = xn
    hf = _rms(xn, gf_ref[...])
    ho_ref[...] = hf.astype(ho_ref.dtype)
    if with_router:
        @pl.when(pl.program_id(0) == 0)
        def _():
            count_ref[...] = jnp.zeros_like(count_ref)
        h_hi = hf.astype(BF16)
        h_lo = (hf - h_hi.astype(F32)).astype(BF16)
        t = jnp.dot(h_hi, r_ref[...], preferred_element_type=F32)
        logits = (t[:, :LANES] + t[:, LANES:]
                  + jnp.dot(h_lo, r_ref[:, :LANES], preferred_element_type=F32))
        lane = lax.broadcasted_iota(jnp.int32, logits.shape, 1)
        logits = jnp.where(lane < N_EXPERTS, logits, NEG_BIG)
        route_ref[...] = _route(logits, count_ref)


def _outproj(oa, ob, x2, ga, gb, w_out, gf, router_pad=None):
    tokens = x2.shape[0]
    tm = 512
    with_router = router_pad is not None
    row = lambda w: pl.BlockSpec((tm, w), lambda i: (i, 0))
    const = lambda a, b: pl.BlockSpec((a, b), lambda i: (0, 0))
    in_specs = [row(WIDTH_A), row(WIDTH_B), row(D_MODEL), const(1, WIDTH_A), const(1, WIDTH_B),
                const(D_MODEL, D_MODEL), const(1, D_MODEL)]
    out_specs = [row(D_MODEL), row(D_MODEL)]
    out_shape = [jax.ShapeDtypeStruct((tokens, D_MODEL), F32),
                 jax.ShapeDtypeStruct((tokens, D_MODEL), F32 if with_router else BF16)]
    args = [oa, ob, x2, ga, gb, w_out, gf]
    if with_router:
        in_specs.append(const(D_MODEL, 2 * LANES))
        out_specs += [row(LANES), const(1, LANES)]
        out_shape += [jax.ShapeDtypeStruct((tokens, LANES), F32),
                      jax.ShapeDtypeStruct((1, LANES), F32)]
        args.append(router_pad)
    return pl.pallas_call(
        functools.partial(_outproj_kernel, with_router=with_router),
        grid=(tokens // tm,), in_specs=in_specs, out_specs=out_specs, out_shape=out_shape,
        compiler_params=_params(("arbitrary",)),
        name="mix_outproj_router" if with_router else "mix_outproj",
    )(*args)


def _silu(x):
    return x * (1.0 / (1.0 + jnp.exp(-x)))


FFN_TM = 1024
FFN_TF = 1792
FFN_TC = 512
ROW_TM = 512


def _swiglu_chunk(h, wg_ref, wu_ref, wd_ref, acc_ref):
    for c0 in range(0, FFN_TF, FFN_TC):
        c1 = min(c0 + FFN_TC, FFN_TF)
        g = jnp.dot(h, wg_ref[:, c0:c1], preferred_element_type=F32)
        u = jnp.dot(h, wu_ref[:, c0:c1], preferred_element_type=F32)
        acc_ref[...] += jnp.dot((_silu(g) * u).astype(BF16), wd_ref[c0:c1, :],
                                preferred_element_type=F32)


def _ffn_kernel(*refs, final):
    h_ref, wg_ref, wu_ref, wd_ref, x_ref = refs[:5]
    if final:
        gfin_ref, o_ref, acc_ref = refs[5:]
    else:
        o_ref, acc_ref = refs[5:]
    f = pl.program_id(1)

    @pl.when(f == 0)
    def _():
        acc_ref[...] = x_ref[...]

    _swiglu_chunk(h_ref[...], wg_ref.at[0], wu_ref.at[0], wd_ref.at[0], acc_ref)

    @pl.when(f == pl.num_programs(1) - 1)
    def _():
        o_ref[...] = _rms(acc_ref[...], gfin_ref[...]) if final else acc_ref[...]


def _ffn(h, x2, wg, wu, wd, layer, final_gain=None):
    tokens = x2.shape[0]
    tm, tf = FFN_TM, FFN_TF
    final = final_gain is not None
    tok = lambda w: pl.BlockSpec((tm, w), lambda i, f: (i, 0))
    in_specs = [tok(D_MODEL),
                pl.BlockSpec((1, D_MODEL, tf), lambda i, f: (layer, 0, f)),
                pl.BlockSpec((1, D_MODEL, tf), lambda i, f: (layer, 0, f)),
                pl.BlockSpec((1, tf, D_MODEL), lambda i, f: (layer, f, 0)),
                tok(D_MODEL)]
    args = [h, wg, wu, wd, x2]
    if final:
        in_specs.append(pl.BlockSpec((1, D_MODEL), lambda i, f: (0, 0)))
        args.append(final_gain)
    return pl.pallas_call(
        functools.partial(_ffn_kernel, final=final),
        grid=(tokens // tm, D_FF // tf), in_specs=in_specs, out_specs=tok(D_MODEL),
        out_shape=jax.ShapeDtypeStruct((tokens, D_MODEL), F32),
        scratch_shapes=[pltpu.VMEM((tm, D_MODEL), F32)],
        compiler_params=_params(("arbitrary", "arbitrary")),
        name="dense_swiglu",
    )(*args)


def _row_copy(src_ref, src_row, dst_ref, dst_row, sem):
    return pltpu.make_async_copy(src_ref.at[pl.ds(src_row, 1)], dst_ref.at[pl.ds(dst_row, 1)], sem)


def _dispatch_kernel(tend_ref, slot_ref, h_ref, out_ref, zero_ref, sem):
    tm = h_ref.shape[0]
    n_tiles = out_ref.shape[0] // FFN_TM

    @pl.when(pl.program_id(0) == 0)
    def _():
        zero_ref[...] = jnp.zeros_like(zero_ref)

        def zero_tile(t):
            return pltpu.make_async_copy(zero_ref, out_ref.at[pl.ds(t * FFN_TM, FFN_TM)], sem)

        def has_tiles(e):
            return tend_ref[e] > (tend_ref[e - 1] if e else 0)

        for act in ("start", "wait"):
            for e in range(N_EXPERTS):
                @pl.when(has_tiles(e))
                def _(e=e, act=act):
                    getattr(zero_tile(tend_ref[e] - 1), act)()
            for k in range(N_EXPERTS - 1):
                t = tend_ref[N_EXPERTS - 1] + k

                @pl.when(t < n_tiles)
                def _(t=t, act=act):
                    getattr(zero_tile(t), act)()

    def issue(r, c):
        _row_copy(h_ref, r, out_ref, slot_ref[0, 0, 2 * r], sem).start()
        _row_copy(h_ref, r, out_ref, slot_ref[0, 0, 2 * r + 1], sem).start(priority=1)
        return c
    lax.fori_loop(0, tm, issue, 0, unroll=8)
    for _ in range(2):
        pltpu.make_async_copy(h_ref, out_ref.at[pl.ds(0, tm)], sem).wait()


def _dispatch(h, slots, tile_end, n_slots):
    tokens = h.shape[0]
    tm = ROW_TM
    slots3 = slots.reshape(tokens // tm, 1, 2 * tm)
    grid_spec = pltpu.PrefetchScalarGridSpec(
        num_scalar_prefetch=1, grid=(tokens // tm,),
        in_specs=[pl.BlockSpec((1, 1, 2 * tm), lambda i, te: (i, 0, 0), memory_space=pltpu.SMEM),
                  pl.BlockSpec((tm, D_MODEL), lambda i, te: (i, 0))],
        out_specs=pl.BlockSpec(memory_space=pl.ANY),
        scratch_shapes=[pltpu.VMEM((FFN_TM, D_MODEL), F32), pltpu.SemaphoreType.DMA(())])
    return pl.pallas_call(
        _dispatch_kernel, grid_spec=grid_spec,
        out_shape=jax.ShapeDtypeStruct((n_slots, D_MODEL), F32),
        compiler_params=_params(("arbitrary",)),
        name="moe_dispatch",
    )(tile_end, slots3, h)


def _grouped_kernel(texp_ref, nt_ref, h_ref, wg_ref, wu_ref, wd_ref, y_ref, hb_ref, acc_ref):
    del texp_ref
    i = pl.program_id(0)
    f = pl.program_id(1)

    @pl.when(i < nt_ref[0])
    def _():
        @pl.when(f == 0)
        def _():
            hb_ref[...] = h_ref[...].astype(BF16)
            acc_ref[...] = jnp.zeros_like(acc_ref)

        _swiglu_chunk(hb_ref[...], wg_ref.at[0, 0], wu_ref.at[0, 0], wd_ref.at[0, 0], acc_ref)

        @pl.when(f == pl.num_programs(1) - 1)
        def _():
            y_ref[...] = acc_ref[...]

    @pl.when((i >= nt_ref[0]) & (f == 0))
    def _():
        y_ref[...] = jnp.zeros_like(y_ref)


def _grouped_swiglu(hs, tile_expert, n_tiles_used, wg, wu, wd, layer):
    n_slots = hs.shape[0]
    tm, tf = FFN_TM, FFN_TF
    nf = D_FF // tf
    last = lambda i, nt: jnp.minimum(i, nt[0] - 1)
    fcol = lambda i, f, nt: jnp.where(i < nt[0], f, nf - 1)
    grid_spec = pltpu.PrefetchScalarGridSpec(
        num_scalar_prefetch=2, grid=(n_slots // tm, nf),
        in_specs=[pl.BlockSpec((tm, D_MODEL), lambda i, f, te, nt: (last(i, nt), 0)),
                  pl.BlockSpec((1, 1, D_MODEL, tf),
                               lambda i, f, te, nt: (layer, te[i], 0, fcol(i, f, nt))),
                  pl.BlockSpec((1, 1, D_MODEL, tf),
                               lambda i, f, te, nt: (layer, te[i], 0, fcol(i, f, nt))),
                  pl.BlockSpec((1, 1, tf, D_MODEL),
                               lambda i, f, te, nt: (layer, te[i], fcol(i, f, nt), 0))],
        out_specs=pl.BlockSpec((tm, D_MODEL), lambda i, f, te, nt: (i, 0)),
        scratch_shapes=[pltpu.VMEM((tm, D_MODEL), BF16), pltpu.VMEM((tm, D_MODEL), F32)])
    return pl.pallas_call(
        _grouped_kernel, grid_spec=grid_spec,
        out_shape=jax.ShapeDtypeStruct((n_slots, D_MODEL), F32),
        compiler_params=_params(("arbitrary", "arbitrary")),
        name="moe_grouped_swiglu",
    )(tile_expert, n_tiles_used, hs, wg, wu, wd)


def _combine_kernel(*refs, final):
    slot_ref, route_ref, x_ref, y_ref = refs[:4]
    if final:
        gfin_ref, o_ref, buf_ref, sem = refs[4:]
    else:
        o_ref, buf_ref, sem = refs[4:]
    tm = x_ref.shape[0]

    def issue(r, c):
        _row_copy(y_ref, slot_ref[0, 0, 2 * r], buf_ref.at[0], r, sem).start()
        _row_copy(y_ref, slot_ref[0, 0, 2 * r + 1], buf_ref.at[1], r, sem).start(priority=1)
        return c
    lax.fori_loop(0, tm, issue, 0, unroll=8)
    for k in range(2):
        pltpu.make_async_copy(y_ref.at[pl.ds(0, tm)], buf_ref.at[k], sem).wait()
    route = route_ref[...]
    out = (x_ref[...] + route[:, _R_G1:_R_G1 + 1] * buf_ref[0]
           + route[:, _R_G2:_R_G2 + 1] * buf_ref[1])
    o_ref[...] = _rms(out, gfin_ref[...]) if final else out


def _combine(y, slots, route, x2, final_gain=None):
    tokens = x2.shape[0]
    tm = ROW_TM
    final = final_gain is not None
    slots3 = slots.reshape(tokens // tm, 1, 2 * tm)
    in_specs = [pl.BlockSpec((1, 1, 2 * tm), lambda i: (i, 0, 0), memory_space=pltpu.SMEM),
                pl.BlockSpec((tm, LANES), lambda i: (i, 0)),
                pl.BlockSpec((tm, D_MODEL), lambda i: (i, 0)),
                pl.BlockSpec(memory_space=pl.ANY)]
    args = [slots3, route, x2, y]
    if final:
        in_specs.append(pl.BlockSpec((1, D_MODEL), lambda i: (0, 0)))
        args.append(final_gain)
    return pl.pallas_call(
        functools.partial(_combine_kernel, final=final),
        grid=(tokens // tm,), in_specs=in_specs,
        out_specs=pl.BlockSpec((tm, D_MODEL), lambda i: (i, 0)),
        out_shape=jax.ShapeDtypeStruct((tokens, D_MODEL), F32),
        scratch_shapes=[pltpu.VMEM((2, tm, D_MODEL), F32), pltpu.SemaphoreType.DMA(())],
        compiler_params=_params(("arbitrary",)),
        name="moe_combine",
    )(*args)


def _routing_tables(route, counts, tokens):
    tm = FFN_TM
    n_tiles = 2 * tokens // tm + N_EXPERTS - 1
    cnt = counts[0, :N_EXPERTS].astype(jnp.int32)
    tiles = (cnt + tm - 1) // tm
    tile_end = jnp.cumsum(tiles)
    start = (tile_end - tiles) * tm
    e = route[:, _R_E1:_R_E2 + 1].astype(jnp.int32)
    rank = route[:, _R_RANK1:_R_RANK2 + 1].astype(jnp.int32)
    slots = start[e] + rank
    tile_ids = jnp.arange(n_tiles, dtype=jnp.int32)[:, None]
    tile_expert = jnp.minimum(jnp.sum(tile_ids >= tile_end[None, :], axis=1),
                              N_EXPERTS - 1).astype(jnp.int32)
    return slots, tile_expert, tile_end.astype(jnp.int32), n_tiles * tm


def _moe(h, route, counts, x2, wg, wu, wd, layer, final_gain=None):
    slots, tile_expert, tile_end, n_slots = _routing_tables(route, counts, x2.shape[0])
    hs = _dispatch(h, slots, tile_end, n_slots)
    y = _grouped_swiglu(hs, tile_expert, tile_end[-1:], wg, wu, wd, layer)
    return _combine(y, slots, route, x2, final_gain)


def _widen_w_in(w):
    a = 3 * WIDTH_A + WIDTH_B
    kb = w[:, a:a + KV_WIDTH_B]
    vb = w[:, a + KV_WIDTH_B:]
    dup = lambda t: jnp.concatenate(
        [t[:, h * HEAD_DIM:(h + 1) * HEAD_DIM] for h in range(N_KV_B) for _ in range(2)], axis=1)
    return jnp.concatenate([w[:, :a], dup(kb), dup(vb)], axis=1).astype(BF16)


def kernel(x, positions, attn_norm, w_in, mix_norm_a, mix_norm_b, sinks, w_out, ffn_norm,
           dense_w_gate, dense_w_up, dense_w_down, router, moe_w_gate, moe_w_up, moe_w_down,
           final_norm):
    batch, seq, _ = x.shape
    depth = w_in.shape[0]
    tokens = batch * seq
    cos, sina, sinb = _rope_tables(positions)
    x2 = x.reshape(tokens, D_MODEL)
    seq3 = lambda t: t.reshape(batch, seq, t.shape[-1])
    dense_w = [w.astype(BF16) for w in (dense_w_gate, dense_w_up, dense_w_down)]
    moe_w = [w.astype(BF16) for w in (moe_w_gate, moe_w_up, moe_w_down)]
    for i in range(depth):
        qa, ka, va, qb, kb, vb = _inproj(x2, attn_norm[i][None, :], _widen_w_in(w_in[i]),
                                         cos, sina, sinb, seq)
        oa = _dilated_attention(seq3(qa), seq3(ka), seq3(va)).reshape(tokens, WIDTH_A)
        ob = _swa_attention(seq3(qb), seq3(kb), seq3(vb), sinks[i]).reshape(tokens, WIDTH_B)
        j = i // 2
        last = i == depth - 1
        norms = (mix_norm_a[i][None, :], mix_norm_b[i][None, :], w_out[i].astype(BF16),
                 ffn_norm[i][None, :])
        if i % 2 == 0:
            x2, h = _outproj(oa, ob, x2, *norms)
            x2 = _ffn(h, x2, *dense_w, j, final_gain=final_norm[None, :] if last else None)
        else:
            r_f32 = jnp.pad(router[j], ((0, 0), (0, LANES - N_EXPERTS)))
            r_hi = r_f32.astype(BF16)
            r_lo = (r_f32 - r_hi.astype(F32)).astype(BF16)
            router_pad = jnp.concatenate([r_hi, r_lo], axis=1)
            x2, h, route, counts = _outproj(oa, ob, x2, *norms, router_pad)
            x2 = _moe(h, route, counts, x2, *moe_w, j,
                      final_gain=final_norm[None, :] if last else None)
    return x2.reshape(batch, seq, D_MODEL)
```

```python
import functools

import numpy as np
import jax
import jax.numpy as jnp
from jax import lax
from jax.experimental import pallas as pl
from jax.experimental.pallas import tpu as pltpu

D_MODEL = 1024
HEAD_DIM = 64
N_HEADS_A = 8
N_HEADS_B = 8
N_KV_B = 2
DILATED_BRANCHES = ((128, 1), (512, 4), (2048, 16))
SWA_WINDOW = 128
BLK = 128
ROPE_THETA = 10000.0
D_FF = 3584
N_EXPERTS = 8
RMS_EPS = 1e-5
WIDTH_A = N_HEADS_A * HEAD_DIM
WIDTH_B = N_HEADS_B * HEAD_DIM
KV_WIDTH_B = N_KV_B * HEAD_DIM

LANES = 128
SUBLANES = 8
NEG_BIG = -1e30
VMEM_LIMIT = 56 * 1024 * 1024

F32 = jnp.float32
BF16 = jnp.bfloat16


def _params(sem):
    return pltpu.CompilerParams(dimension_semantics=sem, vmem_limit_bytes=VMEM_LIMIT)


def _rms(x, g):
    return x * lax.rsqrt(jnp.mean(x * x, axis=-1, keepdims=True) + RMS_EPS) * g


def _rope_table_kernel(pos_ref, inv_ref, cos_ref, sina_ref, sinb_ref):
    ang = pos_ref[...].astype(F32) * inv_ref[...]
    c = jnp.cos(ang)
    s = jnp.sin(ang)
    lane = lax.broadcasted_iota(jnp.int32, ang.shape, 1)
    upper = (lane & (HEAD_DIM - 1)) >= HEAD_DIM // 2
    cos_ref[...] = c
    sina_ref[...] = jnp.where(upper, s, 0.0)
    sinb_ref[...] = jnp.where(upper, 0.0, -s)


def _rope_tables(positions):
    seq = positions.shape[0]
    inv = ROPE_THETA ** (-jnp.arange(0, HEAD_DIM, 2, dtype=F32) / HEAD_DIM)
    inv = jnp.tile(inv, LANES // (HEAD_DIM // 2))[None, :]
    rows = 1024
    tab = jax.ShapeDtypeStruct((seq, LANES), F32)
    return pl.pallas_call(
        _rope_table_kernel,
        grid=(seq // rows,),
        in_specs=[pl.BlockSpec((rows, 1), lambda i: (i, 0)),
                  pl.BlockSpec((1, LANES), lambda i: (0, 0))],
        out_specs=[pl.BlockSpec((rows, LANES), lambda i: (i, 0))] * 3,
        out_shape=[tab, tab, tab],
        compiler_params=_params(("arbitrary",)),
        name="rope_tables",
    )(positions.reshape(seq, 1), inv)


LOG2E = 1.4426950408889634
_QSCALE = HEAD_DIM ** -0.5 * LOG2E
_SEGS = (("qa", WIDTH_A, True, _QSCALE), ("ka", WIDTH_A, True, 1.0),
         ("va", WIDTH_A, False, 1.0), ("qb", WIDTH_B, True, _QSCALE),
         ("kb", 2 * KV_WIDTH_B, True, 1.0), ("vb", 2 * KV_WIDTH_B, False, 1.0))


def _inproj_kernel(x_ref, g_ref, w_ref, cos_ref, sina_ref, sinb_ref, *out_refs):
    h = _rms(x_ref[...], g_ref[...]).astype(BF16)
    cos = cos_ref[...]
    sina = sina_ref[...]
    sinb = sinb_ref[...]
    col = 0
    for (_, width, roped, scale), o_ref in zip(_SEGS, out_refs):
        t = jnp.dot(h, w_ref[:, col:col + width], preferred_element_type=F32)
        col += width
        for j in range(width // LANES):
            tj = t[:, j * LANES:(j + 1) * LANES]
            if roped:
                tj = (tj * cos + pltpu.roll(tj, HEAD_DIM // 2, 1) * sina
                      + pltpu.roll(tj, LANES - HEAD_DIM // 2, 1) * sinb)
                if scale != 1.0:
                    tj = tj * scale
            o_ref[:, j * LANES:(j + 1) * LANES] = tj.astype(o_ref.dtype)


def _inproj(x2, g, w_ext, cos, sina, sinb, seq):
    tokens = x2.shape[0]
    tm = 512
    spt = seq // tm
    widths = [s[1] for s in _SEGS]
    total = sum(widths)
    return pl.pallas_call(
        _inproj_kernel,
        grid=(tokens // tm,),
        in_specs=[pl.BlockSpec((tm, D_MODEL), lambda i: (i, 0)),
                  pl.BlockSpec((1, D_MODEL), lambda i: (0, 0)),
                  pl.BlockSpec((D_MODEL, total), lambda i: (0, 0)),
                  pl.BlockSpec((tm, LANES), lambda i: (i % spt, 0)),
                  pl.BlockSpec((tm, LANES), lambda i: (i % spt, 0)),
                  pl.BlockSpec((tm, LANES), lambda i: (i % spt, 0))],
        out_specs=[pl.BlockSpec((tm, w), lambda i: (i, 0)) for w in widths],
        out_shape=[jax.ShapeDtypeStruct((tokens, w), BF16) for w in widths],
        compiler_params=_params(("arbitrary",)),
        name="inproj_rope",
    )(x2, g, w_ext, cos, sina, sinb)


def _band_bias(max_dist):
    p = np.arange(BLK)[:, None]
    c = np.arange(2 * BLK)[None, :]
    first = c <= p
    dist = BLK + p - c
    main = (dist >= 0) & (dist <= max_dist)
    bias = np.where(np.stack([first, main]), 0.0, NEG_BIG).astype(np.float32)
    return jnp.asarray(np.concatenate([bias, bias], axis=1))


def _low_lanes():
    return lax.broadcasted_iota(jnp.int32, (BLK, LANES), 1) < HEAD_DIM


def _band_block(q, kb, vb, bias, sink=None):
    low = _low_lanes()
    zero = jnp.zeros_like(q)
    qq = jnp.concatenate([jnp.where(low, q, zero), jnp.where(low, zero, q)], axis=0)
    s = lax.dot_general(qq, kb, (((1,), (1,)), ((), ())), preferred_element_type=F32) + bias
    m = jnp.max(s, axis=1, keepdims=True)
    if sink is not None:
        m = jnp.maximum(m, sink)
    p = jnp.exp2(s - m)
    l = jnp.sum(p, axis=1, keepdims=True)
    if sink is not None:
        l = l + jnp.exp2(sink - m)
    o = jnp.dot(p.astype(BF16), vb, preferred_element_type=F32) * (1.0 / l)
    lse = m + jnp.log2(l)
    o_pair = jnp.where(low, o[:BLK], o[BLK:])
    lse_pair = jnp.where(low, jnp.broadcast_to(lse[:BLK], (BLK, LANES)),
                         jnp.broadcast_to(lse[BLK:], (BLK, LANES)))
    return o_pair, lse_pair


def _merge(o_a, lse_a, o_b, lse_b, want_lse=True):
    mx = jnp.maximum(lse_a, lse_b)
    ea = jnp.exp2(lse_a - mx)
    eb = jnp.exp2(lse_b - mx)
    den = ea + eb
    o = (ea * o_a + eb * o_b) * (1.0 / den)
    return (o, mx + jnp.log2(den)) if want_lse else (o, None)


_CLASSES = 4
_LOOP_UNROLL = 16


def _dilated_kernel(q_ref, k_ref, v_ref, bias_ref, o_ref,
                    q4, k4, v4, tmp, so4, sl4, il_o, il_l, *, seq):
    cls = seq // _CLASSES
    nb4 = cls // BLK
    nb16 = seq // 16 // BLK
    sh4, sh16 = nb4.bit_length() - 1, nb16.bit_length() - 1
    chunk = 512
    shc = (cls // chunk).bit_length() - 1

    for src, dst in ((q_ref, q4), (k_ref, k4), (v_ref, v4)):
        def widen(i, c, src=src):
            r0 = pl.multiple_of(i * chunk, chunk)
            tmp[pl.ds(r0, chunk), :] = src[0, pl.ds(r0, chunk), :].astype(F32)
            return c
        lax.fori_loop(0, seq // chunk, widen, 0)

        def regroup(i, c, dst=dst):
            r = lax.shift_right_logical(i, shc)
            j = i & (cls // chunk - 1)
            d0 = pl.multiple_of(r * cls + j * chunk, chunk)
            dst[pl.ds(d0, chunk), :] = tmp[pl.ds(r + _CLASSES * chunk * j, chunk,
                                                 stride=_CLASSES), :]
            return c
        lax.fori_loop(0, seq // chunk, regroup, 0)

    def dil4(idx, c):
        r = lax.shift_right_logical(idx, sh4)
        n = idx & (nb4 - 1)
        q0 = pl.multiple_of(r * cls + n * BLK, BLK)
        k0 = pl.multiple_of(r * cls + jnp.maximum(n - 1, 0) * BLK, BLK)
        o, lse = _band_block(q4[pl.ds(q0, BLK), :].astype(BF16),
                             k4[pl.ds(k0, 2 * BLK), :].astype(BF16),
                             v4[pl.ds(k0, 2 * BLK), :].astype(BF16),
                             bias_ref[jnp.minimum(n, 1)])
        so4[pl.ds(q0, BLK), :] = o
        sl4[pl.ds(q0, BLK), :] = lse
        return c
    lax.fori_loop(0, _CLASSES * nb4, dil4, 0, unroll=_LOOP_UNROLL)

    def dil16(idx, c):
        n = idx & (nb16 - 1)
        sub = lax.shift_right_logical(idx, sh16) & (_CLASSES - 1)
        r = lax.shift_right_logical(idx, sh16 + 2)
        q0 = r * cls + sub + _CLASSES * BLK * n
        k0 = r * cls + sub + _CLASSES * BLK * jnp.maximum(n - 1, 0)
        qs = pl.ds(q0, BLK, stride=_CLASSES)
        ks = pl.ds(k0, 2 * BLK, stride=_CLASSES)
        o_s, l_s = so4[qs, :], sl4[qs, :]
        o, lse = _band_block(q4[qs, :].astype(BF16), k4[ks, :].astype(BF16),
                             v4[ks, :].astype(BF16), bias_ref[jnp.minimum(n, 1)])
        o, lse = _merge(o_s, l_s, o, lse)
        so4[qs, :] = o
        sl4[qs, :] = lse
        return c
    lax.fori_loop(0, 16 * nb16, dil16, 0, unroll=_LOOP_UNROLL)

    piece = BLK // _CLASSES

    def dil1(n, c):
        q0 = pl.multiple_of(n * BLK, BLK)
        k0 = pl.multiple_of(jnp.maximum(n - 1, 0) * BLK, BLK)
        for r in range(_CLASSES):
            s0 = pl.multiple_of(r * cls + n * piece, piece)
            il_o[pl.ds(r, piece, stride=_CLASSES), :] = so4[pl.ds(s0, piece), :]
            il_l[pl.ds(r, piece, stride=_CLASSES), :] = sl4[pl.ds(s0, piece), :]
        o_s, l_s = il_o[...], il_l[...]
        o, lse = _band_block(q_ref[0, pl.ds(q0, BLK), :], k_ref[0, pl.ds(k0, 2 * BLK), :],
                             v_ref[0, pl.ds(k0, 2 * BLK), :], bias_ref[jnp.minimum(n, 1)])
        o, _ = _merge(o_s, l_s, o, lse, want_lse=False)
        o_ref[0, pl.ds(q0, BLK), :] = o.astype(o_ref.dtype)
        return c
    lax.fori_loop(0, seq // BLK, dil1, 0, unroll=_LOOP_UNROLL)


def _dilated_attention(q, k, v):
    assert DILATED_BRANCHES == ((128, 1), (512, 4), (2048, 16))
    batch, seq, width = q.shape
    assert seq % (16 * BLK) == 0 and (seq & (seq - 1)) == 0
    blk = pl.BlockSpec((1, seq, LANES), lambda b, c: (b, 0, c))
    rows = pltpu.VMEM((seq, LANES), F32)
    small = pltpu.VMEM((BLK, LANES), F32)
    return pl.pallas_call(
        functools.partial(_dilated_kernel, seq=seq),
        grid=(batch, width // LANES),
        in_specs=[blk, blk, blk, pl.BlockSpec((2, 2 * BLK, 2 * BLK), lambda b, c: (0, 0, 0))],
        out_specs=blk,
        out_shape=jax.ShapeDtypeStruct((batch, seq, width), BF16),
        scratch_shapes=[rows] * 6 + [small] * 2,
        compiler_params=_params(("arbitrary", "arbitrary")),
        name="dilated_attn",
    )(q, k, v, _band_bias(BLK))


def _swa_kernel(sink_ref, q_ref, k_ref, v_ref, bias_ref, o_ref, *, n_blocks):
    c = pl.program_id(1)
    row = lax.broadcasted_iota(jnp.int32, (2 * BLK, 1), 0)
    sink = jnp.where(row < BLK, sink_ref[2 * c], sink_ref[2 * c + 1]) * LOG2E

    def body(n, carry):
        q0 = pl.multiple_of(n * BLK, BLK)
        k0 = pl.multiple_of(jnp.maximum(n - 1, 0) * BLK, BLK)
        o, _ = _band_block(q_ref[0, pl.ds(q0, BLK), :], k_ref[0, pl.ds(k0, 2 * BLK), :],
                           v_ref[0, pl.ds(k0, 2 * BLK), :], bias_ref[jnp.minimum(n, 1)], sink)
        o_ref[0, pl.ds(q0, BLK), :] = o.astype(o_ref.dtype)
        return carry
    lax.fori_loop(0, n_blocks, body, 0, unroll=_LOOP_UNROLL)


def _swa_attention(q, k, v, sinks):
    batch, seq, width = q.shape
    qblk = pl.BlockSpec((1, seq, LANES), lambda b, c, s: (b, 0, c))
    kblk = pl.BlockSpec((1, seq, LANES), lambda b, c, s: (b, 0, c // 2))
    grid_spec = pltpu.PrefetchScalarGridSpec(
        num_scalar_prefetch=1, grid=(batch, width // LANES),
        in_specs=[qblk, kblk, kblk,
                  pl.BlockSpec((2, 2 * BLK, 2 * BLK), lambda b, c, s: (0, 0, 0))],
        out_specs=qblk)
    return pl.pallas_call(
        functools.partial(_swa_kernel, n_blocks=seq // BLK), grid_spec=grid_spec,
        out_shape=jax.ShapeDtypeStruct((batch, seq, width), BF16),
        compiler_params=_params(("arbitrary", "arbitrary")),
        name="swa_sink_attn",
    )(sinks, q, k, v, _band_bias(SWA_WINDOW - 1))


ROW_TM = 512
RUN_ALIGN = SUBLANES
SORT_ROWS = 2 * ROW_TM + N_EXPERTS * RUN_ALIGN
_R_G1, _R_G2, _R_POS1, _R_POS2 = range(4)
_T_COUNT, _T_OFF, _T_BEFORE = range(3)


def _route(logits, count_ref):
    tm = logits.shape[0]
    lane = lax.broadcasted_iota(jnp.int32, logits.shape, 1)
    v1 = jnp.max(logits, axis=1, keepdims=True)
    i1 = jnp.min(jnp.where(logits == v1, lane, LANES), axis=1, keepdims=True)
    rest = jnp.where(lane == i1, NEG_BIG, logits)
    v2 = jnp.max(rest, axis=1, keepdims=True)
    i2 = jnp.min(jnp.where(rest == v2, lane, LANES), axis=1, keepdims=True)
    e2 = jnp.exp(v2 - v1)
    g1 = 1.0 / (1.0 + e2)
    g2 = e2 / (1.0 + e2)
    sel = jnp.where((lane == i1) | (lane == i2), 1.0, 0.0)
    r = lax.broadcasted_iota(jnp.int32, (tm, tm), 0)
    c = lax.broadcasted_iota(jnp.int32, (tm, tm), 1)
    tri = jnp.where(c < r, 1.0, 0.0).astype(BF16)
    local = jnp.dot(tri, sel.astype(BF16), preferred_element_type=F32)
    count = jnp.sum(sel, axis=0, keepdims=True)
    padded = jnp.floor((count + (RUN_ALIGN - 1)) * (1.0 / RUN_ALIGN)) * RUN_ALIGN
    run_end = jnp.broadcast_to(padded, (SUBLANES, LANES))
    for shift in (1, 2, 4):
        run_end = run_end + pltpu.roll(run_end, shift, 1)
    offset = run_end[0:1] - padded
    pos = local + offset
    pos1 = jnp.sum(jnp.where(lane == i1, pos, 0.0), axis=1, keepdims=True)
    pos2 = jnp.sum(jnp.where(lane == i2, pos, 0.0), axis=1, keepdims=True)
    before = count_ref[...]
    count_ref[...] = before + padded
    rec = jnp.zeros(logits.shape, F32)
    for ln, col in ((_R_G1, g1), (_R_G2, g2), (_R_POS1, pos1), (_R_POS2, pos2)):
        rec = jnp.where(lane == ln, col, rec)
    row = lax.broadcasted_iota(jnp.int32, (SUBLANES, LANES), 0)
    tile = jnp.zeros((SUBLANES, LANES), F32)
    for rw, val in ((_T_COUNT, padded), (_T_OFF, offset), (_T_BEFORE, before)):
        tile = jnp.where(row == rw, val, tile)
    return rec, tile


def _outproj_kernel(*refs, with_router):
    oa_ref, ob_ref, x_ref, ga_ref, gb_ref, w_ref, gf_ref = refs[:7]
    rest = refs[7:]
    if with_router:
        r_ref, xo_ref, ho_ref, route_ref, tile_ref, count_ref = rest
    else:
        xo_ref, ho_ref = rest
    na = _rms(oa_ref[...].astype(F32), ga_ref[...]).astype(BF16)
    nb = _rms(ob_ref[...].astype(F32), gb_ref[...]).astype(BF16)
    xn = (x_ref[...] + jnp.dot(na, w_ref[:WIDTH_A, :], preferred_element_type=F32)
          + jnp.dot(nb, w_ref[WIDTH_A:, :], preferred_element_type=F32))
    xo_ref[...] = xn
    hf = _rms(xn, gf_ref[...])
    h_hi = hf.astype(BF16)
    ho_ref[...] = h_hi
    if with_router:
        @pl.when(pl.program_id(0) == 0)
        def _():
            count_ref[...] = jnp.zeros_like(count_ref)
        h_lo = (hf - h_hi.astype(F32)).astype(BF16)
        t = jnp.dot(h_hi, r_ref[...], preferred_element_type=F32)
        logits = (t[:, :LANES] + t[:, LANES:]
                  + jnp.dot(h_lo, r_ref[:, :LANES], preferred_element_type=F32))
        lane = lax.broadcasted_iota(jnp.int32, logits.shape, 1)
        logits = jnp.where(lane < N_EXPERTS, logits, NEG_BIG)
        route_ref[...], tile_ref[...] = _route(logits, count_ref)


def _outproj(oa, ob, x2, ga, gb, w_out, gf, router_pad=None):
    tokens = x2.shape[0]
    tm = ROW_TM
    with_router = router_pad is not None
    row = lambda w: pl.BlockSpec((tm, w), lambda i: (i, 0))
    const = lambda a, b: pl.BlockSpec((a, b), lambda i: (0, 0))
    in_specs = [row(WIDTH_A), row(WIDTH_B), row(D_MODEL), const(1, WIDTH_A), const(1, WIDTH_B),
                const(D_MODEL, D_MODEL), const(1, D_MODEL)]
    out_specs = [row(D_MODEL), row(D_MODEL)]
    out_shape = [jax.ShapeDtypeStruct((tokens, D_MODEL), F32),
                 jax.ShapeDtypeStruct((tokens, D_MODEL), BF16)]
    args = [oa, ob, x2, ga, gb, w_out, gf]
    if with_router:
        in_specs.append(const(D_MODEL, 2 * LANES))
        out_specs += [row(LANES), pl.BlockSpec((SUBLANES, LANES), lambda i: (i, 0)),
                      const(1, LANES)]
        out_shape += [jax.ShapeDtypeStruct((tokens, LANES), F32),
                      jax.ShapeDtypeStruct((tokens // tm * SUBLANES, LANES), F32),
                      jax.ShapeDtypeStruct((1, LANES), F32)]
        args.append(router_pad)
    return pl.pallas_call(
        functools.partial(_outproj_kernel, with_router=with_router),
        grid=(tokens // tm,), in_specs=in_specs, out_specs=out_specs, out_shape=out_shape,
        compiler_params=_params(("arbitrary",)),
        name="mix_outproj_router" if with_router else "mix_outproj",
    )(*args)


def _silu(x):
    return x * (1.0 / (1.0 + jnp.exp(-x)))


FFN_TM = 1024
FFN_TF = 1792
FFN_TC = 512


def _swiglu_chunk(h, wg_ref, wu_ref, wd_ref, acc_ref):
    for c0 in range(0, FFN_TF, FFN_TC):
        c1 = min(c0 + FFN_TC, FFN_TF)
        g = jnp.dot(h, wg_ref[:, c0:c1], preferred_element_type=F32)
        u = jnp.dot(h, wu_ref[:, c0:c1], preferred_element_type=F32)
        acc_ref[...] += jnp.dot((_silu(g) * u).astype(BF16), wd_ref[c0:c1, :],
                                preferred_element_type=F32)


def _ffn_kernel(*refs, final):
    h_ref, wg_ref, wu_ref, wd_ref, x_ref = refs[:5]
    if final:
        gfin_ref, o_ref, acc_ref = refs[5:]
    else:
        o_ref, acc_ref = refs[5:]
    f = pl.program_id(1)

    @pl.when(f == 0)
    def _():
        acc_ref[...] = x_ref[...]

    _swiglu_chunk(h_ref[...], wg_ref.at[0], wu_ref.at[0], wd_ref.at[0], acc_ref)

    @pl.when(f == pl.num_programs(1) - 1)
    def _():
        o_ref[...] = _rms(acc_ref[...], gfin_ref[...]) if final else acc_ref[...]


def _ffn(h, x2, wg, wu, wd, layer, final_gain=None):
    tokens = x2.shape[0]
    tm, tf = FFN_TM, FFN_TF
    final = final_gain is not None
    tok = lambda w: pl.BlockSpec((tm, w), lambda i, f: (i, 0))
    in_specs = [tok(D_MODEL),
                pl.BlockSpec((1, D_MODEL, tf), lambda i, f: (layer, 0, f)),
                pl.BlockSpec((1, D_MODEL, tf), lambda i, f: (layer, 0, f)),
                pl.BlockSpec((1, tf, D_MODEL), lambda i, f: (layer, f, 0)),
                tok(D_MODEL)]
    args = [h, wg, wu, wd, x2]
    if final:
        in_specs.append(pl.BlockSpec((1, D_MODEL), lambda i, f: (0, 0)))
        args.append(final_gain)
    return pl.pallas_call(
        functools.partial(_ffn_kernel, final=final),
        grid=(tokens // tm, D_FF // tf), in_specs=in_specs, out_specs=tok(D_MODEL),
        out_shape=jax.ShapeDtypeStruct((tokens, D_MODEL), F32),
        scratch_shapes=[pltpu.VMEM((tm, D_MODEL), F32)],
        compiler_params=_params(("arbitrary", "arbitrary")),
        name="dense_swiglu",
    )(*args)


def _run_copies(meta_ref, make_copy):
    for act in ("start", "wait"):
        for e in range(N_EXPERTS):
            n = pl.multiple_of(meta_ref[0, 0, N_EXPERTS + e], RUN_ALIGN)

            @pl.when(n > 0)
            def _(e=e, n=n, act=act):
                off = pl.multiple_of(meta_ref[0, 0, e], RUN_ALIGN)
                dst = pl.multiple_of(meta_ref[0, 0, 2 * N_EXPERTS + e], RUN_ALIGN)
                getattr(make_copy(off, n, dst), act)()


def _sort_matrix(route, n_rows):
    rec_t = route.T
    p1 = rec_t[_R_POS1:_R_POS1 + 1, :].astype(jnp.int32)
    p2 = rec_t[_R_POS2:_R_POS2 + 1, :].astype(jnp.int32)
    ip = lax.broadcasted_iota(jnp.int32, (n_rows, route.shape[0]), 0)
    return jnp.where((ip == p1) | (ip == p2), 1.0, 0.0).astype(BF16)


def _dispatch_kernel(tend_ref, meta_ref, h_ref, route_ref, out_ref, sort_ref, zero_ref, sem,
                     *, max_tail):
    n_tiles = out_ref.shape[0] // FFN_TM

    @pl.when(pl.program_id(0) == 0)
    def _():
        zero_ref[...] = jnp.zeros_like(zero_ref)

        def zero_tile(t):
            return pltpu.make_async_copy(zero_ref, out_ref.at[pl.ds(t * FFN_TM, FFN_TM)], sem)

        def has_tiles(e):
            return tend_ref[e] > (tend_ref[e - 1] if e else 0)

        for act in ("start", "wait"):
            for e in range(N_EXPERTS):
                @pl.when(has_tiles(e))
                def _(e=e, act=act):
                    getattr(zero_tile(tend_ref[e] - 1), act)()
            for k in range(max_tail):
                t = tend_ref[N_EXPERTS - 1] + k

                @pl.when(t < n_tiles)
                def _(t=t, act=act):
                    getattr(zero_tile(t), act)()

    sort_ref[...] = jnp.dot(_sort_matrix(route_ref[...], SORT_ROWS), h_ref[...],
                            preferred_element_type=F32)
    _run_copies(meta_ref, lambda off, n, dst: pltpu.make_async_copy(
        sort_ref.at[pl.ds(off, n)], out_ref.at[pl.ds(dst, n)], sem))


def _dispatch(h, route, meta, tile_end, n_slots):
    tokens = h.shape[0]
    tm = ROW_TM
    grid_spec = pltpu.PrefetchScalarGridSpec(
        num_scalar_prefetch=1, grid=(tokens // tm,),
        in_specs=[pl.BlockSpec((1, 1, 32), lambda i, te: (i, 0, 0), memory_space=pltpu.SMEM),
                  pl.BlockSpec((tm, D_MODEL), lambda i, te: (i, 0)),
                  pl.BlockSpec((tm, LANES), lambda i, te: (i, 0))],
        out_specs=pl.BlockSpec(memory_space=pl.ANY),
        scratch_shapes=[pltpu.VMEM((SORT_ROWS, D_MODEL), F32),
                        pltpu.VMEM((FFN_TM, D_MODEL), F32), pltpu.SemaphoreType.DMA(())])
    max_tail = n_slots // FFN_TM - 2 * tokens // FFN_TM
    return pl.pallas_call(
        functools.partial(_dispatch_kernel, max_tail=max_tail), grid_spec=grid_spec,
        out_shape=jax.ShapeDtypeStruct((n_slots, D_MODEL), F32),
        compiler_params=_params(("arbitrary",)),
        name="moe_dispatch",
    )(tile_end, meta, h, route)


def _grouped_kernel(texp_ref, nt_ref, h_ref, wg_ref, wu_ref, wd_ref, y_ref, hb_ref, acc_ref):
    del texp_ref
    i = pl.program_id(0)
    f = pl.program_id(1)

    @pl.when(i < nt_ref[0])
    def _():
        @pl.when(f == 0)
        def _():
            hb_ref[...] = h_ref[...].astype(BF16)
            acc_ref[...] = jnp.zeros_like(acc_ref)

        _swiglu_chunk(hb_ref[...], wg_ref.at[0, 0], wu_ref.at[0, 0], wd_ref.at[0, 0], acc_ref)

        @pl.when(f == pl.num_programs(1) - 1)
        def _():
            y_ref[...] = acc_ref[...]

    @pl.when((i >= nt_ref[0]) & (f == 0))
    def _():
        y_ref[...] = jnp.zeros_like(y_ref)


def _grouped_swiglu(hs, tile_expert, n_tiles_used, wg, wu, wd, layer):
    n_slots = hs.shape[0]
    tm, tf = FFN_TM, FFN_TF
    nf = D_FF // tf
    last = lambda i, nt: jnp.minimum(i, nt[0] - 1)
    fcol = lambda i, f, nt: jnp.where(i < nt[0], f, nf - 1)
    grid_spec = pltpu.PrefetchScalarGridSpec(
        num_scalar_prefetch=2, grid=(n_slots // tm, nf),
        in_specs=[pl.BlockSpec((tm, D_MODEL), lambda i, f, te, nt: (last(i, nt), 0)),
                  pl.BlockSpec((1, 1, D_MODEL, tf),
                               lambda i, f, te, nt: (layer, te[i], 0, fcol(i, f, nt))),
                  pl.BlockSpec((1, 1, D_MODEL, tf),
                               lambda i, f, te, nt: (layer, te[i], 0, fcol(i, f, nt))),
                  pl.BlockSpec((1, 1, tf, D_MODEL),
                               lambda i, f, te, nt: (layer, te[i], fcol(i, f, nt), 0))],
        out_specs=pl.BlockSpec((tm, D_MODEL), lambda i, f, te, nt: (i, 0)),
        scratch_shapes=[pltpu.VMEM((tm, D_MODEL), BF16), pltpu.VMEM((tm, D_MODEL), F32)])
    return pl.pallas_call(
        _grouped_kernel, grid_spec=grid_spec,
        out_shape=jax.ShapeDtypeStruct((n_slots, D_MODEL), F32),
        compiler_params=_params(("arbitrary", "arbitrary")),
        name="moe_grouped_swiglu",
    )(tile_expert, n_tiles_used, hs, wg, wu, wd)


def _combine_kernel(*refs, final):
    meta_ref, route_ref, x_ref, y_ref = refs[:4]
    if final:
        gfin_ref, o_ref, sort_ref, sem = refs[4:]
    else:
        o_ref, sort_ref, sem = refs[4:]
    tm = x_ref.shape[0]
    sort_ref[pl.ds(2 * tm, SORT_ROWS - 2 * tm), :] = jnp.zeros((SORT_ROWS - 2 * tm, D_MODEL), F32)
    _run_copies(meta_ref, lambda off, n, dst: pltpu.make_async_copy(
        y_ref.at[pl.ds(dst, n)], sort_ref.at[pl.ds(off, n)], sem))
    ys = sort_ref[...].astype(BF16)
    route = route_ref[...]
    ip = lax.broadcasted_iota(jnp.int32, (tm, SORT_ROWS), 1)

    def pick(col):
        pos = route[:, col:col + 1].astype(jnp.int32)
        return jnp.dot(jnp.where(ip == pos, 1.0, 0.0).astype(BF16), ys,
                       preferred_element_type=F32)

    out = (x_ref[...] + route[:, _R_G1:_R_G1 + 1] * pick(_R_POS1)
           + route[:, _R_G2:_R_G2 + 1] * pick(_R_POS2))
    o_ref[...] = _rms(out, gfin_ref[...]) if final else out


def _combine(y, route, meta, x2, final_gain=None):
    tokens = x2.shape[0]
    tm = ROW_TM
    final = final_gain is not None
    in_specs = [pl.BlockSpec((1, 1, 32), lambda i: (i, 0, 0), memory_space=pltpu.SMEM),
                pl.BlockSpec((tm, LANES), lambda i: (i, 0)),
                pl.BlockSpec((tm, D_MODEL), lambda i: (i, 0)),
                pl.BlockSpec(memory_space=pl.ANY)]
    args = [meta, route, x2, y]
    if final:
        in_specs.append(pl.BlockSpec((1, D_MODEL), lambda i: (0, 0)))
        args.append(final_gain)
    return pl.pallas_call(
        functools.partial(_combine_kernel, final=final),
        grid=(tokens // tm,), in_specs=in_specs,
        out_specs=pl.BlockSpec((tm, D_MODEL), lambda i: (i, 0)),
        out_shape=jax.ShapeDtypeStruct((tokens, D_MODEL), F32),
        scratch_shapes=[pltpu.VMEM((SORT_ROWS, D_MODEL), F32), pltpu.SemaphoreType.DMA(())],
        compiler_params=_params(("arbitrary",)),
        name="moe_combine",
    )(*args)


def _routing_tables(tile_rec, counts, tokens):
    tm = FFN_TM
    n_tok_tiles = tokens // ROW_TM
    max_rows = 2 * tokens + n_tok_tiles * N_EXPERTS * (RUN_ALIGN - 1)
    n_tiles = (max_rows + N_EXPERTS * (tm - 1)) // tm
    cnt = counts[0, :N_EXPERTS].astype(jnp.int32)
    tiles = (cnt + tm - 1) // tm
    tile_end = jnp.cumsum(tiles)
    start = (tile_end - tiles) * tm
    rec = tile_rec.reshape(n_tok_tiles, SUBLANES, LANES)[:, :, :N_EXPERTS].astype(jnp.int32)
    meta = jnp.concatenate([rec[:, _T_OFF], rec[:, _T_COUNT], start[None, :] + rec[:, _T_BEFORE],
                            jnp.zeros((n_tok_tiles, 32 - 3 * N_EXPERTS), jnp.int32)], axis=1)
    tile_ids = jnp.arange(n_tiles, dtype=jnp.int32)[:, None]
    tile_expert = jnp.minimum(jnp.sum(tile_ids >= tile_end[None, :], axis=1),
                              N_EXPERTS - 1).astype(jnp.int32)
    return (meta.reshape(n_tok_tiles, 1, 32), tile_expert, tile_end.astype(jnp.int32),
            n_tiles * tm)


def _moe(h, route, tile_rec, counts, x2, wg, wu, wd, layer, final_gain=None):
    meta, tile_expert, tile_end, n_slots = _routing_tables(tile_rec, counts, x2.shape[0])
    hs = _dispatch(h, route, meta, tile_end, n_slots)
    y = _grouped_swiglu(hs, tile_expert, tile_end[-1:], wg, wu, wd, layer)
    return _combine(y, route, meta, x2, final_gain)


def _widen_w_in(w):
    a = 3 * WIDTH_A + WIDTH_B
    kb = w[:, a:a + KV_WIDTH_B]
    vb = w[:, a + KV_WIDTH_B:]
    dup = lambda t: jnp.concatenate(
        [t[:, h * HEAD_DIM:(h + 1) * HEAD_DIM] for h in range(N_KV_B) for _ in range(2)], axis=1)
    return jnp.concatenate([w[:, :a], dup(kb), dup(vb)], axis=1).astype(BF16)


def kernel(x, positions, attn_norm, w_in, mix_norm_a, mix_norm_b, sinks, w_out, ffn_norm,
           dense_w_gate, dense_w_up, dense_w_down, router, moe_w_gate, moe_w_up, moe_w_down,
           final_norm):
    batch, seq, _ = x.shape
    depth = w_in.shape[0]
    tokens = batch * seq
    cos, sina, sinb = _rope_tables(positions)
    x2 = x.reshape(tokens, D_MODEL)
    seq3 = lambda t: t.reshape(batch, seq, t.shape[-1])
    dense_w = [w.astype(BF16) for w in (dense_w_gate, dense_w_up, dense_w_down)]
    moe_w = [w.astype(BF16) for w in (moe_w_gate, moe_w_up, moe_w_down)]
    for i in range(depth):
        qa, ka, va, qb, kb, vb = _inproj(x2, attn_norm[i][None, :], _widen_w_in(w_in[i]),
                                         cos, sina, sinb, seq)
        oa = _dilated_attention(seq3(qa), seq3(ka), seq3(va)).reshape(tokens, WIDTH_A)
        ob = _swa_attention(seq3(qb), seq3(kb), seq3(vb), sinks[i]).reshape(tokens, WIDTH_B)
        j = i // 2
        last = i == depth - 1
        norms = (mix_norm_a[i][None, :], mix_norm_b[i][None, :], w_out[i].astype(BF16),
                 ffn_norm[i][None, :])
        if i % 2 == 0:
            x2, h = _outproj(oa, ob, x2, *norms)
            x2 = _ffn(h, x2, *dense_w, j, final_gain=final_norm[None, :] if last else None)
        else:
            r_f32 = jnp.pad(router[j], ((0, 0), (0, LANES - N_EXPERTS)))
            r_hi = r_f32.astype(BF16)
            r_lo = (r_f32 - r_hi.astype(F32)).astype(BF16)
            router_pad = jnp.concatenate([r_hi, r_lo], axis=1)
            x2, h, route, tile_rec, counts = _outproj(oa, ob, x2, *norms, router_pad)
            x2 = _moe(h, route, tile_rec, counts, x2, *moe_w, j,
                      final_gain=final_norm[None, :] if last else None)
    return x2.reshape(batch, seq, D_MODEL)
```

```python
import functools

import numpy as np
import jax
import jax.numpy as jnp
from jax import lax
from jax.experimental import pallas as pl
from jax.experimental.pallas import tpu as pltpu

D_MODEL = 1024
HEAD_DIM = 64
N_HEADS_A = 8
N_HEADS_B = 8
N_KV_B = 2
DILATED_BRANCHES = ((128, 1), (512, 4), (2048, 16))
SWA_WINDOW = 128
BLK = 128
ROPE_THETA = 10000.0
D_FF = 3584
N_EXPERTS = 8
RMS_EPS = 1e-5
WIDTH_A = N_HEADS_A * HEAD_DIM
WIDTH_B = N_HEADS_B * HEAD_DIM
KV_WIDTH_B = N_KV_B * HEAD_DIM

LANES = 128
SUBLANES = 8
NEG_BIG = -1e30
VMEM_LIMIT = 56 * 1024 * 1024

F32 = jnp.float32
BF16 = jnp.bfloat16


def _params(sem):
    return pltpu.CompilerParams(dimension_semantics=sem, vmem_limit_bytes=VMEM_LIMIT)


def _rms(x, g):
    return x * lax.rsqrt(jnp.mean(x * x, axis=-1, keepdims=True) + RMS_EPS) * g


def _rope_table_kernel(pos_ref, inv_ref, cos_ref, sina_ref, sinb_ref):
    ang = pos_ref[...].astype(F32) * inv_ref[...]
    c = jnp.cos(ang)
    s = jnp.sin(ang)
    lane = lax.broadcasted_iota(jnp.int32, ang.shape, 1)
    upper = (lane & (HEAD_DIM - 1)) >= HEAD_DIM // 2
    cos_ref[...] = c
    sina_ref[...] = jnp.where(upper, s, 0.0)
    sinb_ref[...] = jnp.where(upper, 0.0, -s)


def _rope_tables(positions):
    seq = positions.shape[0]
    inv = ROPE_THETA ** (-jnp.arange(0, HEAD_DIM, 2, dtype=F32) / HEAD_DIM)
    inv = jnp.tile(inv, LANES // (HEAD_DIM // 2))[None, :]
    rows = 1024
    tab = jax.ShapeDtypeStruct((seq, LANES), F32)
    return pl.pallas_call(
        _rope_table_kernel,
        grid=(seq // rows,),
        in_specs=[pl.BlockSpec((rows, 1), lambda i: (i, 0)),
                  pl.BlockSpec((1, LANES), lambda i: (0, 0))],
        out_specs=[pl.BlockSpec((rows, LANES), lambda i: (i, 0))] * 3,
        out_shape=[tab, tab, tab],
        compiler_params=_params(("arbitrary",)),
        name="rope_tables",
    )(positions.reshape(seq, 1), inv)


LOG2E = 1.4426950408889634
_QSCALE = HEAD_DIM ** -0.5 * LOG2E
_SEGS = (("qa", WIDTH_A, True, _QSCALE), ("ka", WIDTH_A, True, 1.0),
         ("va", WIDTH_A, False, 1.0), ("qb", WIDTH_B, True, _QSCALE),
         ("kb", 2 * KV_WIDTH_B, True, 1.0), ("vb", 2 * KV_WIDTH_B, False, 1.0))


def _inproj_kernel(x_ref, g_ref, w_ref, cos_ref, sina_ref, sinb_ref, *out_refs):
    h = _rms(x_ref[...], g_ref[...]).astype(BF16)
    cos = cos_ref[...]
    sina = sina_ref[...]
    sinb = sinb_ref[...]
    col = 0
    for (_, width, roped, scale), o_ref in zip(_SEGS, out_refs):
        t = jnp.dot(h, w_ref[:, col:col + width], preferred_element_type=F32)
        col += width
        for j in range(width // LANES):
            tj = t[:, j * LANES:(j + 1) * LANES]
            if roped:
                tj = (tj * cos + pltpu.roll(tj, HEAD_DIM // 2, 1) * sina
                      + pltpu.roll(tj, LANES - HEAD_DIM // 2, 1) * sinb)
                if scale != 1.0:
                    tj = tj * scale
            o_ref[:, j * LANES:(j + 1) * LANES] = tj.astype(o_ref.dtype)


def _inproj(x2, g, w_ext, cos, sina, sinb, seq):
    tokens = x2.shape[0]
    tm = 512
    spt = seq // tm
    widths = [s[1] for s in _SEGS]
    total = sum(widths)
    return pl.pallas_call(
        _inproj_kernel,
        grid=(tokens // tm,),
        in_specs=[pl.BlockSpec((tm, D_MODEL), lambda i: (i, 0)),
                  pl.BlockSpec((1, D_MODEL), lambda i: (0, 0)),
                  pl.BlockSpec((D_MODEL, total), lambda i: (0, 0)),
                  pl.BlockSpec((tm, LANES), lambda i: (i % spt, 0)),
                  pl.BlockSpec((tm, LANES), lambda i: (i % spt, 0)),
                  pl.BlockSpec((tm, LANES), lambda i: (i % spt, 0))],
        out_specs=[pl.BlockSpec((tm, w), lambda i: (i, 0)) for w in widths],
        out_shape=[jax.ShapeDtypeStruct((tokens, w), BF16) for w in widths],
        compiler_params=_params(("arbitrary",)),
        name="inproj_rope",
    )(x2, g, w_ext, cos, sina, sinb)


def _band_bias(max_dist):
    p = np.arange(BLK)[:, None]
    c = np.arange(2 * BLK)[None, :]
    first = c <= p
    dist = BLK + p - c
    main = (dist >= 0) & (dist <= max_dist)
    bias = np.where(np.stack([first, main]), 0.0, NEG_BIG).astype(np.float32)
    return jnp.asarray(np.concatenate([bias, bias], axis=1))


def _low_lanes():
    return lax.broadcasted_iota(jnp.int32, (BLK, LANES), 1) < HEAD_DIM


def _band_block(q, kb, vb, bias, sink=None):
    low = _low_lanes()
    zero = jnp.zeros_like(q)
    qq = jnp.concatenate([jnp.where(low, q, zero), jnp.where(low, zero, q)], axis=0)
    s = lax.dot_general(qq, kb, (((1,), (1,)), ((), ())), preferred_element_type=F32) + bias
    m = jnp.max(s, axis=1, keepdims=True)
    if sink is not None:
        m = jnp.maximum(m, sink)
    p = jnp.exp2(s - m)
    l = jnp.sum(p, axis=1, keepdims=True)
    if sink is not None:
        l = l + jnp.exp2(sink - m)
    o = jnp.dot(p.astype(BF16), vb, preferred_element_type=F32) * (1.0 / l)
    lse = m + jnp.log2(l)
    o_pair = jnp.where(low, o[:BLK], o[BLK:])
    lse_pair = jnp.where(low, jnp.broadcast_to(lse[:BLK], (BLK, LANES)),
                         jnp.broadcast_to(lse[BLK:], (BLK, LANES)))
    return o_pair, lse_pair


def _merge(o_a, lse_a, o_b, lse_b, want_lse=True):
    mx = jnp.maximum(lse_a, lse_b)
    ea = jnp.exp2(lse_a - mx)
    eb = jnp.exp2(lse_b - mx)
    den = ea + eb
    o = (ea * o_a + eb * o_b) * (1.0 / den)
    return (o, mx + jnp.log2(den)) if want_lse else (o, None)


_CLASSES = 4
_LOOP_UNROLL = 16


def _dilated_kernel(q_ref, k_ref, v_ref, bias_ref, o_ref,
                    q4, k4, v4, tmp, so4, sl4, il_o, il_l, *, seq):
    cls = seq // _CLASSES
    nb4 = cls // BLK
    nb16 = seq // 16 // BLK
    sh4, sh16 = nb4.bit_length() - 1, nb16.bit_length() - 1
    chunk = 512
    shc = (cls // chunk).bit_length() - 1

    for src, dst in ((q_ref, q4), (k_ref, k4), (v_ref, v4)):
        def widen(i, c, src=src):
            r0 = pl.multiple_of(i * chunk, chunk)
            tmp[pl.ds(r0, chunk), :] = src[0, pl.ds(r0, chunk), :].astype(F32)
            return c
        lax.fori_loop(0, seq // chunk, widen, 0)

        def regroup(i, c, dst=dst):
            r = lax.shift_right_logical(i, shc)
            j = i & (cls // chunk - 1)
            d0 = pl.multiple_of(r * cls + j * chunk, chunk)
            dst[pl.ds(d0, chunk), :] = tmp[pl.ds(r + _CLASSES * chunk * j, chunk,
                                                 stride=_CLASSES), :]
            return c
        lax.fori_loop(0, seq // chunk, regroup, 0)

    def dil4(idx, c):
        r = lax.shift_right_logical(idx, sh4)
        n = idx & (nb4 - 1)
        q0 = pl.multiple_of(r * cls + n * BLK, BLK)
        k0 = pl.multiple_of(r * cls + jnp.maximum(n - 1, 0) * BLK, BLK)
        o, lse = _band_block(q4[pl.ds(q0, BLK), :].astype(BF16),
                             k4[pl.ds(k0, 2 * BLK), :].astype(BF16),
                             v4[pl.ds(k0, 2 * BLK), :].astype(BF16),
                             bias_ref[jnp.minimum(n, 1)])
        so4[pl.ds(q0, BLK), :] = o
        sl4[pl.ds(q0, BLK), :] = lse
        return c
    lax.fori_loop(0, _CLASSES * nb4, dil4, 0, unroll=_LOOP_UNROLL)

    def dil16(idx, c):
        n = idx & (nb16 - 1)
        sub = lax.shift_right_logical(idx, sh16) & (_CLASSES - 1)
        r = lax.shift_right_logical(idx, sh16 + 2)
        q0 = r * cls + sub + _CLASSES * BLK * n
        k0 = r * cls + sub + _CLASSES * BLK * jnp.maximum(n - 1, 0)
        qs = pl.ds(q0, BLK, stride=_CLASSES)
        ks = pl.ds(k0, 2 * BLK, stride=_CLASSES)
        o_s, l_s = so4[qs, :], sl4[qs, :]
        o, lse = _band_block(q4[qs, :].astype(BF16), k4[ks, :].astype(BF16),
                             v4[ks, :].astype(BF16), bias_ref[jnp.minimum(n, 1)])
        o, lse = _merge(o_s, l_s, o, lse)
        so4[qs, :] = o
        sl4[qs, :] = lse
        return c
    lax.fori_loop(0, 16 * nb16, dil16, 0, unroll=_LOOP_UNROLL)

    piece = BLK // _CLASSES

    def dil1(n, c):
        q0 = pl.multiple_of(n * BLK, BLK)
        k0 = pl.multiple_of(jnp.maximum(n - 1, 0) * BLK, BLK)
        for r in range(_CLASSES):
            s0 = pl.multiple_of(r * cls + n * piece, piece)
            il_o[pl.ds(r, piece, stride=_CLASSES), :] = so4[pl.ds(s0, piece), :]
            il_l[pl.ds(r, piece, stride=_CLASSES), :] = sl4[pl.ds(s0, piece), :]
        o_s, l_s = il_o[...], il_l[...]
        o, lse = _band_block(q_ref[0, pl.ds(q0, BLK), :], k_ref[0, pl.ds(k0, 2 * BLK), :],
                             v_ref[0, pl.ds(k0, 2 * BLK), :], bias_ref[jnp.minimum(n, 1)])
        o, _ = _merge(o_s, l_s, o, lse, want_lse=False)
        o_ref[0, pl.ds(q0, BLK), :] = o.astype(o_ref.dtype)
        return c
    lax.fori_loop(0, seq // BLK, dil1, 0, unroll=_LOOP_UNROLL)


def _dilated_attention(q, k, v):
    assert DILATED_BRANCHES == ((128, 1), (512, 4), (2048, 16))
    batch, seq, width = q.shape
    assert seq % (16 * BLK) == 0 and (seq & (seq - 1)) == 0
    blk = pl.BlockSpec((1, seq, LANES), lambda b, c: (b, 0, c))
    rows = pltpu.VMEM((seq, LANES), F32)
    small = pltpu.VMEM((BLK, LANES), F32)
    return pl.pallas_call(
        functools.partial(_dilated_kernel, seq=seq),
        grid=(batch, width // LANES),
        in_specs=[blk, blk, blk, pl.BlockSpec((2, 2 * BLK, 2 * BLK), lambda b, c: (0, 0, 0))],
        out_specs=blk,
        out_shape=jax.ShapeDtypeStruct((batch, seq, width), BF16),
        scratch_shapes=[rows] * 6 + [small] * 2,
        compiler_params=_params(("arbitrary", "arbitrary")),
        name="dilated_attn",
    )(q, k, v, _band_bias(BLK))


def _swa_kernel(sink_ref, q_ref, k_ref, v_ref, bias_ref, o_ref, *, n_blocks):
    c = pl.program_id(1)
    row = lax.broadcasted_iota(jnp.int32, (2 * BLK, 1), 0)
    sink = jnp.where(row < BLK, sink_ref[2 * c], sink_ref[2 * c + 1]) * LOG2E

    def body(n, carry):
        q0 = pl.multiple_of(n * BLK, BLK)
        k0 = pl.multiple_of(jnp.maximum(n - 1, 0) * BLK, BLK)
        o, _ = _band_block(q_ref[0, pl.ds(q0, BLK), :], k_ref[0, pl.ds(k0, 2 * BLK), :],
                           v_ref[0, pl.ds(k0, 2 * BLK), :], bias_ref[jnp.minimum(n, 1)], sink)
        o_ref[0, pl.ds(q0, BLK), :] = o.astype(o_ref.dtype)
        return carry
    lax.fori_loop(0, n_blocks, body, 0, unroll=_LOOP_UNROLL)


def _swa_attention(q, k, v, sinks):
    batch, seq, width = q.shape
    qblk = pl.BlockSpec((1, seq, LANES), lambda b, c, s: (b, 0, c))
    kblk = pl.BlockSpec((1, seq, LANES), lambda b, c, s: (b, 0, c // 2))
    grid_spec = pltpu.PrefetchScalarGridSpec(
        num_scalar_prefetch=1, grid=(batch, width // LANES),
        in_specs=[qblk, kblk, kblk,
                  pl.BlockSpec((2, 2 * BLK, 2 * BLK), lambda b, c, s: (0, 0, 0))],
        out_specs=qblk)
    return pl.pallas_call(
        functools.partial(_swa_kernel, n_blocks=seq // BLK), grid_spec=grid_spec,
        out_shape=jax.ShapeDtypeStruct((batch, seq, width), BF16),
        compiler_params=_params(("arbitrary", "arbitrary")),
        name="swa_sink_attn",
    )(sinks, q, k, v, _band_bias(SWA_WINDOW - 1))


ROW_TM = 512
RUN_ALIGN = SUBLANES
SORT_ROWS = 2 * ROW_TM + N_EXPERTS * RUN_ALIGN
_R_G1, _R_G2, _R_POS1, _R_POS2 = range(4)
_T_COUNT, _T_OFF, _T_BEFORE = range(3)


def _route(logits, count_ref):
    tm = logits.shape[0]
    lane = lax.broadcasted_iota(jnp.int32, logits.shape, 1)
    v1 = jnp.max(logits, axis=1, keepdims=True)
    i1 = jnp.min(jnp.where(logits == v1, lane, LANES), axis=1, keepdims=True)
    rest = jnp.where(lane == i1, NEG_BIG, logits)
    v2 = jnp.max(rest, axis=1, keepdims=True)
    i2 = jnp.min(jnp.where(rest == v2, lane, LANES), axis=1, keepdims=True)
    e2 = jnp.exp(v2 - v1)
    g1 = 1.0 / (1.0 + e2)
    g2 = e2 / (1.0 + e2)
    sel = jnp.where((lane == i1) | (lane == i2), 1.0, 0.0)
    r = lax.broadcasted_iota(jnp.int32, (tm, tm), 0)
    c = lax.broadcasted_iota(jnp.int32, (tm, tm), 1)
    tri = jnp.where(c < r, 1.0, 0.0).astype(BF16)
    local = jnp.dot(tri, sel.astype(BF16), preferred_element_type=F32)
    count = jnp.sum(sel, axis=0, keepdims=True)
    padded = jnp.floor((count + (RUN_ALIGN - 1)) * (1.0 / RUN_ALIGN)) * RUN_ALIGN
    run_end = jnp.broadcast_to(padded, (SUBLANES, LANES))
    for shift in (1, 2, 4):
        run_end = run_end + pltpu.roll(run_end, shift, 1)
    offset = run_end[0:1] - padded
    pos = local + offset
    pos1 = jnp.sum(jnp.where(lane == i1, pos, 0.0), axis=1, keepdims=True)
    pos2 = jnp.sum(jnp.where(lane == i2, pos, 0.0), axis=1, keepdims=True)
    before = count_ref[...]
    count_ref[...] = before + padded
    rec = jnp.zeros(logits.shape, F32)
    for ln, col in ((_R_G1, g1), (_R_G2, g2), (_R_POS1, pos1), (_R_POS2, pos2)):
        rec = jnp.where(lane == ln, col, rec)
    row = lax.broadcasted_iota(jnp.int32, (SUBLANES, LANES), 0)
    tile = jnp.zeros((SUBLANES, LANES), F32)
    for rw, val in ((_T_COUNT, padded), (_T_OFF, offset), (_T_BEFORE, before)):
        tile = jnp.where(row == rw, val, tile)
    return rec, tile


def _outproj_kernel(*refs, with_router):
    oa_ref, ob_ref, x_ref, ga_ref, gb_ref, w_ref, gf_ref = refs[:7]
    rest = refs[7:]
    if with_router:
        r_ref, xo_ref, ho_ref, route_ref, tile_ref, count_ref = rest
    else:
        xo_ref, ho_ref = rest
    na = _rms(oa_ref[...].astype(F32), ga_ref[...]).astype(BF16)
    nb = _rms(ob_ref[...].astype(F32), gb_ref[...]).astype(BF16)
    xn = (x_ref[...] + jnp.dot(na, w_ref[:WIDTH_A, :], preferred_element_type=F32)
          + jnp.dot(nb, w_ref[WIDTH_A:, :], preferred_element_type=F32))
    xo_ref[...] = xn
    hf = _rms(xn, gf_ref[...])
    h_hi = hf.astype(BF16)
    ho_ref[...] = h_hi
    if with_router:
        @pl.when(pl.program_id(0) == 0)
        def _():
            count_ref[...] = jnp.zeros_like(count_ref)
        h_lo = (hf - h_hi.astype(F32)).astype(BF16)
        t = jnp.dot(h_hi, r_ref[...], preferred_element_type=F32)
        logits = (t[:, :LANES] + t[:, LANES:]
                  + jnp.dot(h_lo, r_ref[:, :LANES], preferred_element_type=F32))
        lane = lax.broadcasted_iota(jnp.int32, logits.shape, 1)
        logits = jnp.where(lane < N_EXPERTS, logits, NEG_BIG)
        route_ref[...], tile_ref[...] = _route(logits, count_ref)


def _outproj(oa, ob, x2, ga, gb, w_out, gf, router_pad=None):
    tokens = x2.shape[0]
    tm = ROW_TM
    with_router = router_pad is not None
    row = lambda w: pl.BlockSpec((tm, w), lambda i: (i, 0))
    const = lambda a, b: pl.BlockSpec((a, b), lambda i: (0, 0))
    in_specs = [row(WIDTH_A), row(WIDTH_B), row(D_MODEL), const(1, WIDTH_A), const(1, WIDTH_B),
                const(D_MODEL, D_MODEL), const(1, D_MODEL)]
    out_specs = [row(D_MODEL), row(D_MODEL)]
    out_shape = [jax.ShapeDtypeStruct((tokens, D_MODEL), F32),
                 jax.ShapeDtypeStruct((tokens, D_MODEL), BF16)]
    args = [oa, ob, x2, ga, gb, w_out, gf]
    if with_router:
        in_specs.append(const(D_MODEL, 2 * LANES))
        out_specs += [row(LANES), pl.BlockSpec((SUBLANES, LANES), lambda i: (i, 0)),
                      const(1, LANES)]
        out_shape += [jax.ShapeDtypeStruct((tokens, LANES), F32),
                      jax.ShapeDtypeStruct((tokens // tm * SUBLANES, LANES), F32),
                      jax.ShapeDtypeStruct((1, LANES), F32)]
        args.append(router_pad)
    return pl.pallas_call(
        functools.partial(_outproj_kernel, with_router=with_router),
        grid=(tokens // tm,), in_specs=in_specs, out_specs=out_specs, out_shape=out_shape,
        compiler_params=_params(("arbitrary",)),
        name="mix_outproj_router" if with_router else "mix_outproj",
    )(*args)


def _silu(x):
    return x * (1.0 / (1.0 + jnp.exp(-x)))


FFN_TM = 1024
FFN_TF = 1792
FFN_TC = 512


def _swiglu_chunk(h, wg_ref, wu_ref, wd_ref, acc_ref):
    for c0 in range(0, FFN_TF, FFN_TC):
        c1 = min(c0 + FFN_TC, FFN_TF)
        g = jnp.dot(h, wg_ref[:, c0:c1], preferred_element_type=F32)
        u = jnp.dot(h, wu_ref[:, c0:c1], preferred_element_type=F32)
        acc_ref[...] += jnp.dot((_silu(g) * u).astype(BF16), wd_ref[c0:c1, :],
                                preferred_element_type=F32)


def _ffn_kernel(*refs, final):
    h_ref, wg_ref, wu_ref, wd_ref, x_ref = refs[:5]
    if final:
        gfin_ref, o_ref, acc_ref = refs[5:]
    else:
        o_ref, acc_ref = refs[5:]
    f = pl.program_id(1)

    @pl.when(f == 0)
    def _():
        acc_ref[...] = x_ref[...]

    _swiglu_chunk(h_ref[...], wg_ref.at[0], wu_ref.at[0], wd_ref.at[0], acc_ref)

    @pl.when(f == pl.num_programs(1) - 1)
    def _():
        o_ref[...] = _rms(acc_ref[...], gfin_ref[...]) if final else acc_ref[...]


def _ffn(h, x2, wg, wu, wd, layer, final_gain=None):
    tokens = x2.shape[0]
    tm, tf = FFN_TM, FFN_TF
    final = final_gain is not None
    tok = lambda w: pl.BlockSpec((tm, w), lambda i, f: (i, 0))
    in_specs = [tok(D_MODEL),
                pl.BlockSpec((1, D_MODEL, tf), lambda i, f: (layer, 0, f)),
                pl.BlockSpec((1, D_MODEL, tf), lambda i, f: (layer, 0, f)),
                pl.BlockSpec((1, tf, D_MODEL), lambda i, f: (layer, f, 0)),
                tok(D_MODEL)]
    args = [h, wg, wu, wd, x2]
    if final:
        in_specs.append(pl.BlockSpec((1, D_MODEL), lambda i, f: (0, 0)))
        args.append(final_gain)
    return pl.pallas_call(
        functools.partial(_ffn_kernel, final=final),
        grid=(tokens // tm, D_FF // tf), in_specs=in_specs, out_specs=tok(D_MODEL),
        out_shape=jax.ShapeDtypeStruct((tokens, D_MODEL), F32),
        scratch_shapes=[pltpu.VMEM((tm, D_MODEL), F32)],
        compiler_params=_params(("arbitrary", "arbitrary")),
        name="dense_swiglu",
    )(*args)


def _run_copies(act, meta_ref, make_copy):
    for e in range(N_EXPERTS):
        n = pl.multiple_of(meta_ref[0, 0, N_EXPERTS + e], RUN_ALIGN)

        @pl.when(n > 0)
        def _(e=e, n=n):
            off = pl.multiple_of(meta_ref[0, 0, e], RUN_ALIGN)
            dst = pl.multiple_of(meta_ref[0, 0, 2 * N_EXPERTS + e], RUN_ALIGN)
            getattr(make_copy(off, n, dst), act)()


def _sort_matrix(route, n_rows):
    rec_t = route.T
    p1 = rec_t[_R_POS1:_R_POS1 + 1, :].astype(jnp.int32)
    p2 = rec_t[_R_POS2:_R_POS2 + 1, :].astype(jnp.int32)
    ip = lax.broadcasted_iota(jnp.int32, (n_rows, route.shape[0]), 0)
    return jnp.where((ip == p1) | (ip == p2), 1.0, 0.0).astype(BF16)


def _dispatch_kernel(tend_ref, meta_ref, prev_ref, h_ref, route_ref, out_ref, sort_ref, zero_ref,
                     sems, *, max_tail):
    n_tiles = out_ref.shape[0] // FFN_TM

    @pl.when(pl.program_id(0) == 0)
    def _():
        zero_ref[...] = jnp.zeros_like(zero_ref)

        def zero_tile(t):
            return pltpu.make_async_copy(zero_ref, out_ref.at[pl.ds(t * FFN_TM, FFN_TM)],
                                         sems.at[0])

        def has_tiles(e):
            return tend_ref[e] > (tend_ref[e - 1] if e else 0)

        for act in ("start", "wait"):
            for e in range(N_EXPERTS):
                @pl.when(has_tiles(e))
                def _(e=e, act=act):
                    getattr(zero_tile(tend_ref[e] - 1), act)()
            for k in range(max_tail):
                t = tend_ref[N_EXPERTS - 1] + k

                @pl.when(t < n_tiles)
                def _(t=t, act=act):
                    getattr(zero_tile(t), act)()

    i = pl.program_id(0)
    slot = i & 1

    def run_copy(s):
        return lambda off, n, dst: pltpu.make_async_copy(
            sort_ref.at[s, pl.ds(off, n)], out_ref.at[pl.ds(dst, n)], sems.at[s])

    sort_ref[slot] = jnp.dot(_sort_matrix(route_ref[...], SORT_ROWS), h_ref[...],
                             preferred_element_type=F32)
    _run_copies("start", meta_ref, run_copy(slot))

    @pl.when(i > 0)
    def _():
        _run_copies("wait", prev_ref, run_copy(1 - slot))

    @pl.when(i == pl.num_programs(0) - 1)
    def _():
        _run_copies("wait", meta_ref, run_copy(slot))


def _dispatch(h, route, meta, tile_end, n_slots):
    tokens = h.shape[0]
    tm = ROW_TM
    grid_spec = pltpu.PrefetchScalarGridSpec(
        num_scalar_prefetch=1, grid=(tokens // tm,),
        in_specs=[pl.BlockSpec((1, 1, 32), lambda i, te: (i, 0, 0), memory_space=pltpu.SMEM),
                  pl.BlockSpec((1, 1, 32), lambda i, te: (jnp.maximum(i - 1, 0), 0, 0),
                               memory_space=pltpu.SMEM),
                  pl.BlockSpec((tm, D_MODEL), lambda i, te: (i, 0)),
                  pl.BlockSpec((tm, LANES), lambda i, te: (i, 0))],
        out_specs=pl.BlockSpec(memory_space=pl.ANY),
        scratch_shapes=[pltpu.VMEM((2, SORT_ROWS, D_MODEL), F32),
                        pltpu.VMEM((FFN_TM, D_MODEL), F32), pltpu.SemaphoreType.DMA((2,))])
    max_tail = n_slots // FFN_TM - 2 * tokens // FFN_TM
    return pl.pallas_call(
        functools.partial(_dispatch_kernel, max_tail=max_tail), grid_spec=grid_spec,
        out_shape=jax.ShapeDtypeStruct((n_slots, D_MODEL), F32),
        compiler_params=_params(("arbitrary",)),
        name="moe_dispatch",
    )(tile_end, meta, meta, h, route)


def _grouped_kernel(texp_ref, nt_ref, h_ref, wg_ref, wu_ref, wd_ref, y_ref, hb_ref, acc_ref):
    del texp_ref
    i = pl.program_id(0)
    f = pl.program_id(1)

    @pl.when(i < nt_ref[0])
    def _():
        @pl.when(f == 0)
        def _():
            hb_ref[...] = h_ref[...].astype(BF16)
            acc_ref[...] = jnp.zeros_like(acc_ref)

        _swiglu_chunk(hb_ref[...], wg_ref.at[0, 0], wu_ref.at[0, 0], wd_ref.at[0, 0], acc_ref)

        @pl.when(f == pl.num_programs(1) - 1)
        def _():
            y_ref[...] = acc_ref[...]

    @pl.when((i >= nt_ref[0]) & (f == 0))
    def _():
        y_ref[...] = jnp.zeros_like(y_ref)


def _grouped_swiglu(hs, tile_expert, n_tiles_used, wg, wu, wd, layer):
    n_slots = hs.shape[0]
    tm, tf = FFN_TM, FFN_TF
    nf = D_FF // tf
    last = lambda i, nt: jnp.minimum(i, nt[0] - 1)
    fcol = lambda i, f, nt: jnp.where(i < nt[0], f, nf - 1)
    grid_spec = pltpu.PrefetchScalarGridSpec(
        num_scalar_prefetch=2, grid=(n_slots // tm, nf),
        in_specs=[pl.BlockSpec((tm, D_MODEL), lambda i, f, te, nt: (last(i, nt), 0)),
                  pl.BlockSpec((1, 1, D_MODEL, tf),
                               lambda i, f, te, nt: (layer, te[i], 0, fcol(i, f, nt))),
                  pl.BlockSpec((1, 1, D_MODEL, tf),
                               lambda i, f, te, nt: (layer, te[i], 0, fcol(i, f, nt))),
                  pl.BlockSpec((1, 1, tf, D_MODEL),
                               lambda i, f, te, nt: (layer, te[i], fcol(i, f, nt), 0))],
        out_specs=pl.BlockSpec((tm, D_MODEL), lambda i, f, te, nt: (i, 0)),
        scratch_shapes=[pltpu.VMEM((tm, D_MODEL), BF16), pltpu.VMEM((tm, D_MODEL), F32)])
    return pl.pallas_call(
        _grouped_kernel, grid_spec=grid_spec,
        out_shape=jax.ShapeDtypeStruct((n_slots, D_MODEL), F32),
        compiler_params=_params(("arbitrary", "arbitrary")),
        name="moe_grouped_swiglu",
    )(tile_expert, n_tiles_used, hs, wg, wu, wd)


def _combine_kernel(*refs, final):
    meta_ref, next_ref, route_ref, x_ref, y_ref = refs[:5]
    if final:
        gfin_ref, o_ref, sort_ref, sems = refs[5:]
    else:
        o_ref, sort_ref, sems = refs[5:]
    tm = x_ref.shape[0]
    i = pl.program_id(0)
    slot = i & 1

    def fetch(act, m_ref, s):
        if act == "start":
            sort_ref[s, pl.ds(2 * tm, SORT_ROWS - 2 * tm), :] = jnp.zeros(
                (SORT_ROWS - 2 * tm, D_MODEL), F32)
        _run_copies(act, m_ref, lambda off, n, dst: pltpu.make_async_copy(
            y_ref.at[pl.ds(dst, n)], sort_ref.at[s, pl.ds(off, n)], sems.at[s]))

    @pl.when(i == 0)
    def _():
        fetch("start", meta_ref, slot)

    @pl.when(i + 1 < pl.num_programs(0))
    def _():
        fetch("start", next_ref, 1 - slot)

    fetch("wait", meta_ref, slot)
    ys = sort_ref[slot].astype(BF16)
    route = route_ref[...]
    ip = lax.broadcasted_iota(jnp.int32, (tm, SORT_ROWS), 1)

    def pick(col):
        pos = route[:, col:col + 1].astype(jnp.int32)
        return jnp.dot(jnp.where(ip == pos, 1.0, 0.0).astype(BF16), ys,
                       preferred_element_type=F32)

    out = (x_ref[...] + route[:, _R_G1:_R_G1 + 1] * pick(_R_POS1)
           + route[:, _R_G2:_R_G2 + 1] * pick(_R_POS2))
    o_ref[...] = _rms(out, gfin_ref[...]) if final else out


def _combine(y, route, meta, x2, final_gain=None):
    tokens = x2.shape[0]
    tm = ROW_TM
    final = final_gain is not None
    n_steps = tokens // tm
    in_specs = [pl.BlockSpec((1, 1, 32), lambda i: (i, 0, 0), memory_space=pltpu.SMEM),
                pl.BlockSpec((1, 1, 32), lambda i: (jnp.minimum(i + 1, n_steps - 1), 0, 0),
                             memory_space=pltpu.SMEM),
                pl.BlockSpec((tm, LANES), lambda i: (i, 0)),
                pl.BlockSpec((tm, D_MODEL), lambda i: (i, 0)),
                pl.BlockSpec(memory_space=pl.ANY)]
    args = [meta, meta, route, x2, y]
    if final:
        in_specs.append(pl.BlockSpec((1, D_MODEL), lambda i: (0, 0)))
        args.append(final_gain)
    return pl.pallas_call(
        functools.partial(_combine_kernel, final=final),
        grid=(tokens // tm,), in_specs=in_specs,
        out_specs=pl.BlockSpec((tm, D_MODEL), lambda i: (i, 0)),
        out_shape=jax.ShapeDtypeStruct((tokens, D_MODEL), F32),
        scratch_shapes=[pltpu.VMEM((2, SORT_ROWS, D_MODEL), F32),
                        pltpu.SemaphoreType.DMA((2,))],
        compiler_params=_params(("arbitrary",)),
        name="moe_combine",
    )(*args)


def _routing_tables(tile_rec, counts, tokens):
    tm = FFN_TM
    n_tok_tiles = tokens // ROW_TM
    max_rows = 2 * tokens + n_tok_tiles * N_EXPERTS * (RUN_ALIGN - 1)
    n_tiles = (max_rows + N_EXPERTS * (tm - 1)) // tm
    cnt = counts[0, :N_EXPERTS].astype(jnp.int32)
    tiles = (cnt + tm - 1) // tm
    tile_end = jnp.cumsum(tiles)
    start = (tile_end - tiles) * tm
    rec = tile_rec.reshape(n_tok_tiles, SUBLANES, LANES)[:, :, :N_EXPERTS].astype(jnp.int32)
    meta = jnp.concatenate([rec[:, _T_OFF], rec[:, _T_COUNT], start[None, :] + rec[:, _T_BEFORE],
                            jnp.zeros((n_tok_tiles, 32 - 3 * N_EXPERTS), jnp.int32)], axis=1)
    tile_ids = jnp.arange(n_tiles, dtype=jnp.int32)[:, None]
    tile_expert = jnp.minimum(jnp.sum(tile_ids >= tile_end[None, :], axis=1),
                              N_EXPERTS - 1).astype(jnp.int32)
    return (meta.reshape(n_tok_tiles, 1, 32), tile_expert, tile_end.astype(jnp.int32),
            n_tiles * tm)


def _moe(h, route, tile_rec, counts, x2, wg, wu, wd, layer, final_gain=None):
    meta, tile_expert, tile_end, n_slots = _routing_tables(tile_rec, counts, x2.shape[0])
    hs = _dispatch(h, route, meta, tile_end, n_slots)
    y = _grouped_swiglu(hs, tile_expert, tile_end[-1:], wg, wu, wd, layer)
    return _combine(y, route, meta, x2, final_gain)


def _widen_w_in(w):
    a = 3 * WIDTH_A + WIDTH_B
    kb = w[:, a:a + KV_WIDTH_B]
    vb = w[:, a + KV_WIDTH_B:]
    dup = lambda t: jnp.concatenate(
        [t[:, h * HEAD_DIM:(h + 1) * HEAD_DIM] for h in range(N_KV_B) for _ in range(2)], axis=1)
    return jnp.concatenate([w[:, :a], dup(kb), dup(vb)], axis=1).astype(BF16)


def kernel(x, positions, attn_norm, w_in, mix_norm_a, mix_norm_b, sinks, w_out, ffn_norm,
           dense_w_gate, dense_w_up, dense_w_down, router, moe_w_gate, moe_w_up, moe_w_down,
           final_norm):
    batch, seq, _ = x.shape
    depth = w_in.shape[0]
    tokens = batch * seq
    cos, sina, sinb = _rope_tables(positions)
    x2 = x.reshape(tokens, D_MODEL)
    seq3 = lambda t: t.reshape(batch, seq, t.shape[-1])
    dense_w = [w.astype(BF16) for w in (dense_w_gate, dense_w_up, dense_w_down)]
    moe_w = [w.astype(BF16) for w in (moe_w_gate, moe_w_up, moe_w_down)]
    for i in range(depth):
        qa, ka, va, qb, kb, vb = _inproj(x2, attn_norm[i][None, :], _widen_w_in(w_in[i]),
                                         cos, sina, sinb, seq)
        oa = _dilated_attention(seq3(qa), seq3(ka), seq3(va)).reshape(tokens, WIDTH_A)
        ob = _swa_attention(seq3(qb), seq3(kb), seq3(vb), sinks[i]).reshape(tokens, WIDTH_B)
        j = i // 2
        last = i == depth - 1
        norms = (mix_norm_a[i][None, :], mix_norm_b[i][None, :], w_out[i].astype(BF16),
                 ffn_norm[i][None, :])
        if i % 2 == 0:
            x2, h = _outproj(oa, ob, x2, *norms)
            x2 = _ffn(h, x2, *dense_w, j, final_gain=final_norm[None, :] if last else None)
        else:
            r_f32 = jnp.pad(router[j], ((0, 0), (0, LANES - N_EXPERTS)))
            r_hi = r_f32.astype(BF16)
            r_lo = (r_f32 - r_hi.astype(F32)).astype(BF16)
            router_pad = jnp.concatenate([r_hi, r_lo], axis=1)
            x2, h, route, tile_rec, counts = _outproj(oa, ob, x2, *norms, router_pad)
            x2 = _moe(h, route, tile_rec, counts, x2, *moe_w, j,
                      final_gain=final_norm[None, :] if last else None)
    return x2.reshape(batch, seq, D_MODEL)
```

```python
import functools

import numpy as np
import jax
import jax.numpy as jnp
from jax import lax
from jax.experimental import pallas as pl
from jax.experimental.pallas import tpu as pltpu

D_MODEL = 1024
HEAD_DIM = 64
N_HEADS_A = 8
N_HEADS_B = 8
N_KV_B = 2
DILATED_BRANCHES = ((128, 1), (512, 4), (2048, 16))
SWA_WINDOW = 128
BLK = 128
ROPE_THETA = 10000.0
D_FF = 3584
N_EXPERTS = 8
RMS_EPS = 1e-5
WIDTH_A = N_HEADS_A * HEAD_DIM
WIDTH_B = N_HEADS_B * HEAD_DIM
KV_WIDTH_B = N_KV_B * HEAD_DIM

LANES = 128
SUBLANES = 8
NEG_BIG = -1e30
VMEM_LIMIT = 56 * 1024 * 1024

F32 = jnp.float32
BF16 = jnp.bfloat16


def _params(sem):
    return pltpu.CompilerParams(dimension_semantics=sem, vmem_limit_bytes=VMEM_LIMIT)


def _rms(x, g):
    return x * lax.rsqrt(jnp.mean(x * x, axis=-1, keepdims=True) + RMS_EPS) * g


def _rope_table_kernel(pos_ref, inv_ref, cos_ref, sina_ref, sinb_ref):
    ang = pos_ref[...].astype(F32) * inv_ref[...]
    c = jnp.cos(ang)
    s = jnp.sin(ang)
    lane = lax.broadcasted_iota(jnp.int32, ang.shape, 1)
    upper = (lane & (HEAD_DIM - 1)) >= HEAD_DIM // 2
    cos_ref[...] = c
    sina_ref[...] = jnp.where(upper, s, 0.0)
    sinb_ref[...] = jnp.where(upper, 0.0, -s)


def _rope_tables(positions):
    seq = positions.shape[0]
    inv = ROPE_THETA ** (-jnp.arange(0, HEAD_DIM, 2, dtype=F32) / HEAD_DIM)
    inv = jnp.tile(inv, LANES // (HEAD_DIM // 2))[None, :]
    rows = 1024
    tab = jax.ShapeDtypeStruct((seq, LANES), F32)
    return pl.pallas_call(
        _rope_table_kernel,
        grid=(seq // rows,),
        in_specs=[pl.BlockSpec((rows, 1), lambda i: (i, 0)),
                  pl.BlockSpec((1, LANES), lambda i: (0, 0))],
        out_specs=[pl.BlockSpec((rows, LANES), lambda i: (i, 0))] * 3,
        out_shape=[tab, tab, tab],
        compiler_params=_params(("arbitrary",)),
        name="rope_tables",
    )(positions.reshape(seq, 1), inv)


LOG2E = 1.4426950408889634
_QSCALE = HEAD_DIM ** -0.5 * LOG2E
_SEGS = (("qa", WIDTH_A, True, _QSCALE), ("ka", WIDTH_A, True, 1.0),
         ("va", WIDTH_A, False, 1.0), ("qb", WIDTH_B, True, _QSCALE),
         ("kb", 2 * KV_WIDTH_B, True, 1.0), ("vb", 2 * KV_WIDTH_B, False, 1.0))


def _inproj_kernel(x_ref, g_ref, w_ref, cos_ref, sina_ref, sinb_ref, *out_refs):
    h = _rms(x_ref[...], g_ref[...]).astype(BF16)
    cos = cos_ref[...]
    sina = sina_ref[...]
    sinb = sinb_ref[...]
    col = 0
    for (_, width, roped, scale), o_ref in zip(_SEGS, out_refs):
        t = jnp.dot(h, w_ref[:, col:col + width], preferred_element_type=F32)
        col += width
        for j in range(width // LANES):
            tj = t[:, j * LANES:(j + 1) * LANES]
            if roped:
                tj = (tj * cos + pltpu.roll(tj, HEAD_DIM // 2, 1) * sina
                      + pltpu.roll(tj, LANES - HEAD_DIM // 2, 1) * sinb)
                if scale != 1.0:
                    tj = tj * scale
            o_ref[:, j * LANES:(j + 1) * LANES] = tj.astype(o_ref.dtype)


def _inproj(x2, g, w_ext, cos, sina, sinb, seq):
    tokens = x2.shape[0]
    tm = 512
    spt = seq // tm
    widths = [s[1] for s in _SEGS]
    total = sum(widths)
    return pl.pallas_call(
        _inproj_kernel,
        grid=(tokens // tm,),
        in_specs=[pl.BlockSpec((tm, D_MODEL), lambda i: (i, 0)),
                  pl.BlockSpec((1, D_MODEL), lambda i: (0, 0)),
                  pl.BlockSpec((D_MODEL, total), lambda i: (0, 0)),
                  pl.BlockSpec((tm, LANES), lambda i: (i % spt, 0)),
                  pl.BlockSpec((tm, LANES), lambda i: (i % spt, 0)),
                  pl.BlockSpec((tm, LANES), lambda i: (i % spt, 0))],
        out_specs=[pl.BlockSpec((tm, w), lambda i: (i, 0)) for w in widths],
        out_shape=[jax.ShapeDtypeStruct((tokens, w), BF16) for w in widths],
        compiler_params=_params(("arbitrary",)),
        name="inproj_rope",
    )(x2, g, w_ext, cos, sina, sinb)


def _band_bias(max_dist):
    p = np.arange(BLK)[:, None]
    c = np.arange(2 * BLK)[None, :]
    first = c <= p
    dist = BLK + p - c
    main = (dist >= 0) & (dist <= max_dist)
    bias = np.where(np.stack([first, main]), 0.0, NEG_BIG).astype(np.float32)
    return jnp.asarray(np.concatenate([bias, bias], axis=1))


def _low_lanes():
    return lax.broadcasted_iota(jnp.int32, (BLK, LANES), 1) < HEAD_DIM


def _band_block(q, kb, vb, bias, sink=None):
    low = _low_lanes()
    zero = jnp.zeros_like(q)
    qq = jnp.concatenate([jnp.where(low, q, zero), jnp.where(low, zero, q)], axis=0)
    s = lax.dot_general(qq, kb, (((1,), (1,)), ((), ())), preferred_element_type=F32) + bias
    m = jnp.max(s, axis=1, keepdims=True)
    if sink is not None:
        m = jnp.maximum(m, sink)
    p = jnp.exp2(s - m).astype(BF16)
    ol = jnp.dot(p, jnp.concatenate([vb, jnp.ones_like(vb)], axis=1),
                 preferred_element_type=F32)
    acc = jnp.where(low, ol[:BLK, :LANES], ol[BLK:, :LANES])
    l = jnp.where(low, ol[:BLK, LANES:], ol[BLK:, LANES:])
    if sink is not None:
        ls = jnp.exp2(sink - m)
        l = l + jnp.where(low, ls[:BLK], ls[BLK:])
    return acc, jnp.where(low, m[:BLK], m[BLK:]), l


def _finish(acc, m, l):
    return acc * (1.0 / l), m + jnp.log2(l)


def _merge(o_s, lse_s, acc, m, l, want_lse=True):
    d = lse_s - m
    t = jnp.exp2(-jnp.abs(d))
    ws = jnp.where(d >= 0, 1.0, t)
    wb = jnp.where(d >= 0, t, 1.0)
    den = ws + wb * l
    o = (ws * o_s + wb * acc) * (1.0 / den)
    return (o, jnp.maximum(lse_s, m) + jnp.log2(den)) if want_lse else (o, None)


_CLASSES = 4
_LOOP_UNROLL = 16


def _dilated_kernel(q_ref, k_ref, v_ref, bias_ref, o_ref,
                    q4, k4, v4, tmp, so4, sl4, il_o, il_l, *, seq):
    cls = seq // _CLASSES
    nb4 = cls // BLK
    nb16 = seq // 16 // BLK
    sh4, sh16 = nb4.bit_length() - 1, nb16.bit_length() - 1
    chunk = 512
    shc = (cls // chunk).bit_length() - 1

    for src, dst in ((q_ref, q4), (k_ref, k4), (v_ref, v4)):
        def widen(i, c, src=src):
            r0 = pl.multiple_of(i * chunk, chunk)
            tmp[pl.ds(r0, chunk), :] = src[0, pl.ds(r0, chunk), :].astype(F32)
            return c
        lax.fori_loop(0, seq // chunk, widen, 0)

        def regroup(i, c, dst=dst):
            r = lax.shift_right_logical(i, shc)
            j = i & (cls // chunk - 1)
            d0 = pl.multiple_of(r * cls + j * chunk, chunk)
            dst[pl.ds(d0, chunk), :] = tmp[pl.ds(r + _CLASSES * chunk * j, chunk,
                                                 stride=_CLASSES), :]
            return c
        lax.fori_loop(0, seq // chunk, regroup, 0)

    def dil4(idx, c):
        r = lax.shift_right_logical(idx, sh4)
        n = idx & (nb4 - 1)
        q0 = pl.multiple_of(r * cls + n * BLK, BLK)
        k0 = pl.multiple_of(r * cls + jnp.maximum(n - 1, 0) * BLK, BLK)
        o, lse = _finish(*_band_block(q4[pl.ds(q0, BLK), :].astype(BF16),
                                      k4[pl.ds(k0, 2 * BLK), :].astype(BF16),
                                      v4[pl.ds(k0, 2 * BLK), :].astype(BF16),
                                      bias_ref[jnp.minimum(n, 1)]))
        so4[pl.ds(q0, BLK), :] = o
        sl4[pl.ds(q0, BLK), :] = lse
        return c
    lax.fori_loop(0, _CLASSES * nb4, dil4, 0, unroll=_LOOP_UNROLL)

    def dil16(idx, c):
        n = idx & (nb16 - 1)
        sub = lax.shift_right_logical(idx, sh16) & (_CLASSES - 1)
        r = lax.shift_right_logical(idx, sh16 + 2)
        q0 = r * cls + sub + _CLASSES * BLK * n
        k0 = r * cls + sub + _CLASSES * BLK * jnp.maximum(n - 1, 0)
        qs = pl.ds(q0, BLK, stride=_CLASSES)
        ks = pl.ds(k0, 2 * BLK, stride=_CLASSES)
        o_s, l_s = so4[qs, :], sl4[qs, :]
        o, lse = _merge(o_s, l_s, *_band_block(
            q4[qs, :].astype(BF16), k4[ks, :].astype(BF16), v4[ks, :].astype(BF16),
            bias_ref[jnp.minimum(n, 1)]))
        so4[qs, :] = o
        sl4[qs, :] = lse
        return c
    lax.fori_loop(0, 16 * nb16, dil16, 0, unroll=_LOOP_UNROLL)

    piece = BLK // _CLASSES

    def dil1(n, c):
        q0 = pl.multiple_of(n * BLK, BLK)
        k0 = pl.multiple_of(jnp.maximum(n - 1, 0) * BLK, BLK)
        for r in range(_CLASSES):
            s0 = pl.multiple_of(r * cls + n * piece, piece)
            il_o[pl.ds(r, piece, stride=_CLASSES), :] = so4[pl.ds(s0, piece), :]
            il_l[pl.ds(r, piece, stride=_CLASSES), :] = sl4[pl.ds(s0, piece), :]
        o_s, l_s = il_o[...], il_l[...]
        o, _ = _merge(o_s, l_s, *_band_block(
            q_ref[0, pl.ds(q0, BLK), :], k_ref[0, pl.ds(k0, 2 * BLK), :],
            v_ref[0, pl.ds(k0, 2 * BLK), :], bias_ref[jnp.minimum(n, 1)]), want_lse=False)
        o_ref[0, pl.ds(q0, BLK), :] = o.astype(o_ref.dtype)
        return c
    lax.fori_loop(0, seq // BLK, dil1, 0, unroll=_LOOP_UNROLL)


def _dilated_attention(q, k, v):
    assert DILATED_BRANCHES == ((128, 1), (512, 4), (2048, 16))
    batch, seq, width = q.shape
    assert seq % (16 * BLK) == 0 and (seq & (seq - 1)) == 0
    blk = pl.BlockSpec((1, seq, LANES), lambda b, c: (b, 0, c))
    rows = pltpu.VMEM((seq, LANES), F32)
    small = pltpu.VMEM((BLK, LANES), F32)
    return pl.pallas_call(
        functools.partial(_dilated_kernel, seq=seq),
        grid=(batch, width // LANES),
        in_specs=[blk, blk, blk, pl.BlockSpec((2, 2 * BLK, 2 * BLK), lambda b, c: (0, 0, 0))],
        out_specs=blk,
        out_shape=jax.ShapeDtypeStruct((batch, seq, width), BF16),
        scratch_shapes=[rows] * 6 + [small] * 2,
        compiler_params=_params(("arbitrary", "arbitrary")),
        name="dilated_attn",
    )(q, k, v, _band_bias(BLK))


def _swa_kernel(sink_ref, q_ref, k_ref, v_ref, bias_ref, o_ref, *, n_blocks):
    c = pl.program_id(1)
    row = lax.broadcasted_iota(jnp.int32, (2 * BLK, 1), 0)
    sink = jnp.where(row < BLK, sink_ref[2 * c], sink_ref[2 * c + 1]) * LOG2E

    def body(n, carry):
        q0 = pl.multiple_of(n * BLK, BLK)
        k0 = pl.multiple_of(jnp.maximum(n - 1, 0) * BLK, BLK)
        acc, _, l = _band_block(q_ref[0, pl.ds(q0, BLK), :], k_ref[0, pl.ds(k0, 2 * BLK), :],
                                v_ref[0, pl.ds(k0, 2 * BLK), :], bias_ref[jnp.minimum(n, 1)],
                                sink)
        o_ref[0, pl.ds(q0, BLK), :] = (acc * (1.0 / l)).astype(o_ref.dtype)
        return carry
    lax.fori_loop(0, n_blocks, body, 0, unroll=_LOOP_UNROLL)


def _swa_attention(q, k, v, sinks):
    batch, seq, width = q.shape
    qblk = pl.BlockSpec((1, seq, LANES), lambda b, c, s: (b, 0, c))
    kblk = pl.BlockSpec((1, seq, LANES), lambda b, c, s: (b, 0, c // 2))
    grid_spec = pltpu.PrefetchScalarGridSpec(
        num_scalar_prefetch=1, grid=(batch, width // LANES),
        in_specs=[qblk, kblk, kblk,
                  pl.BlockSpec((2, 2 * BLK, 2 * BLK), lambda b, c, s: (0, 0, 0))],
        out_specs=qblk)
    return pl.pallas_call(
        functools.partial(_swa_kernel, n_blocks=seq // BLK), grid_spec=grid_spec,
        out_shape=jax.ShapeDtypeStruct((batch, seq, width), BF16),
        compiler_params=_params(("arbitrary", "arbitrary")),
        name="swa_sink_attn",
    )(sinks, q, k, v, _band_bias(SWA_WINDOW - 1))


ROW_TM = 512
RUN_ALIGN = SUBLANES
SORT_ROWS = 2 * ROW_TM + N_EXPERTS * RUN_ALIGN
_R_G1, _R_G2, _R_POS1, _R_POS2 = range(4)
_T_COUNT, _T_OFF, _T_BEFORE = range(3)


def _route(logits, count_ref):
    tm = logits.shape[0]
    lane = lax.broadcasted_iota(jnp.int32, logits.shape, 1)
    v1 = jnp.max(logits, axis=1, keepdims=True)
    i1 = jnp.min(jnp.where(logits == v1, lane, LANES), axis=1, keepdims=True)
    rest = jnp.where(lane == i1, NEG_BIG, logits)
    v2 = jnp.max(rest, axis=1, keepdims=True)
    i2 = jnp.min(jnp.where(rest == v2, lane, LANES), axis=1, keepdims=True)
    e2 = jnp.exp(v2 - v1)
    g1 = 1.0 / (1.0 + e2)
    g2 = e2 / (1.0 + e2)
    sel = jnp.where((lane == i1) | (lane == i2), 1.0, 0.0)
    r = lax.broadcasted_iota(jnp.int32, (tm, tm), 0)
    c = lax.broadcasted_iota(jnp.int32, (tm, tm), 1)
    tri = jnp.where(c < r, 1.0, 0.0).astype(BF16)
    local = jnp.dot(tri, sel.astype(BF16), preferred_element_type=F32)
    count = jnp.sum(sel, axis=0, keepdims=True)
    padded = jnp.floor((count + (RUN_ALIGN - 1)) * (1.0 / RUN_ALIGN)) * RUN_ALIGN
    run_end = jnp.broadcast_to(padded, (SUBLANES, LANES))
    for shift in (1, 2, 4):
        run_end = run_end + pltpu.roll(run_end, shift, 1)
    offset = run_end[0:1] - padded
    pos = local + offset
    pos1 = jnp.sum(jnp.where(lane == i1, pos, 0.0), axis=1, keepdims=True)
    pos2 = jnp.sum(jnp.where(lane == i2, pos, 0.0), axis=1, keepdims=True)
    before = count_ref[...]
    count_ref[...] = before + padded
    rec = jnp.zeros(logits.shape, F32)
    for ln, col in ((_R_G1, g1), (_R_G2, g2), (_R_POS1, pos1), (_R_POS2, pos2)):
        rec = jnp.where(lane == ln, col, rec)
    row = lax.broadcasted_iota(jnp.int32, (SUBLANES, LANES), 0)
    tile = jnp.zeros((SUBLANES, LANES), F32)
    for rw, val in ((_T_COUNT, padded), (_T_OFF, offset), (_T_BEFORE, before)):
        tile = jnp.where(row == rw, val, tile)
    return rec, tile


def _outproj_kernel(*refs, with_router):
    oa_ref, ob_ref, x_ref, ga_ref, gb_ref, w_ref, gf_ref = refs[:7]
    rest = refs[7:]
    if with_router:
        r_ref, xo_ref, ho_ref, route_ref, tile_ref, count_ref = rest
    else:
        xo_ref, ho_ref = rest
    na = _rms(oa_ref[...].astype(F32), ga_ref[...]).astype(BF16)
    nb = _rms(ob_ref[...].astype(F32), gb_ref[...]).astype(BF16)
    xn = (x_ref[...] + jnp.dot(na, w_ref[:WIDTH_A, :], preferred_element_type=F32)
          + jnp.dot(nb, w_ref[WIDTH_A:, :], preferred_element_type=F32))
    xo_ref[...] = xn
    hf = _rms(xn, gf_ref[...])
    h_hi = hf.astype(BF16)
    ho_ref[...] = h_hi
    if with_router:
        @pl.when(pl.program_id(0) == 0)
        def _():
            count_ref[...] = jnp.zeros_like(count_ref)
        h_lo = (hf - h_hi.astype(F32)).astype(BF16)
        t = jnp.dot(h_hi, r_ref[...], preferred_element_type=F32)
        logits = (t[:, :LANES] + t[:, LANES:]
                  + jnp.dot(h_lo, r_ref[:, :LANES], preferred_element_type=F32))
        lane = lax.broadcasted_iota(jnp.int32, logits.shape, 1)
        logits = jnp.where(lane < N_EXPERTS, logits, NEG_BIG)
        route_ref[...], tile_ref[...] = _route(logits, count_ref)


def _outproj(oa, ob, x2, ga, gb, w_out, gf, router_pad=None):
    tokens = x2.shape[0]
    tm = ROW_TM
    with_router = router_pad is not None
    row = lambda w: pl.BlockSpec((tm, w), lambda i: (i, 0))
    const = lambda a, b: pl.BlockSpec((a, b), lambda i: (0, 0))
    in_specs = [row(WIDTH_A), row(WIDTH_B), row(D_MODEL), const(1, WIDTH_A), const(1, WIDTH_B),
                const(D_MODEL, D_MODEL), const(1, D_MODEL)]
    out_specs = [row(D_MODEL), row(D_MODEL)]
    out_shape = [jax.ShapeDtypeStruct((tokens, D_MODEL), F32),
                 jax.ShapeDtypeStruct((tokens, D_MODEL), BF16)]
    args = [oa, ob, x2, ga, gb, w_out, gf]
    if with_router:
        in_specs.append(const(D_MODEL, 2 * LANES))
        out_specs += [row(LANES), pl.BlockSpec((SUBLANES, LANES), lambda i: (i, 0)),
                      const(1, LANES)]
        out_shape += [jax.ShapeDtypeStruct((tokens, LANES), F32),
                      jax.ShapeDtypeStruct((tokens // tm * SUBLANES, LANES), F32),
                      jax.ShapeDtypeStruct((1, LANES), F32)]
        args.append(router_pad)
    return pl.pallas_call(
        functools.partial(_outproj_kernel, with_router=with_router),
        grid=(tokens // tm,), in_specs=in_specs, out_specs=out_specs, out_shape=out_shape,
        compiler_params=_params(("arbitrary",)),
        name="mix_outproj_router" if with_router else "mix_outproj",
    )(*args)


def _silu(x):
    return x * (1.0 / (1.0 + jnp.exp(-x)))


FFN_TM = 1024
FFN_TF = 1792
FFN_TC = 512


def _swiglu_chunk(h, wg_ref, wu_ref, wd_ref, acc_ref):
    for c0 in range(0, FFN_TF, FFN_TC):
        c1 = min(c0 + FFN_TC, FFN_TF)
        g = jnp.dot(h, wg_ref[:, c0:c1], preferred_element_type=F32)
        u = jnp.dot(h, wu_ref[:, c0:c1], preferred_element_type=F32)
        acc_ref[...] += jnp.dot((_silu(g) * u).astype(BF16), wd_ref[c0:c1, :],
                                preferred_element_type=F32)


def _ffn_kernel(*refs, final):
    h_ref, wg_ref, wu_ref, wd_ref, x_ref = refs[:5]
    if final:
        gfin_ref, o_ref, acc_ref = refs[5:]
    else:
        o_ref, acc_ref = refs[5:]
    f = pl.program_id(1)

    @pl.when(f == 0)
    def _():
        acc_ref[...] = x_ref[...]

    _swiglu_chunk(h_ref[...], wg_ref.at[0], wu_ref.at[0], wd_ref.at[0], acc_ref)

    @pl.when(f == pl.num_programs(1) - 1)
    def _():
        o_ref[...] = _rms(acc_ref[...], gfin_ref[...]) if final else acc_ref[...]


def _ffn(h, x2, wg, wu, wd, layer, final_gain=None):
    tokens = x2.shape[0]
    tm, tf = FFN_TM, FFN_TF
    final = final_gain is not None
    tok = lambda w: pl.BlockSpec((tm, w), lambda i, f: (i, 0))
    in_specs = [tok(D_MODEL),
                pl.BlockSpec((1, D_MODEL, tf), lambda i, f: (layer, 0, f)),
                pl.BlockSpec((1, D_MODEL, tf), lambda i, f: (layer, 0, f)),
                pl.BlockSpec((1, tf, D_MODEL), lambda i, f: (layer, f, 0)),
                tok(D_MODEL)]
    args = [h, wg, wu, wd, x2]
    if final:
        in_specs.append(pl.BlockSpec((1, D_MODEL), lambda i, f: (0, 0)))
        args.append(final_gain)
    return pl.pallas_call(
        functools.partial(_ffn_kernel, final=final),
        grid=(tokens // tm, D_FF // tf), in_specs=in_specs, out_specs=tok(D_MODEL),
        out_shape=jax.ShapeDtypeStruct((tokens, D_MODEL), F32),
        scratch_shapes=[pltpu.VMEM((tm, D_MODEL), F32)],
        compiler_params=_params(("arbitrary", "arbitrary")),
        name="dense_swiglu",
    )(*args)


def _run_copies(act, meta_ref, make_copy):
    for e in range(N_EXPERTS):
        n = pl.multiple_of(meta_ref[0, 0, N_EXPERTS + e], RUN_ALIGN)

        @pl.when(n > 0)
        def _(e=e, n=n):
            off = pl.multiple_of(meta_ref[0, 0, e], RUN_ALIGN)
            dst = pl.multiple_of(meta_ref[0, 0, 2 * N_EXPERTS + e], RUN_ALIGN)
            getattr(make_copy(off, n, dst), act)()


def _sort_matrix(route, n_rows):
    rec_t = route.T
    p1 = rec_t[_R_POS1:_R_POS1 + 1, :].astype(jnp.int32)
    p2 = rec_t[_R_POS2:_R_POS2 + 1, :].astype(jnp.int32)
    ip = lax.broadcasted_iota(jnp.int32, (n_rows, route.shape[0]), 0)
    return jnp.where((ip == p1) | (ip == p2), 1.0, 0.0).astype(BF16)


def _dispatch_kernel(tend_ref, meta_ref, prev_ref, h_ref, route_ref, out_ref, sort_ref, zero_ref,
                     sems, *, max_tail):
    n_tiles = out_ref.shape[0] // FFN_TM

    @pl.when(pl.program_id(0) == 0)
    def _():
        zero_ref[...] = jnp.zeros_like(zero_ref)

        def zero_tile(t):
            return pltpu.make_async_copy(zero_ref, out_ref.at[pl.ds(t * FFN_TM, FFN_TM)],
                                         sems.at[0])

        def has_tiles(e):
            return tend_ref[e] > (tend_ref[e - 1] if e else 0)

        for act in ("start", "wait"):
            for e in range(N_EXPERTS):
                @pl.when(has_tiles(e))
                def _(e=e, act=act):
                    getattr(zero_tile(tend_ref[e] - 1), act)()
            for k in range(max_tail):
                t = tend_ref[N_EXPERTS - 1] + k

                @pl.when(t < n_tiles)
                def _(t=t, act=act):
                    getattr(zero_tile(t), act)()

    i = pl.program_id(0)
    slot = i & 1

    def run_copy(s):
        return lambda off, n, dst: pltpu.make_async_copy(
            sort_ref.at[s, pl.ds(off, n)], out_ref.at[pl.ds(dst, n)], sems.at[s])

    sort_ref[slot] = jnp.dot(_sort_matrix(route_ref[...], SORT_ROWS), h_ref[...],
                             preferred_element_type=F32)
    _run_copies("start", meta_ref, run_copy(slot))

    @pl.when(i > 0)
    def _():
        _run_copies("wait", prev_ref, run_copy(1 - slot))

    @pl.when(i == pl.num_programs(0) - 1)
    def _():
        _run_copies("wait", meta_ref, run_copy(slot))


def _dispatch(h, route, meta, tile_end, n_slots):
    tokens = h.shape[0]
    tm = ROW_TM
    grid_spec = pltpu.PrefetchScalarGridSpec(
        num_scalar_prefetch=1, grid=(tokens // tm,),
        in_specs=[pl.BlockSpec((1, 1, 32), lambda i, te: (i, 0, 0), memory_space=pltpu.SMEM),
                  pl.BlockSpec((1, 1, 32), lambda i, te: (jnp.maximum(i - 1, 0), 0, 0),
                               memory_space=pltpu.SMEM),
                  pl.BlockSpec((tm, D_MODEL), lambda i, te: (i, 0)),
                  pl.BlockSpec((tm, LANES), lambda i, te: (i, 0))],
        out_specs=pl.BlockSpec(memory_space=pl.ANY),
        scratch_shapes=[pltpu.VMEM((2, SORT_ROWS, D_MODEL), F32),
                        pltpu.VMEM((FFN_TM, D_MODEL), F32), pltpu.SemaphoreType.DMA((2,))])
    max_tail = n_slots // FFN_TM - 2 * tokens // FFN_TM
    return pl.pallas_call(
        functools.partial(_dispatch_kernel, max_tail=max_tail), grid_spec=grid_spec,
        out_shape=jax.ShapeDtypeStruct((n_slots, D_MODEL), F32),
        compiler_params=_params(("arbitrary",)),
        name="moe_dispatch",
    )(tile_end, meta, meta, h, route)


def _grouped_kernel(texp_ref, nt_ref, h_ref, wg_ref, wu_ref, wd_ref, y_ref, hb_ref, acc_ref):
    del texp_ref
    i = pl.program_id(0)
    f = pl.program_id(1)

    @pl.when(i < nt_ref[0])
    def _():
        @pl.when(f == 0)
        def _():
            hb_ref[...] = h_ref[...].astype(BF16)
            acc_ref[...] = jnp.zeros_like(acc_ref)

        _swiglu_chunk(hb_ref[...], wg_ref.at[0, 0], wu_ref.at[0, 0], wd_ref.at[0, 0], acc_ref)

        @pl.when(f == pl.num_programs(1) - 1)
        def _():
            y_ref[...] = acc_ref[...]

    @pl.when((i >= nt_ref[0]) & (f == 0))
    def _():
        y_ref[...] = jnp.zeros_like(y_ref)


def _grouped_swiglu(hs, tile_expert, n_tiles_used, wg, wu, wd, layer):
    n_slots = hs.shape[0]
    tm, tf = FFN_TM, FFN_TF
    nf = D_FF // tf
    last = lambda i, nt: jnp.minimum(i, nt[0] - 1)
    fcol = lambda i, f, nt: jnp.where(i < nt[0], f, nf - 1)
    grid_spec = pltpu.PrefetchScalarGridSpec(
        num_scalar_prefetch=2, grid=(n_slots // tm, nf),
        in_specs=[pl.BlockSpec((tm, D_MODEL), lambda i, f, te, nt: (last(i, nt), 0)),
                  pl.BlockSpec((1, 1, D_MODEL, tf),
                               lambda i, f, te, nt: (layer, te[i], 0, fcol(i, f, nt))),
                  pl.BlockSpec((1, 1, D_MODEL, tf),
                               lambda i, f, te, nt: (layer, te[i], 0, fcol(i, f, nt))),
                  pl.BlockSpec((1, 1, tf, D_MODEL),
                               lambda i, f, te, nt: (layer, te[i], fcol(i, f, nt), 0))],
        out_specs=pl.BlockSpec((tm, D_MODEL), lambda i, f, te, nt: (i, 0)),
        scratch_shapes=[pltpu.VMEM((tm, D_MODEL), BF16), pltpu.VMEM((tm, D_MODEL), F32)])
    return pl.pallas_call(
        _grouped_kernel, grid_spec=grid_spec,
        out_shape=jax.ShapeDtypeStruct((n_slots, D_MODEL), F32),
        compiler_params=_params(("arbitrary", "arbitrary")),
        name="moe_grouped_swiglu",
    )(tile_expert, n_tiles_used, hs, wg, wu, wd)


def _combine_kernel(*refs, final):
    meta_ref, next_ref, route_ref, x_ref, y_ref = refs[:5]
    if final:
        gfin_ref, o_ref, sort_ref, sems = refs[5:]
    else:
        o_ref, sort_ref, sems = refs[5:]
    tm = x_ref.shape[0]
    i = pl.program_id(0)
    slot = i & 1

    def fetch(act, m_ref, s):
        if act == "start":
            sort_ref[s, pl.ds(2 * tm, SORT_ROWS - 2 * tm), :] = jnp.zeros(
                (SORT_ROWS - 2 * tm, D_MODEL), F32)
        _run_copies(act, m_ref, lambda off, n, dst: pltpu.make_async_copy(
            y_ref.at[pl.ds(dst, n)], sort_ref.at[s, pl.ds(off, n)], sems.at[s]))

    @pl.when(i == 0)
    def _():
        fetch("start", meta_ref, slot)

    @pl.when(i + 1 < pl.num_programs(0))
    def _():
        fetch("start", next_ref, 1 - slot)

    fetch("wait", meta_ref, slot)
    ys = sort_ref[slot].astype(BF16)
    route = route_ref[...]
    ip = lax.broadcasted_iota(jnp.int32, (tm, SORT_ROWS), 1)

    def pick(col):
        pos = route[:, col:col + 1].astype(jnp.int32)
        return jnp.dot(jnp.where(ip == pos, 1.0, 0.0).astype(BF16), ys,
                       preferred_element_type=F32)

    out = (x_ref[...] + route[:, _R_G1:_R_G1 + 1] * pick(_R_POS1)
           + route[:, _R_G2:_R_G2 + 1] * pick(_R_POS2))
    o_ref[...] = _rms(out, gfin_ref[...]) if final else out


def _combine(y, route, meta, x2, final_gain=None):
    tokens = x2.shape[0]
    tm = ROW_TM
    final = final_gain is not None
    n_steps = tokens // tm
    in_specs = [pl.BlockSpec((1, 1, 32), lambda i: (i, 0, 0), memory_space=pltpu.SMEM),
                pl.BlockSpec((1, 1, 32), lambda i: (jnp.minimum(i + 1, n_steps - 1), 0, 0),
                             memory_space=pltpu.SMEM),
                pl.BlockSpec((tm, LANES), lambda i: (i, 0)),
                pl.BlockSpec((tm, D_MODEL), lambda i: (i, 0)),
                pl.BlockSpec(memory_space=pl.ANY)]
    args = [meta, meta, route, x2, y]
    if final:
        in_specs.append(pl.BlockSpec((1, D_MODEL), lambda i: (0, 0)))
        args.append(final_gain)
    return pl.pallas_call(
        functools.partial(_combine_kernel, final=final),
        grid=(tokens // tm,), in_specs=in_specs,
        out_specs=pl.BlockSpec((tm, D_MODEL), lambda i: (i, 0)),
        out_shape=jax.ShapeDtypeStruct((tokens, D_MODEL), F32),
        scratch_shapes=[pltpu.VMEM((2, SORT_ROWS, D_MODEL), F32),
                        pltpu.SemaphoreType.DMA((2,))],
        compiler_params=_params(("arbitrary",)),
        name="moe_combine",
    )(*args)


def _routing_tables(tile_rec, counts, tokens):
    tm = FFN_TM
    n_tok_tiles = tokens // ROW_TM
    max_rows = 2 * tokens + n_tok_tiles * N_EXPERTS * (RUN_ALIGN - 1)
    n_tiles = (max_rows + N_EXPERTS * (tm - 1)) // tm
    cnt = counts[0, :N_EXPERTS].astype(jnp.int32)
    tiles = (cnt + tm - 1) // tm
    tile_end = jnp.cumsum(tiles)
    start = (tile_end - tiles) * tm
    rec = tile_rec.reshape(n_tok_tiles, SUBLANES, LANES)[:, :, :N_EXPERTS].astype(jnp.int32)
    meta = jnp.concatenate([rec[:, _T_OFF], rec[:, _T_COUNT], start[None, :] + rec[:, _T_BEFORE],
                            jnp.zeros((n_tok_tiles, 32 - 3 * N_EXPERTS), jnp.int32)], axis=1)
    tile_ids = jnp.arange(n_tiles, dtype=jnp.int32)[:, None]
    tile_expert = jnp.minimum(jnp.sum(tile_ids >= tile_end[None, :], axis=1),
                              N_EXPERTS - 1).astype(jnp.int32)
    return (meta.reshape(n_tok_tiles, 1, 32), tile_expert, tile_end.astype(jnp.int32),
            n_tiles * tm)


def _moe(h, route, tile_rec, counts, x2, wg, wu, wd, layer, final_gain=None):
    meta, tile_expert, tile_end, n_slots = _routing_tables(tile_rec, counts, x2.shape[0])
    hs = _dispatch(h, route, meta, tile_end, n_slots)
    y = _grouped_swiglu(hs, tile_expert, tile_end[-1:], wg, wu, wd, layer)
    return _combine(y, route, meta, x2, final_gain)


def _widen_w_in(w):
    a = 3 * WIDTH_A + WIDTH_B
    kb = w[:, a:a + KV_WIDTH_B]
    vb = w[:, a + KV_WIDTH_B:]
    dup = lambda t: jnp.concatenate(
        [t[:, h * HEAD_DIM:(h + 1) * HEAD_DIM] for h in range(N_KV_B) for _ in range(2)], axis=1)
    return jnp.concatenate([w[:, :a], dup(kb), dup(vb)], axis=1).astype(BF16)


def kernel(x, positions, attn_norm, w_in, mix_norm_a, mix_norm_b, sinks, w_out, ffn_norm,
           dense_w_gate, dense_w_up, dense_w_down, router, moe_w_gate, moe_w_up, moe_w_down,
           final_norm):
    batch, seq, _ = x.shape
    depth = w_in.shape[0]
    tokens = batch * seq
    cos, sina, sinb = _rope_tables(positions)
    x2 = x.reshape(tokens, D_MODEL)
    seq3 = lambda t: t.reshape(batch, seq, t.shape[-1])
    dense_w = [w.astype(BF16) for w in (dense_w_gate, dense_w_up, dense_w_down)]
    moe_w = [w.astype(BF16) for w in (moe_w_gate, moe_w_up, moe_w_down)]
    for i in range(depth):
        qa, ka, va, qb, kb, vb = _inproj(x2, attn_norm[i][None, :], _widen_w_in(w_in[i]),
                                         cos, sina, sinb, seq)
        oa = _dilated_attention(seq3(qa), seq3(ka), seq3(va)).reshape(tokens, WIDTH_A)
        ob = _swa_attention(seq3(qb), seq3(kb), seq3(vb), sinks[i]).reshape(tokens, WIDTH_B)
        j = i // 2
        last = i == depth - 1
        norms = (mix_norm_a[i][None, :], mix_norm_b[i][None, :], w_out[i].astype(BF16),
                 ffn_norm[i][None, :])
        if i % 2 == 0:
            x2, h = _outproj(oa, ob, x2, *norms)
            x2 = _ffn(h, x2, *dense_w, j, final_gain=final_norm[None, :] if last else None)
        else:
            r_f32 = jnp.pad(router[j], ((0, 0), (0, LANES - N_EXPERTS)))
            r_hi = r_f32.astype(BF16)
            r_lo = (r_f32 - r_hi.astype(F32)).astype(BF16)
            router_pad = jnp.concatenate([r_hi, r_lo], axis=1)
            x2, h, route, tile_rec, counts = _outproj(oa, ob, x2, *norms, router_pad)
            x2 = _moe(h, route, tile_rec, counts, x2, *moe_w, j,
                      final_gain=final_norm[None, :] if last else None)
    return x2.reshape(batch, seq, D_MODEL)
```

```python
import functools

import numpy as np
import jax
import jax.numpy as jnp
from jax import lax
from jax.experimental import pallas as pl
from jax.experimental.pallas import tpu as pltpu

D_MODEL = 1024
HEAD_DIM = 64
N_HEADS_A = 8
N_HEADS_B = 8
N_KV_B = 2
DILATED_BRANCHES = ((128, 1), (512, 4), (2048, 16))
SWA_WINDOW = 128
BLK = 128
ROPE_THETA = 10000.0
D_FF = 3584
N_EXPERTS = 8
RMS_EPS = 1e-5
WIDTH_A = N_HEADS_A * HEAD_DIM
WIDTH_B = N_HEADS_B * HEAD_DIM
KV_WIDTH_B = N_KV_B * HEAD_DIM

LANES = 128
SUBLANES = 8
NEG_BIG = -1e30
VMEM_LIMIT = 56 * 1024 * 1024

F32 = jnp.float32
BF16 = jnp.bfloat16


def _params(sem):
    return pltpu.CompilerParams(dimension_semantics=sem, vmem_limit_bytes=VMEM_LIMIT)


def _rms(x, g):
    return x * lax.rsqrt(jnp.mean(x * x, axis=-1, keepdims=True) + RMS_EPS) * g


def _rope_table_kernel(pos_ref, inv_ref, cos_ref, sina_ref, sinb_ref):
    ang = pos_ref[...].astype(F32) * inv_ref[...]
    c = jnp.cos(ang)
    s = jnp.sin(ang)
    lane = lax.broadcasted_iota(jnp.int32, ang.shape, 1)
    upper = (lane & (HEAD_DIM - 1)) >= HEAD_DIM // 2
    cos_ref[...] = c
    sina_ref[...] = jnp.where(upper, s, 0.0)
    sinb_ref[...] = jnp.where(upper, 0.0, -s)


def _rope_tables(positions):
    seq = positions.shape[0]
    inv = ROPE_THETA ** (-jnp.arange(0, HEAD_DIM, 2, dtype=F32) / HEAD_DIM)
    inv = jnp.tile(inv, LANES // (HEAD_DIM // 2))[None, :]
    rows = 1024
    tab = jax.ShapeDtypeStruct((seq, LANES), F32)
    return pl.pallas_call(
        _rope_table_kernel,
        grid=(seq // rows,),
        in_specs=[pl.BlockSpec((rows, 1), lambda i: (i, 0)),
                  pl.BlockSpec((1, LANES), lambda i: (0, 0))],
        out_specs=[pl.BlockSpec((rows, LANES), lambda i: (i, 0))] * 3,
        out_shape=[tab, tab, tab],
        compiler_params=_params(("arbitrary",)),
        name="rope_tables",
    )(positions.reshape(seq, 1), inv)


LOG2E = 1.4426950408889634
_QSCALE = HEAD_DIM ** -0.5 * LOG2E
_SEGS = (("qa", WIDTH_A, True, _QSCALE), ("ka", WIDTH_A, True, 1.0),
         ("va", WIDTH_A, False, 1.0), ("qb", WIDTH_B, True, _QSCALE),
         ("kb", 2 * KV_WIDTH_B, True, 1.0), ("vb", 2 * KV_WIDTH_B, False, 1.0))


def _cast_specs(w, layer, n_steps):
    _, n_exp, rows, cols = w.shape
    per = n_steps // n_exp
    rb = rows // per
    assert n_steps % n_exp == 0 and rows % per == 0 and rb % 16 == 0
    return (pl.BlockSpec((1, 1, rb, cols), lambda i: (layer, i // per, i % per, 0)),
            pl.BlockSpec((1, rb, cols), lambda i: (i // per, i % per, 0)),
            jax.ShapeDtypeStruct((n_exp, rows, cols), BF16))


def _inproj_kernel(x_ref, g_ref, w_ref, cos_ref, sina_ref, sinb_ref, cast_in_ref, *out_refs):
    out_refs[-1][0] = cast_in_ref[0, 0].astype(BF16)
    h = _rms(x_ref[...], g_ref[...]).astype(BF16)
    cos = cos_ref[...]
    sina = sina_ref[...]
    sinb = sinb_ref[...]
    col = 0
    for (_, width, roped, scale), o_ref in zip(_SEGS, out_refs):
        t = jnp.dot(h, w_ref[:, col:col + width], preferred_element_type=F32)
        col += width
        for j in range(width // LANES):
            tj = t[:, j * LANES:(j + 1) * LANES]
            if roped:
                tj = (tj * cos + pltpu.roll(tj, HEAD_DIM // 2, 1) * sina
                      + pltpu.roll(tj, LANES - HEAD_DIM // 2, 1) * sinb)
                if scale != 1.0:
                    tj = tj * scale
            o_ref[:, j * LANES:(j + 1) * LANES] = tj.astype(o_ref.dtype)


def _inproj(x2, g, w_ext, cos, sina, sinb, seq, cast_w, cast_layer):
    tokens = x2.shape[0]
    tm = 512
    spt = seq // tm
    widths = [s[1] for s in _SEGS]
    total = sum(widths)
    cast_in, cast_out, cast_shape = _cast_specs(cast_w, cast_layer, tokens // tm)
    return pl.pallas_call(
        _inproj_kernel,
        grid=(tokens // tm,),
        in_specs=[pl.BlockSpec((tm, D_MODEL), lambda i: (i, 0)),
                  pl.BlockSpec((1, D_MODEL), lambda i: (0, 0)),
                  pl.BlockSpec((D_MODEL, total), lambda i: (0, 0)),
                  pl.BlockSpec((tm, LANES), lambda i: (i % spt, 0)),
                  pl.BlockSpec((tm, LANES), lambda i: (i % spt, 0)),
                  pl.BlockSpec((tm, LANES), lambda i: (i % spt, 0)),
                  cast_in],
        out_specs=[pl.BlockSpec((tm, w), lambda i: (i, 0)) for w in widths] + [cast_out],
        out_shape=[jax.ShapeDtypeStruct((tokens, w), BF16) for w in widths] + [cast_shape],
        compiler_params=_params(("arbitrary",)),
        name="inproj_rope",
    )(x2, g, w_ext, cos, sina, sinb, cast_w)


def _band_bias(max_dist):
    p = np.arange(BLK)[:, None]
    c = np.arange(2 * BLK)[None, :]
    first = c <= p
    dist = BLK + p - c
    main = (dist >= 0) & (dist <= max_dist)
    bias = np.where(np.stack([first, main]), 0.0, NEG_BIG).astype(np.float32)
    return jnp.asarray(np.concatenate([bias, bias], axis=1))


def _low_lanes():
    return lax.broadcasted_iota(jnp.int32, (BLK, LANES), 1) < HEAD_DIM


def _band_block(q, kb, vb, bias, sink=None):
    low = _low_lanes()
    zero = jnp.zeros_like(q)
    qq = jnp.concatenate([jnp.where(low, q, zero), jnp.where(low, zero, q)], axis=0)
    s = lax.dot_general(qq, kb, (((1,), (1,)), ((), ())), preferred_element_type=F32) + bias
    m = jnp.max(s, axis=1, keepdims=True)
    if sink is not None:
        m = jnp.maximum(m, sink)
    p = jnp.exp2(s - m).astype(BF16)
    ol = jnp.dot(p, jnp.concatenate([vb, jnp.ones_like(vb)], axis=1),
                 preferred_element_type=F32)
    acc = jnp.where(low, ol[:BLK, :LANES], ol[BLK:, :LANES])
    l = jnp.where(low, ol[:BLK, LANES:], ol[BLK:, LANES:])
    if sink is not None:
        ls = jnp.exp2(sink - m)
        l = l + jnp.where(low, ls[:BLK], ls[BLK:])
    return acc, jnp.where(low, m[:BLK], m[BLK:]), l


def _finish(acc, m, l):
    return acc * (1.0 / l), m + jnp.log2(l)


def _merge(o_s, lse_s, acc, m, l, want_lse=True):
    d = lse_s - m
    t = jnp.exp2(-jnp.abs(d))
    ws = jnp.where(d >= 0, 1.0, t)
    wb = jnp.where(d >= 0, t, 1.0)
    den = ws + wb * l
    o = (ws * o_s + wb * acc) * (1.0 / den)
    return (o, jnp.maximum(lse_s, m) + jnp.log2(den)) if want_lse else (o, None)


_CLASSES = 4
_LOOP_UNROLL = 32


def _dilated_kernel(q_ref, k_ref, v_ref, bias_ref, o_ref,
                    q4, k4, v4, tmp, so4, sl4, il_o, il_l, *, seq):
    cls = seq // _CLASSES
    nb4 = cls // BLK
    nb16 = seq // 16 // BLK
    sh4, sh16 = nb4.bit_length() - 1, nb16.bit_length() - 1
    chunk = 512
    shc = (cls // chunk).bit_length() - 1

    for src, dst in ((q_ref, q4), (k_ref, k4), (v_ref, v4)):
        def widen(i, c, src=src):
            r0 = pl.multiple_of(i * chunk, chunk)
            tmp[pl.ds(r0, chunk), :] = src[0, pl.ds(r0, chunk), :].astype(F32)
            return c
        lax.fori_loop(0, seq // chunk, widen, 0)

        def regroup(i, c, dst=dst):
            r = lax.shift_right_logical(i, shc)
            j = i & (cls // chunk - 1)
            d0 = pl.multiple_of(r * cls + j * chunk, chunk)
            dst[pl.ds(d0, chunk), :] = tmp[pl.ds(r + _CLASSES * chunk * j, chunk,
                                                 stride=_CLASSES), :]
            return c
        lax.fori_loop(0, seq // chunk, regroup, 0)

    def dil4(idx, c):
        r = lax.shift_right_logical(idx, sh4)
        n = idx & (nb4 - 1)
        q0 = pl.multiple_of(r * cls + n * BLK, BLK)
        k0 = pl.multiple_of(r * cls + jnp.maximum(n - 1, 0) * BLK, BLK)
        o, lse = _finish(*_band_block(q4[pl.ds(q0, BLK), :].astype(BF16),
                                      k4[pl.ds(k0, 2 * BLK), :].astype(BF16),
                                      v4[pl.ds(k0, 2 * BLK), :].astype(BF16),
                                      bias_ref[jnp.minimum(n, 1)]))
        so4[pl.ds(q0, BLK), :] = o
        sl4[pl.ds(q0, BLK), :] = lse
        return c
    lax.fori_loop(0, _CLASSES * nb4, dil4, 0, unroll=_LOOP_UNROLL)

    def dil16(idx, c):
        n = idx & (nb16 - 1)
        sub = lax.shift_right_logical(idx, sh16) & (_CLASSES - 1)
        r = lax.shift_right_logical(idx, sh16 + 2)
        q0 = r * cls + sub + _CLASSES * BLK * n
        k0 = r * cls + sub + _CLASSES * BLK * jnp.maximum(n - 1, 0)
        qs = pl.ds(q0, BLK, stride=_CLASSES)
        ks = pl.ds(k0, 2 * BLK, stride=_CLASSES)
        o_s, l_s = so4[qs, :], sl4[qs, :]
        o, lse = _merge(o_s, l_s, *_band_block(
            q4[qs, :].astype(BF16), k4[ks, :].astype(BF16), v4[ks, :].astype(BF16),
            bias_ref[jnp.minimum(n, 1)]))
        so4[qs, :] = o
        sl4[qs, :] = lse
        return c
    lax.fori_loop(0, 16 * nb16, dil16, 0, unroll=_LOOP_UNROLL)

    piece = BLK // _CLASSES

    def dil1(n, c):
        q0 = pl.multiple_of(n * BLK, BLK)
        k0 = pl.multiple_of(jnp.maximum(n - 1, 0) * BLK, BLK)
        for r in range(_CLASSES):
            s0 = pl.multiple_of(r * cls + n * piece, piece)
            il_o[pl.ds(r, piece, stride=_CLASSES), :] = so4[pl.ds(s0, piece), :]
            il_l[pl.ds(r, piece, stride=_CLASSES), :] = sl4[pl.ds(s0, piece), :]
        o_s, l_s = il_o[...], il_l[...]
        o, _ = _merge(o_s, l_s, *_band_block(
            q_ref[0, pl.ds(q0, BLK), :], k_ref[0, pl.ds(k0, 2 * BLK), :],
            v_ref[0, pl.ds(k0, 2 * BLK), :], bias_ref[jnp.minimum(n, 1)]), want_lse=False)
        o_ref[0, pl.ds(q0, BLK), :] = o.astype(o_ref.dtype)
        return c
    lax.fori_loop(0, seq // BLK, dil1, 0, unroll=_LOOP_UNROLL)


def _dilated_attention(q, k, v):
    assert DILATED_BRANCHES == ((128, 1), (512, 4), (2048, 16))
    batch, seq, width = q.shape
    assert seq % (16 * BLK) == 0 and (seq & (seq - 1)) == 0
    blk = pl.BlockSpec((1, seq, LANES), lambda b, c: (b, 0, c))
    rows = pltpu.VMEM((seq, LANES), F32)
    small = pltpu.VMEM((BLK, LANES), F32)
    return pl.pallas_call(
        functools.partial(_dilated_kernel, seq=seq),
        grid=(batch, width // LANES),
        in_specs=[blk, blk, blk, pl.BlockSpec((2, 2 * BLK, 2 * BLK), lambda b, c: (0, 0, 0))],
        out_specs=blk,
        out_shape=jax.ShapeDtypeStruct((batch, seq, width), BF16),
        scratch_shapes=[rows] * 6 + [small] * 2,
        compiler_params=_params(("arbitrary", "arbitrary")),
        name="dilated_attn",
    )(q, k, v, _band_bias(BLK))


def _swa_kernel(sink_ref, q_ref, k_ref, v_ref, bias_ref, o_ref, *, n_blocks):
    c = pl.program_id(1)
    row = lax.broadcasted_iota(jnp.int32, (2 * BLK, 1), 0)
    sink = jnp.where(row < BLK, sink_ref[2 * c], sink_ref[2 * c + 1]) * LOG2E

    def body(n, carry):
        q0 = pl.multiple_of(n * BLK, BLK)
        k0 = pl.multiple_of(jnp.maximum(n - 1, 0) * BLK, BLK)
        acc, _, l = _band_block(q_ref[0, pl.ds(q0, BLK), :], k_ref[0, pl.ds(k0, 2 * BLK), :],
                                v_ref[0, pl.ds(k0, 2 * BLK), :], bias_ref[jnp.minimum(n, 1)],
                                sink)
        o_ref[0, pl.ds(q0, BLK), :] = (acc * (1.0 / l)).astype(o_ref.dtype)
        return carry
    lax.fori_loop(0, n_blocks, body, 0, unroll=_LOOP_UNROLL)


def _swa_attention(q, k, v, sinks):
    batch, seq, width = q.shape
    qblk = pl.BlockSpec((1, seq, LANES), lambda b, c, s: (b, 0, c))
    kblk = pl.BlockSpec((1, seq, LANES), lambda b, c, s: (b, 0, c // 2))
    grid_spec = pltpu.PrefetchScalarGridSpec(
        num_scalar_prefetch=1, grid=(batch, width // LANES),
        in_specs=[qblk, kblk, kblk,
                  pl.BlockSpec((2, 2 * BLK, 2 * BLK), lambda b, c, s: (0, 0, 0))],
        out_specs=qblk)
    return pl.pallas_call(
        functools.partial(_swa_kernel, n_blocks=seq // BLK), grid_spec=grid_spec,
        out_shape=jax.ShapeDtypeStruct((batch, seq, width), BF16),
        compiler_params=_params(("arbitrary", "arbitrary")),
        name="swa_sink_attn",
    )(sinks, q, k, v, _band_bias(SWA_WINDOW - 1))


ROW_TM = 512
RUN_ALIGN = SUBLANES
SORT_ROWS = 2 * ROW_TM + N_EXPERTS * RUN_ALIGN
_R_G1, _R_G2, _R_POS1, _R_POS2 = range(4)
_T_COUNT, _T_OFF, _T_BEFORE = range(3)


def _route(logits, count_ref):
    tm = logits.shape[0]
    lane = lax.broadcasted_iota(jnp.int32, logits.shape, 1)
    v1 = jnp.max(logits, axis=1, keepdims=True)
    i1 = jnp.min(jnp.where(logits == v1, lane, LANES), axis=1, keepdims=True)
    rest = jnp.where(lane == i1, NEG_BIG, logits)
    v2 = jnp.max(rest, axis=1, keepdims=True)
    i2 = jnp.min(jnp.where(rest == v2, lane, LANES), axis=1, keepdims=True)
    e2 = jnp.exp(v2 - v1)
    g1 = 1.0 / (1.0 + e2)
    g2 = e2 / (1.0 + e2)
    sel = jnp.where((lane == i1) | (lane == i2), 1.0, 0.0)
    r = lax.broadcasted_iota(jnp.int32, (tm, tm), 0)
    c = lax.broadcasted_iota(jnp.int32, (tm, tm), 1)
    tri = jnp.where(c < r, 1.0, 0.0).astype(BF16)
    local = jnp.dot(tri, sel.astype(BF16), preferred_element_type=F32)
    count = jnp.sum(sel, axis=0, keepdims=True)
    padded = jnp.floor((count + (RUN_ALIGN - 1)) * (1.0 / RUN_ALIGN)) * RUN_ALIGN
    run_end = jnp.broadcast_to(padded, (SUBLANES, LANES))
    for shift in (1, 2, 4):
        run_end = run_end + pltpu.roll(run_end, shift, 1)
    offset = run_end[0:1] - padded
    pos = local + offset
    pos1 = jnp.sum(jnp.where(lane == i1, pos, 0.0), axis=1, keepdims=True)
    pos2 = jnp.sum(jnp.where(lane == i2, pos, 0.0), axis=1, keepdims=True)
    before = count_ref[...]
    count_ref[...] = before + padded
    rec = jnp.zeros(logits.shape, F32)
    for ln, col in ((_R_G1, g1), (_R_G2, g2), (_R_POS1, pos1), (_R_POS2, pos2)):
        rec = jnp.where(lane == ln, col, rec)
    row = lax.broadcasted_iota(jnp.int32, (SUBLANES, LANES), 0)
    tile = jnp.zeros((SUBLANES, LANES), F32)
    for rw, val in ((_T_COUNT, padded), (_T_OFF, offset), (_T_BEFORE, before)):
        tile = jnp.where(row == rw, val, tile)
    return rec, tile


def _outproj_kernel(*refs, with_router):
    oa_ref, ob_ref, x_ref, ga_ref, gb_ref, w_ref, gf_ref = refs[:7]
    rest = refs[7:]
    if with_router:
        r_ref, xo_ref, ho_ref, route_ref, tile_ref, count_ref = rest
    else:
        cast_in_ref, xo_ref, ho_ref, cast_out_ref = rest
        cast_out_ref[0] = cast_in_ref[0, 0].astype(BF16)
    na = _rms(oa_ref[...].astype(F32), ga_ref[...]).astype(BF16)
    nb = _rms(ob_ref[...].astype(F32), gb_ref[...]).astype(BF16)
    xn = (x_ref[...] + jnp.dot(na, w_ref[:WIDTH_A, :], preferred_element_type=F32)
          + jnp.dot(nb, w_ref[WIDTH_A:, :], preferred_element_type=F32))
    xo_ref[...] = xn
    hf = _rms(xn, gf_ref[...])
    h_hi = hf.astype(BF16)
    ho_ref[...] = h_hi
    if with_router:
        @pl.when(pl.program_id(0) == 0)
        def _():
            count_ref[...] = jnp.zeros_like(count_ref)
        h_lo = (hf - h_hi.astype(F32)).astype(BF16)
        t = jnp.dot(h_hi, r_ref[...], preferred_element_type=F32)
        logits = (t[:, :LANES] + t[:, LANES:]
                  + jnp.dot(h_lo, r_ref[:, :LANES], preferred_element_type=F32))
        lane = lax.broadcasted_iota(jnp.int32, logits.shape, 1)
        logits = jnp.where(lane < N_EXPERTS, logits, NEG_BIG)
        route_ref[...], tile_ref[...] = _route(logits, count_ref)


def _outproj(oa, ob, x2, ga, gb, w_out, gf, router_pad=None, cast=None):
    tokens = x2.shape[0]
    tm = ROW_TM
    with_router = router_pad is not None
    assert with_router != (cast is not None)
    row = lambda w: pl.BlockSpec((tm, w), lambda i: (i, 0))
    const = lambda a, b: pl.BlockSpec((a, b), lambda i: (0, 0))
    in_specs = [row(WIDTH_A), row(WIDTH_B), row(D_MODEL), const(1, WIDTH_A), const(1, WIDTH_B),
                const(D_MODEL, D_MODEL), const(1, D_MODEL)]
    out_specs = [row(D_MODEL), row(D_MODEL)]
    out_shape = [jax.ShapeDtypeStruct((tokens, D_MODEL), F32),
                 jax.ShapeDtypeStruct((tokens, D_MODEL), BF16)]
    args = [oa, ob, x2, ga, gb, w_out, gf]
    if with_router:
        in_specs.append(const(D_MODEL, 2 * LANES))
        out_specs += [row(LANES), pl.BlockSpec((SUBLANES, LANES), lambda i: (i, 0)),
                      const(1, LANES)]
        out_shape += [jax.ShapeDtypeStruct((tokens, LANES), F32),
                      jax.ShapeDtypeStruct((tokens // tm * SUBLANES, LANES), F32),
                      jax.ShapeDtypeStruct((1, LANES), F32)]
        args.append(router_pad)
    else:
        cast_in, cast_out, cast_shape = _cast_specs(*cast, tokens // tm)
        in_specs.append(cast_in)
        out_specs.append(cast_out)
        out_shape.append(cast_shape)
        args.append(cast[0])
    return pl.pallas_call(
        functools.partial(_outproj_kernel, with_router=with_router),
        grid=(tokens // tm,), in_specs=in_specs, out_specs=out_specs, out_shape=out_shape,
        compiler_params=_params(("arbitrary",)),
        name="mix_outproj_router" if with_router else "mix_outproj",
    )(*args)


def _silu(x):
    return x * (1.0 / (1.0 + jnp.exp(-x)))


FFN_TM = 1024
FFN_TF = 1792
FFN_TC = 512


def _swiglu_chunk(h, wg_ref, wu_ref, wd_ref, acc_ref):
    for c0 in range(0, FFN_TF, FFN_TC):
        c1 = min(c0 + FFN_TC, FFN_TF)
        g = jnp.dot(h, wg_ref[:, c0:c1], preferred_element_type=F32)
        u = jnp.dot(h, wu_ref[:, c0:c1], preferred_element_type=F32)
        acc_ref[...] += jnp.dot((_silu(g) * u).astype(BF16), wd_ref[c0:c1, :],
                                preferred_element_type=F32)


def _ffn_kernel(*refs, final):
    h_ref, wg_ref, wu_ref, wd_ref, x_ref = refs[:5]
    if final:
        gfin_ref, o_ref, acc_ref = refs[5:]
    else:
        o_ref, acc_ref = refs[5:]
    f = pl.program_id(1)

    @pl.when(f == 0)
    def _():
        acc_ref[...] = x_ref[...]

    _swiglu_chunk(h_ref[...], wg_ref.at[0], wu_ref.at[0], wd_ref.at[0], acc_ref)

    @pl.when(f == pl.num_programs(1) - 1)
    def _():
        o_ref[...] = _rms(acc_ref[...], gfin_ref[...]) if final else acc_ref[...]


def _ffn(h, x2, wg, wu, wd, layer, final_gain=None):
    tokens = x2.shape[0]
    tm, tf = FFN_TM, FFN_TF
    final = final_gain is not None
    tok = lambda w: pl.BlockSpec((tm, w), lambda i, f: (i, 0))
    in_specs = [tok(D_MODEL),
                pl.BlockSpec((1, D_MODEL, tf), lambda i, f: (layer, 0, f)),
                pl.BlockSpec((1, D_MODEL, tf), lambda i, f: (layer, 0, f)),
                pl.BlockSpec((1, tf, D_MODEL), lambda i, f: (layer, f, 0)),
                tok(D_MODEL)]
    args = [h, wg, wu, wd, x2]
    if final:
        in_specs.append(pl.BlockSpec((1, D_MODEL), lambda i, f: (0, 0)))
        args.append(final_gain)
    return pl.pallas_call(
        functools.partial(_ffn_kernel, final=final),
        grid=(tokens // tm, D_FF // tf), in_specs=in_specs, out_specs=tok(D_MODEL),
        out_shape=jax.ShapeDtypeStruct((tokens, D_MODEL), F32),
        scratch_shapes=[pltpu.VMEM((tm, D_MODEL), F32)],
        compiler_params=_params(("arbitrary", "arbitrary")),
        name="dense_swiglu",
    )(*args)


def _run_copies(act, meta_ref, make_copy):
    for e in range(N_EXPERTS):
        n = pl.multiple_of(meta_ref[0, 0, N_EXPERTS + e], RUN_ALIGN)

        @pl.when(n > 0)
        def _(e=e, n=n):
            off = pl.multiple_of(meta_ref[0, 0, e], RUN_ALIGN)
            dst = pl.multiple_of(meta_ref[0, 0, 2 * N_EXPERTS + e], RUN_ALIGN)
            getattr(make_copy(off, n, dst), act)()


def _sort_matrix(route, n_rows):
    rec_t = route.T
    p1 = rec_t[_R_POS1:_R_POS1 + 1, :].astype(jnp.int32)
    p2 = rec_t[_R_POS2:_R_POS2 + 1, :].astype(jnp.int32)
    ip = lax.broadcasted_iota(jnp.int32, (n_rows, route.shape[0]), 0)
    return jnp.where((ip == p1) | (ip == p2), 1.0, 0.0).astype(BF16)


def _dispatch_kernel(tend_ref, meta_ref, prev_ref, h_ref, route_ref, out_ref, sort_ref, zero_ref,
                     sems, *, max_tail):
    n_tiles = out_ref.shape[0] // FFN_TM

    @pl.when(pl.program_id(0) == 0)
    def _():
        zero_ref[...] = jnp.zeros_like(zero_ref)

        def zero_tile(t):
            return pltpu.make_async_copy(zero_ref, out_ref.at[pl.ds(t * FFN_TM, FFN_TM)],
                                         sems.at[0])

        def has_tiles(e):
            return tend_ref[e] > (tend_ref[e - 1] if e else 0)

        for act in ("start", "wait"):
            for e in range(N_EXPERTS):
                @pl.when(has_tiles(e))
                def _(e=e, act=act):
                    getattr(zero_tile(tend_ref[e] - 1), act)()
            for k in range(max_tail):
                t = tend_ref[N_EXPERTS - 1] + k

                @pl.when(t < n_tiles)
                def _(t=t, act=act):
                    getattr(zero_tile(t), act)()

    i = pl.program_id(0)
    slot = i & 1

    def run_copy(s):
        return lambda off, n, dst: pltpu.make_async_copy(
            sort_ref.at[s, pl.ds(off, n)], out_ref.at[pl.ds(dst, n)], sems.at[s])

    sort_ref[slot] = jnp.dot(_sort_matrix(route_ref[...], SORT_ROWS), h_ref[...],
                             preferred_element_type=F32)
    _run_copies("start", meta_ref, run_copy(slot))

    @pl.when(i > 0)
    def _():
        _run_copies("wait", prev_ref, run_copy(1 - slot))

    @pl.when(i == pl.num_programs(0) - 1)
    def _():
        _run_copies("wait", meta_ref, run_copy(slot))


def _dispatch(h, route, meta, tile_end, n_slots):
    tokens = h.shape[0]
    tm = ROW_TM
    grid_spec = pltpu.PrefetchScalarGridSpec(
        num_scalar_prefetch=1, grid=(tokens // tm,),
        in_specs=[pl.BlockSpec((1, 1, 32), lambda i, te: (i, 0, 0), memory_space=pltpu.SMEM),
                  pl.BlockSpec((1, 1, 32), lambda i, te: (jnp.maximum(i - 1, 0), 0, 0),
                               memory_space=pltpu.SMEM),
                  pl.BlockSpec((tm, D_MODEL), lambda i, te: (i, 0)),
                  pl.BlockSpec((tm, LANES), lambda i, te: (i, 0))],
        out_specs=pl.BlockSpec(memory_space=pl.ANY),
        scratch_shapes=[pltpu.VMEM((2, SORT_ROWS, D_MODEL), F32),
                        pltpu.VMEM((FFN_TM, D_MODEL), F32), pltpu.SemaphoreType.DMA((2,))])
    max_tail = n_slots // FFN_TM - 2 * tokens // FFN_TM
    return pl.pallas_call(
        functools.partial(_dispatch_kernel, max_tail=max_tail), grid_spec=grid_spec,
        out_shape=jax.ShapeDtypeStruct((n_slots, D_MODEL), F32),
        compiler_params=_params(("arbitrary",)),
        name="moe_dispatch",
    )(tile_end, meta, meta, h, route)


def _grouped_kernel(texp_ref, nt_ref, h_ref, wg_ref, wu_ref, wd_ref, y_ref, hb_ref, acc_ref):
    del texp_ref
    i = pl.program_id(0)
    f = pl.program_id(1)

    @pl.when(i < nt_ref[0])
    def _():
        @pl.when(f == 0)
        def _():
            hb_ref[...] = h_ref[...].astype(BF16)
            acc_ref[...] = jnp.zeros_like(acc_ref)

        _swiglu_chunk(hb_ref[...], wg_ref.at[0], wu_ref.at[0], wd_ref.at[0], acc_ref)

        @pl.when(f == pl.num_programs(1) - 1)
        def _():
            y_ref[...] = acc_ref[...]

    @pl.when((i >= nt_ref[0]) & (f == 0))
    def _():
        y_ref[...] = jnp.zeros_like(y_ref)


def _grouped_swiglu(hs, tile_expert, n_tiles_used, wg, wu, wd):
    n_slots = hs.shape[0]
    tm, tf = FFN_TM, FFN_TF
    nf = D_FF // tf
    last = lambda i, nt: jnp.minimum(i, nt[0] - 1)
    fcol = lambda i, f, nt: jnp.where(i < nt[0], f, nf - 1)
    grid_spec = pltpu.PrefetchScalarGridSpec(
        num_scalar_prefetch=2, grid=(n_slots // tm, nf),
        in_specs=[pl.BlockSpec((tm, D_MODEL), lambda i, f, te, nt: (last(i, nt), 0)),
                  pl.BlockSpec((1, D_MODEL, tf), lambda i, f, te, nt: (te[i], 0, fcol(i, f, nt))),
                  pl.BlockSpec((1, D_MODEL, tf), lambda i, f, te, nt: (te[i], 0, fcol(i, f, nt))),
                  pl.BlockSpec((1, tf, D_MODEL), lambda i, f, te, nt: (te[i], fcol(i, f, nt), 0))],
        out_specs=pl.BlockSpec((tm, D_MODEL), lambda i, f, te, nt: (i, 0)),
        scratch_shapes=[pltpu.VMEM((tm, D_MODEL), BF16), pltpu.VMEM((tm, D_MODEL), F32)])
    return pl.pallas_call(
        _grouped_kernel, grid_spec=grid_spec,
        out_shape=jax.ShapeDtypeStruct((n_slots, D_MODEL), F32),
        compiler_params=_params(("arbitrary", "arbitrary")),
        name="moe_grouped_swiglu",
    )(tile_expert, n_tiles_used, hs, wg, wu, wd)


def _combine_kernel(*refs, final):
    meta_ref, next_ref, route_ref, x_ref, y_ref = refs[:5]
    if final:
        gfin_ref, o_ref, sort_ref, sems = refs[5:]
    else:
        o_ref, sort_ref, sems = refs[5:]
    tm = x_ref.shape[0]
    i = pl.program_id(0)
    slot = i & 1

    def fetch(act, m_ref, s):
        if act == "start":
            sort_ref[s, pl.ds(2 * tm, SORT_ROWS - 2 * tm), :] = jnp.zeros(
                (SORT_ROWS - 2 * tm, D_MODEL), F32)
        _run_copies(act, m_ref, lambda off, n, dst: pltpu.make_async_copy(
            y_ref.at[pl.ds(dst, n)], sort_ref.at[s, pl.ds(off, n)], sems.at[s]))

    @pl.when(i == 0)
    def _():
        fetch("start", meta_ref, slot)

    @pl.when(i + 1 < pl.num_programs(0))
    def _():
        fetch("start", next_ref, 1 - slot)

    fetch("wait", meta_ref, slot)
    ys = sort_ref[slot].astype(BF16)
    route = route_ref[...]
    ip = lax.broadcasted_iota(jnp.int32, (tm, SORT_ROWS), 1)

    def pick(col):
        pos = route[:, col:col + 1].astype(jnp.int32)
        return jnp.dot(jnp.where(ip == pos, 1.0, 0.0).astype(BF16), ys,
                       preferred_element_type=F32)

    out = (x_ref[...] + route[:, _R_G1:_R_G1 + 1] * pick(_R_POS1)
           + route[:, _R_G2:_R_G2 + 1] * pick(_R_POS2))
    o_ref[...] = _rms(out, gfin_ref[...]) if final else out


def _combine(y, route, meta, x2, final_gain=None):
    tokens = x2.shape[0]
    tm = ROW_TM
    final = final_gain is not None
    n_steps = tokens // tm
    in_specs = [pl.BlockSpec((1, 1, 32), lambda i: (i, 0, 0), memory_space=pltpu.SMEM),
                pl.BlockSpec((1, 1, 32), lambda i: (jnp.minimum(i + 1, n_steps - 1), 0, 0),
                             memory_space=pltpu.SMEM),
                pl.BlockSpec((tm, LANES), lambda i: (i, 0)),
                pl.BlockSpec((tm, D_MODEL), lambda i: (i, 0)),
                pl.BlockSpec(memory_space=pl.ANY)]
    args = [meta, meta, route, x2, y]
    if final:
        in_specs.append(pl.BlockSpec((1, D_MODEL), lambda i: (0, 0)))
        args.append(final_gain)
    return pl.pallas_call(
        functools.partial(_combine_kernel, final=final),
        grid=(tokens // tm,), in_specs=in_specs,
        out_specs=pl.BlockSpec((tm, D_MODEL), lambda i: (i, 0)),
        out_shape=jax.ShapeDtypeStruct((tokens, D_MODEL), F32),
        scratch_shapes=[pltpu.VMEM((2, SORT_ROWS, D_MODEL), F32),
                        pltpu.SemaphoreType.DMA((2,))],
        compiler_params=_params(("arbitrary",)),
        name="moe_combine",
    )(*args)


def _routing_tables(tile_rec, counts, tokens):
    tm = FFN_TM
    n_tok_tiles = tokens // ROW_TM
    max_rows = 2 * tokens + n_tok_tiles * N_EXPERTS * (RUN_ALIGN - 1)
    n_tiles = (max_rows + N_EXPERTS * (tm - 1)) // tm
    cnt = counts[0, :N_EXPERTS].astype(jnp.int32)
    tiles = (cnt + tm - 1) // tm
    tile_end = jnp.cumsum(tiles)
    start = (tile_end - tiles) * tm
    rec = tile_rec.reshape(n_tok_tiles, SUBLANES, LANES)[:, :, :N_EXPERTS].astype(jnp.int32)
    meta = jnp.concatenate([rec[:, _T_OFF], rec[:, _T_COUNT], start[None, :] + rec[:, _T_BEFORE],
                            jnp.zeros((n_tok_tiles, 32 - 3 * N_EXPERTS), jnp.int32)], axis=1)
    tile_ids = jnp.arange(n_tiles, dtype=jnp.int32)[:, None]
    tile_expert = jnp.minimum(jnp.sum(tile_ids >= tile_end[None, :], axis=1),
                              N_EXPERTS - 1).astype(jnp.int32)
    return (meta.reshape(n_tok_tiles, 1, 32), tile_expert, tile_end.astype(jnp.int32),
            n_tiles * tm)


def _moe(h, route, tile_rec, counts, x2, wg, wu, wd, final_gain=None):
    meta, tile_expert, tile_end, n_slots = _routing_tables(tile_rec, counts, x2.shape[0])
    hs = _dispatch(h, route, meta, tile_end, n_slots)
    y = _grouped_swiglu(hs, tile_expert, tile_end[-1:], wg, wu, wd)
    return _combine(y, route, meta, x2, final_gain)


def _widen_w_in(w):
    a = 3 * WIDTH_A + WIDTH_B
    kb = w[:, a:a + KV_WIDTH_B]
    vb = w[:, a + KV_WIDTH_B:]
    dup = lambda t: jnp.concatenate(
        [t[:, h * HEAD_DIM:(h + 1) * HEAD_DIM] for h in range(N_KV_B) for _ in range(2)], axis=1)
    return jnp.concatenate([w[:, :a], dup(kb), dup(vb)], axis=1).astype(BF16)


def kernel(x, positions, attn_norm, w_in, mix_norm_a, mix_norm_b, sinks, w_out, ffn_norm,
           dense_w_gate, dense_w_up, dense_w_down, router, moe_w_gate, moe_w_up, moe_w_down,
           final_norm):
    batch, seq, _ = x.shape
    depth = w_in.shape[0]
    tokens = batch * seq
    cos, sina, sinb = _rope_tables(positions)
    x2 = x.reshape(tokens, D_MODEL)
    seq3 = lambda t: t.reshape(batch, seq, t.shape[-1])
    dense_w = [w.astype(BF16) for w in (dense_w_gate, dense_w_up, dense_w_down)]
    n_moe = moe_w_gate.shape[0]
    for i in range(depth):
        jm = min(i // 2, n_moe - 1)
        qa, ka, va, qb, kb, vb, w_cast = _inproj(
            x2, attn_norm[i][None, :], _widen_w_in(w_in[i]), cos, sina, sinb, seq,
            moe_w_gate if i % 2 == 0 else moe_w_down, jm)
        if i % 2 == 0:
            wg_b = w_cast
        else:
            wd_b = w_cast
        oa = _dilated_attention(seq3(qa), seq3(ka), seq3(va)).reshape(tokens, WIDTH_A)
        ob = _swa_attention(seq3(qb), seq3(kb), seq3(vb), sinks[i]).reshape(tokens, WIDTH_B)
        j = i // 2
        last = i == depth - 1
        norms = (mix_norm_a[i][None, :], mix_norm_b[i][None, :], w_out[i].astype(BF16),
                 ffn_norm[i][None, :])
        if i % 2 == 0:
            x2, h, wu_b = _outproj(oa, ob, x2, *norms, cast=(moe_w_up, jm))
            x2 = _ffn(h, x2, *dense_w, j, final_gain=final_norm[None, :] if last else None)
        else:
            r_f32 = jnp.pad(router[j], ((0, 0), (0, LANES - N_EXPERTS)))
            r_hi = r_f32.astype(BF16)
            r_lo = (r_f32 - r_hi.astype(F32)).astype(BF16)
            router_pad = jnp.concatenate([r_hi, r_lo], axis=1)
            x2, h, route, tile_rec, counts = _outproj(oa, ob, x2, *norms, router_pad)
            x2 = _moe(h, route, tile_rec, counts, x2, wg_b, wu_b, wd_b,
                      final_gain=final_norm[None, :] if last else None)
    return x2.reshape(batch, seq, D_MODEL)
```

```python
import functools

import numpy as np
import jax
import jax.numpy as jnp
from jax import lax
from jax.experimental import pallas as pl
from jax.experimental.pallas import tpu as pltpu

D_MODEL = 1024
HEAD_DIM = 64
N_HEADS_A = 8
N_HEADS_B = 8
N_KV_B = 2
DILATED_BRANCHES = ((128, 1), (512, 4), (2048, 16))
SWA_WINDOW = 128
BLK = 128
ROPE_THETA = 10000.0
D_FF = 3584
N_EXPERTS = 8
RMS_EPS = 1e-5
WIDTH_A = N_HEADS_A * HEAD_DIM
WIDTH_B = N_HEADS_B * HEAD_DIM
KV_WIDTH_B = N_KV_B * HEAD_DIM

LANES = 128
SUBLANES = 8
NEG_BIG = -1e30
VMEM_LIMIT = 56 * 1024 * 1024
PROJ_TM = 1024

F32 = jnp.float32
BF16 = jnp.bfloat16


def _params(sem):
    return pltpu.CompilerParams(dimension_semantics=sem, vmem_limit_bytes=VMEM_LIMIT)


def _rms(x, g):
    return x * lax.rsqrt(jnp.mean(x * x, axis=-1, keepdims=True) + RMS_EPS) * g


def _rope_table_kernel(pos_ref, inv_ref, cos_ref, sina_ref, sinb_ref):
    ang = pos_ref[...].astype(F32) * inv_ref[...]
    c = jnp.cos(ang)
    s = jnp.sin(ang)
    lane = lax.broadcasted_iota(jnp.int32, ang.shape, 1)
    upper = (lane & (HEAD_DIM - 1)) >= HEAD_DIM // 2
    cos_ref[...] = c
    sina_ref[...] = jnp.where(upper, s, 0.0)
    sinb_ref[...] = jnp.where(upper, 0.0, -s)


def _rope_tables(positions):
    seq = positions.shape[0]
    inv = ROPE_THETA ** (-jnp.arange(0, HEAD_DIM, 2, dtype=F32) / HEAD_DIM)
    inv = jnp.tile(inv, LANES // (HEAD_DIM // 2))[None, :]
    rows = 1024
    tab = jax.ShapeDtypeStruct((seq, LANES), F32)
    return pl.pallas_call(
        _rope_table_kernel,
        grid=(seq // rows,),
        in_specs=[pl.BlockSpec((rows, 1), lambda i: (i, 0)),
                  pl.BlockSpec((1, LANES), lambda i: (0, 0))],
        out_specs=[pl.BlockSpec((rows, LANES), lambda i: (i, 0))] * 3,
        out_shape=[tab, tab, tab],
        compiler_params=_params(("arbitrary",)),
        name="rope_tables",
    )(positions.reshape(seq, 1), inv)


LOG2E = 1.4426950408889634
_QSCALE = HEAD_DIM ** -0.5 * LOG2E
_SEGS = (("qa", WIDTH_A, True, _QSCALE), ("ka", WIDTH_A, True, 1.0),
         ("va", WIDTH_A, False, 1.0), ("qb", WIDTH_B, True, _QSCALE),
         ("kb", 2 * KV_WIDTH_B, True, 1.0), ("vb", 2 * KV_WIDTH_B, False, 1.0))


def _cast_specs(w, layer, n_steps):
    _, n_exp, rows, cols = w.shape
    per = n_steps // n_exp
    rb = rows // per
    assert n_steps % n_exp == 0 and rows % per == 0 and rb % 16 == 0
    return (pl.BlockSpec((1, 1, rb, cols), lambda i: (layer, i // per, i % per, 0)),
            pl.BlockSpec((1, rb, cols), lambda i: (i // per, i % per, 0)),
            jax.ShapeDtypeStruct((n_exp, rows, cols), BF16))


def _inproj_kernel(x_ref, g_ref, w_ref, cos_ref, sina_ref, sinb_ref, cast_in_ref, *out_refs):
    out_refs[-1][0] = cast_in_ref[0, 0].astype(BF16)
    h = _rms(x_ref[...], g_ref[...]).astype(BF16)
    cos = cos_ref[...]
    sina = sina_ref[...]
    sinb = sinb_ref[...]
    col = 0
    for (_, width, roped, scale), o_ref in zip(_SEGS, out_refs):
        t = jnp.dot(h, w_ref[:, col:col + width], preferred_element_type=F32)
        col += width
        for j in range(width // LANES):
            tj = t[:, j * LANES:(j + 1) * LANES]
            if roped:
                tj = (tj * cos + pltpu.roll(tj, HEAD_DIM // 2, 1) * sina
                      + pltpu.roll(tj, LANES - HEAD_DIM // 2, 1) * sinb)
                if scale != 1.0:
                    tj = tj * scale
            o_ref[:, j * LANES:(j + 1) * LANES] = tj.astype(o_ref.dtype)


def _inproj(x2, g, w_ext, cos, sina, sinb, seq, cast_w, cast_layer):
    tokens = x2.shape[0]
    tm = PROJ_TM
    spt = seq // tm
    widths = [s[1] for s in _SEGS]
    total = sum(widths)
    cast_in, cast_out, cast_shape = _cast_specs(cast_w, cast_layer, tokens // tm)
    return pl.pallas_call(
        _inproj_kernel,
        grid=(tokens // tm,),
        in_specs=[pl.BlockSpec((tm, D_MODEL), lambda i: (i, 0)),
                  pl.BlockSpec((1, D_MODEL), lambda i: (0, 0)),
                  pl.BlockSpec((D_MODEL, total), lambda i: (0, 0)),
                  pl.BlockSpec((tm, LANES), lambda i: (i % spt, 0)),
                  pl.BlockSpec((tm, LANES), lambda i: (i % spt, 0)),
                  pl.BlockSpec((tm, LANES), lambda i: (i % spt, 0)),
                  cast_in],
        out_specs=[pl.BlockSpec((tm, w), lambda i: (i, 0)) for w in widths] + [cast_out],
        out_shape=[jax.ShapeDtypeStruct((tokens, w), BF16) for w in widths] + [cast_shape],
        compiler_params=_params(("arbitrary",)),
        name="inproj_rope",
    )(x2, g, w_ext, cos, sina, sinb, cast_w)


def _band_bias(max_dist):
    p = np.arange(BLK)[:, None]
    c = np.arange(2 * BLK)[None, :]
    first = c <= p
    dist = BLK + p - c
    main = (dist >= 0) & (dist <= max_dist)
    bias = np.where(np.stack([first, main]), 0.0, NEG_BIG).astype(np.float32)
    return jnp.asarray(np.concatenate([bias, bias], axis=1))


def _low_lanes():
    return lax.broadcasted_iota(jnp.int32, (BLK, LANES), 1) < HEAD_DIM


def _band_block(q, kb, vb, bias, sink=None):
    low = _low_lanes()
    zero = jnp.zeros_like(q)
    qq = jnp.concatenate([jnp.where(low, q, zero), jnp.where(low, zero, q)], axis=0)
    s = lax.dot_general(qq, kb, (((1,), (1,)), ((), ())), preferred_element_type=F32) + bias
    m = jnp.max(s, axis=1, keepdims=True)
    if sink is not None:
        m = jnp.maximum(m, sink)
    p = jnp.exp2(s - m).astype(BF16)
    ol = jnp.dot(p, jnp.concatenate([vb, jnp.ones_like(vb)], axis=1),
                 preferred_element_type=F32)
    acc = jnp.where(low, ol[:BLK, :LANES], ol[BLK:, :LANES])
    l = jnp.where(low, ol[:BLK, LANES:], ol[BLK:, LANES:])
    if sink is not None:
        ls = jnp.exp2(sink - m)
        l = l + jnp.where(low, ls[:BLK], ls[BLK:])
    return acc, jnp.where(low, m[:BLK], m[BLK:]), l


def _finish(acc, m, l):
    return acc * (1.0 / l), m + jnp.log2(l)


def _merge(o_s, lse_s, acc, m, l, want_lse=True):
    d = lse_s - m
    t = jnp.exp2(-jnp.abs(d))
    ws = jnp.where(d >= 0, 1.0, t)
    wb = jnp.where(d >= 0, t, 1.0)
    den = ws + wb * l
    o = (ws * o_s + wb * acc) * (1.0 / den)
    return (o, jnp.maximum(lse_s, m) + jnp.log2(den)) if want_lse else (o, None)


_CLASSES = 4
_LOOP_UNROLL = 32


def _dilated_kernel(q_ref, k_ref, v_ref, bias_ref, o_ref,
                    q4, k4, v4, tmp, so4, sl4, il_o, il_l, *, seq):
    cls = seq // _CLASSES
    nb4 = cls // BLK
    nb16 = seq // 16 // BLK
    sh4, sh16 = nb4.bit_length() - 1, nb16.bit_length() - 1
    chunk = 512
    shc = (cls // chunk).bit_length() - 1

    for src, dst in ((q_ref, q4), (k_ref, k4), (v_ref, v4)):
        def widen(i, c, src=src):
            r0 = pl.multiple_of(i * chunk, chunk)
            tmp[pl.ds(r0, chunk), :] = src[0, pl.ds(r0, chunk), :].astype(F32)
            return c
        lax.fori_loop(0, seq // chunk, widen, 0)

        def regroup(i, c, dst=dst):
            r = lax.shift_right_logical(i, shc)
            j = i & (cls // chunk - 1)
            d0 = pl.multiple_of(r * cls + j * chunk, chunk)
            dst[pl.ds(d0, chunk), :] = tmp[pl.ds(r + _CLASSES * chunk * j, chunk,
                                                 stride=_CLASSES), :]
            return c
        lax.fori_loop(0, seq // chunk, regroup, 0)

    def dil4(idx, c):
        r = lax.shift_right_logical(idx, sh4)
        n = idx & (nb4 - 1)
        q0 = pl.multiple_of(r * cls + n * BLK, BLK)
        k0 = pl.multiple_of(r * cls + jnp.maximum(n - 1, 0) * BLK, BLK)
        o, lse = _finish(*_band_block(q4[pl.ds(q0, BLK), :].astype(BF16),
                                      k4[pl.ds(k0, 2 * BLK), :].astype(BF16),
                                      v4[pl.ds(k0, 2 * BLK), :].astype(BF16),
                                      bias_ref[jnp.minimum(n, 1)]))
        so4[pl.ds(q0, BLK), :] = o
        sl4[pl.ds(q0, BLK), :] = lse
        return c
    lax.fori_loop(0, _CLASSES * nb4, dil4, 0, unroll=_LOOP_UNROLL)

    def dil16(idx, c):
        n = idx & (nb16 - 1)
        sub = lax.shift_right_logical(idx, sh16) & (_CLASSES - 1)
        r = lax.shift_right_logical(idx, sh16 + 2)
        q0 = r * cls + sub + _CLASSES * BLK * n
        k0 = r * cls + sub + _CLASSES * BLK * jnp.maximum(n - 1, 0)
        qs = pl.ds(q0, BLK, stride=_CLASSES)
        ks = pl.ds(k0, 2 * BLK, stride=_CLASSES)
        o_s, l_s = so4[qs, :], sl4[qs, :]
        o, lse = _merge(o_s, l_s, *_band_block(
            q4[qs, :].astype(BF16), k4[ks, :].astype(BF16), v4[ks, :].astype(BF16),
            bias_ref[jnp.minimum(n, 1)]))
        so4[qs, :] = o
        sl4[qs, :] = lse
        return c
    lax.fori_loop(0, 16 * nb16, dil16, 0, unroll=_LOOP_UNROLL)

    piece = BLK // _CLASSES

    def dil1(n, c):
        q0 = pl.multiple_of(n * BLK, BLK)
        k0 = pl.multiple_of(jnp.maximum(n - 1, 0) * BLK, BLK)
        for r in range(_CLASSES):
            s0 = pl.multiple_of(r * cls + n * piece, piece)
            il_o[pl.ds(r, piece, stride=_CLASSES), :] = so4[pl.ds(s0, piece), :]
            il_l[pl.ds(r, piece, stride=_CLASSES), :] = sl4[pl.ds(s0, piece), :]
        o_s, l_s = il_o[...], il_l[...]
        o, _ = _merge(o_s, l_s, *_band_block(
            q_ref[0, pl.ds(q0, BLK), :], k_ref[0, pl.ds(k0, 2 * BLK), :],
            v_ref[0, pl.ds(k0, 2 * BLK), :], bias_ref[jnp.minimum(n, 1)]), want_lse=False)
        o_ref[0, pl.ds(q0, BLK), :] = o.astype(o_ref.dtype)
        return c
    lax.fori_loop(0, seq // BLK, dil1, 0, unroll=_LOOP_UNROLL)


def _dilated_attention(q, k, v):
    assert DILATED_BRANCHES == ((128, 1), (512, 4), (2048, 16))
    batch, seq, width = q.shape
    assert seq % (16 * BLK) == 0 and (seq & (seq - 1)) == 0
    blk = pl.BlockSpec((1, seq, LANES), lambda b, c: (b, 0, c))
    rows = pltpu.VMEM((seq, LANES), F32)
    small = pltpu.VMEM((BLK, LANES), F32)
    return pl.pallas_call(
        functools.partial(_dilated_kernel, seq=seq),
        grid=(batch, width // LANES),
        in_specs=[blk, blk, blk, pl.BlockSpec((2, 2 * BLK, 2 * BLK), lambda b, c: (0, 0, 0))],
        out_specs=blk,
        out_shape=jax.ShapeDtypeStruct((batch, seq, width), BF16),
        scratch_shapes=[rows] * 6 + [small] * 2,
        compiler_params=_params(("arbitrary", "arbitrary")),
        name="dilated_attn",
    )(q, k, v, _band_bias(BLK))


def _swa_kernel(sink_ref, q_ref, k_ref, v_ref, bias_ref, o_ref, *, n_blocks):
    c = pl.program_id(1)
    row = lax.broadcasted_iota(jnp.int32, (2 * BLK, 1), 0)
    sink = jnp.where(row < BLK, sink_ref[2 * c], sink_ref[2 * c + 1]) * LOG2E

    def body(n, carry):
        q0 = pl.multiple_of(n * BLK, BLK)
        k0 = pl.multiple_of(jnp.maximum(n - 1, 0) * BLK, BLK)
        acc, _, l = _band_block(q_ref[0, pl.ds(q0, BLK), :], k_ref[0, pl.ds(k0, 2 * BLK), :],
                                v_ref[0, pl.ds(k0, 2 * BLK), :], bias_ref[jnp.minimum(n, 1)],
                                sink)
        o_ref[0, pl.ds(q0, BLK), :] = (acc * (1.0 / l)).astype(o_ref.dtype)
        return carry
    lax.fori_loop(0, n_blocks, body, 0, unroll=_LOOP_UNROLL)


def _swa_attention(q, k, v, sinks):
    batch, seq, width = q.shape
    qblk = pl.BlockSpec((1, seq, LANES), lambda b, c, s: (b, 0, c))
    kblk = pl.BlockSpec((1, seq, LANES), lambda b, c, s: (b, 0, c // 2))
    grid_spec = pltpu.PrefetchScalarGridSpec(
        num_scalar_prefetch=1, grid=(batch, width // LANES),
        in_specs=[qblk, kblk, kblk,
                  pl.BlockSpec((2, 2 * BLK, 2 * BLK), lambda b, c, s: (0, 0, 0))],
        out_specs=qblk)
    return pl.pallas_call(
        functools.partial(_swa_kernel, n_blocks=seq // BLK), grid_spec=grid_spec,
        out_shape=jax.ShapeDtypeStruct((batch, seq, width), BF16),
        compiler_params=_params(("arbitrary", "arbitrary")),
        name="swa_sink_attn",
    )(sinks, q, k, v, _band_bias(SWA_WINDOW - 1))


ROW_TM = 512
RUN_ALIGN = SUBLANES
SORT_ROWS = 2 * ROW_TM + N_EXPERTS * RUN_ALIGN
_R_G1, _R_G2, _R_POS1, _R_POS2 = range(4)
_T_COUNT, _T_OFF, _T_BEFORE = range(3)


def _route(logits, count_ref):
    tm = logits.shape[0]
    lane = lax.broadcasted_iota(jnp.int32, logits.shape, 1)
    v1 = jnp.max(logits, axis=1, keepdims=True)
    i1 = jnp.min(jnp.where(logits == v1, lane, LANES), axis=1, keepdims=True)
    rest = jnp.where(lane == i1, NEG_BIG, logits)
    v2 = jnp.max(rest, axis=1, keepdims=True)
    i2 = jnp.min(jnp.where(rest == v2, lane, LANES), axis=1, keepdims=True)
    e2 = jnp.exp(v2 - v1)
    g1 = 1.0 / (1.0 + e2)
    g2 = e2 / (1.0 + e2)
    sel = jnp.where((lane == i1) | (lane == i2), 1.0, 0.0)
    r = lax.broadcasted_iota(jnp.int32, (tm, tm), 0)
    c = lax.broadcasted_iota(jnp.int32, (tm, tm), 1)
    tri = jnp.where(c < r, 1.0, 0.0).astype(BF16)
    local = jnp.dot(tri, sel.astype(BF16), preferred_element_type=F32)
    count = jnp.sum(sel, axis=0, keepdims=True)
    padded = jnp.floor((count + (RUN_ALIGN - 1)) * (1.0 / RUN_ALIGN)) * RUN_ALIGN
    run_end = jnp.broadcast_to(padded, (SUBLANES, LANES))
    for shift in (1, 2, 4):
        run_end = run_end + pltpu.roll(run_end, shift, 1)
    offset = run_end[0:1] - padded
    pos = local + offset
    pos1 = jnp.sum(jnp.where(lane == i1, pos, 0.0), axis=1, keepdims=True)
    pos2 = jnp.sum(jnp.where(lane == i2, pos, 0.0), axis=1, keepdims=True)
    before = count_ref[...]
    count_ref[...] = before + padded
    rec = jnp.zeros(logits.shape, F32)
    for ln, col in ((_R_G1, g1), (_R_G2, g2), (_R_POS1, pos1), (_R_POS2, pos2)):
        rec = jnp.where(lane == ln, col, rec)
    row = lax.broadcasted_iota(jnp.int32, (SUBLANES, LANES), 0)
    tile = jnp.zeros((SUBLANES, LANES), F32)
    for rw, val in ((_T_COUNT, padded), (_T_OFF, offset), (_T_BEFORE, before)):
        tile = jnp.where(row == rw, val, tile)
    return rec, tile


def _outproj_kernel(*refs, with_router):
    oa_ref, ob_ref, x_ref, ga_ref, gb_ref, w_ref, gf_ref = refs[:7]
    rest = refs[7:]
    if with_router:
        r_ref, xo_ref, ho_ref, route_ref, tile_ref, count_ref = rest
    else:
        cast_in_ref, xo_ref, ho_ref, cast_out_ref = rest
        cast_out_ref[0] = cast_in_ref[0, 0].astype(BF16)
    na = _rms(oa_ref[...].astype(F32), ga_ref[...]).astype(BF16)
    nb = _rms(ob_ref[...].astype(F32), gb_ref[...]).astype(BF16)
    xn = (x_ref[...] + jnp.dot(na, w_ref[:WIDTH_A, :], preferred_element_type=F32)
          + jnp.dot(nb, w_ref[WIDTH_A:, :], preferred_element_type=F32))
    xo_ref[...] = xn
    hf = _rms(xn, gf_ref[...])
    h_hi = hf.astype(BF16)
    ho_ref[...] = h_hi
    if with_router:
        @pl.when(pl.program_id(0) == 0)
        def _():
            count_ref[...] = jnp.zeros_like(count_ref)
        h_lo = (hf - h_hi.astype(F32)).astype(BF16)
        t = jnp.dot(h_hi, r_ref[...], preferred_element_type=F32)
        logits = (t[:, :LANES] + t[:, LANES:]
                  + jnp.dot(h_lo, r_ref[:, :LANES], preferred_element_type=F32))
        lane = lax.broadcasted_iota(jnp.int32, logits.shape, 1)
        logits = jnp.where(lane < N_EXPERTS, logits, NEG_BIG)
        route_ref[...], tile_ref[...] = _route(logits, count_ref)


def _outproj(oa, ob, x2, ga, gb, w_out, gf, router_pad=None, cast=None):
    tokens = x2.shape[0]
    with_router = router_pad is not None
    assert with_router != (cast is not None)
    tm = ROW_TM if with_router else PROJ_TM
    row = lambda w: pl.BlockSpec((tm, w), lambda i: (i, 0))
    const = lambda a, b: pl.BlockSpec((a, b), lambda i: (0, 0))
    in_specs = [row(WIDTH_A), row(WIDTH_B), row(D_MODEL), const(1, WIDTH_A), const(1, WIDTH_B),
                const(D_MODEL, D_MODEL), const(1, D_MODEL)]
    out_specs = [row(D_MODEL), row(D_MODEL)]
    out_shape = [jax.ShapeDtypeStruct((tokens, D_MODEL), F32),
                 jax.ShapeDtypeStruct((tokens, D_MODEL), BF16)]
    args = [oa, ob, x2, ga, gb, w_out, gf]
    if with_router:
        in_specs.append(const(D_MODEL, 2 * LANES))
        out_specs += [row(LANES), pl.BlockSpec((SUBLANES, LANES), lambda i: (i, 0)),
                      const(1, LANES)]
        out_shape += [jax.ShapeDtypeStruct((tokens, LANES), F32),
                      jax.ShapeDtypeStruct((tokens // tm * SUBLANES, LANES), F32),
                      jax.ShapeDtypeStruct((1, LANES), F32)]
        args.append(router_pad)
    else:
        cast_in, cast_out, cast_shape = _cast_specs(*cast, tokens // tm)
        in_specs.append(cast_in)
        out_specs.append(cast_out)
        out_shape.append(cast_shape)
        args.append(cast[0])
    return pl.pallas_call(
        functools.partial(_outproj_kernel, with_router=with_router),
        grid=(tokens // tm,), in_specs=in_specs, out_specs=out_specs, out_shape=out_shape,
        compiler_params=_params(("arbitrary",)),
        name="mix_outproj_router" if with_router else "mix_outproj",
    )(*args)


def _silu(x):
    return x * (1.0 / (1.0 + jnp.exp(-x)))


FFN_TM = 1024
FFN_TF = 1792
FFN_TC = 512


def _swiglu_chunk(h, wg_ref, wu_ref, wd_ref, acc_ref):
    for c0 in range(0, FFN_TF, FFN_TC):
        c1 = min(c0 + FFN_TC, FFN_TF)
        g = jnp.dot(h, wg_ref[:, c0:c1], preferred_element_type=F32)
        u = jnp.dot(h, wu_ref[:, c0:c1], preferred_element_type=F32)
        acc_ref[...] += jnp.dot((_silu(g) * u).astype(BF16), wd_ref[c0:c1, :],
                                preferred_element_type=F32)


def _ffn_kernel(*refs, final):
    h_ref, wg_ref, wu_ref, wd_ref, x_ref = refs[:5]
    if final:
        gfin_ref, o_ref, acc_ref = refs[5:]
    else:
        o_ref, acc_ref = refs[5:]
    f = pl.program_id(1)

    @pl.when(f == 0)
    def _():
        acc_ref[...] = x_ref[...]

    _swiglu_chunk(h_ref[...], wg_ref.at[0], wu_ref.at[0], wd_ref.at[0], acc_ref)

    @pl.when(f == pl.num_programs(1) - 1)
    def _():
        o_ref[...] = _rms(acc_ref[...], gfin_ref[...]) if final else acc_ref[...]


def _ffn(h, x2, wg, wu, wd, layer, final_gain=None):
    tokens = x2.shape[0]
    tm, tf = FFN_TM, FFN_TF
    final = final_gain is not None
    tok = lambda w: pl.BlockSpec((tm, w), lambda i, f: (i, 0))
    in_specs = [tok(D_MODEL),
                pl.BlockSpec((1, D_MODEL, tf), lambda i, f: (layer, 0, f)),
                pl.BlockSpec((1, D_MODEL, tf), lambda i, f: (layer, 0, f)),
                pl.BlockSpec((1, tf, D_MODEL), lambda i, f: (layer, f, 0)),
                tok(D_MODEL)]
    args = [h, wg, wu, wd, x2]
    if final:
        in_specs.append(pl.BlockSpec((1, D_MODEL), lambda i, f: (0, 0)))
        args.append(final_gain)
    return pl.pallas_call(
        functools.partial(_ffn_kernel, final=final),
        grid=(tokens // tm, D_FF // tf), in_specs=in_specs, out_specs=tok(D_MODEL),
        out_shape=jax.ShapeDtypeStruct((tokens, D_MODEL), F32),
        scratch_shapes=[pltpu.VMEM((tm, D_MODEL), F32)],
        compiler_params=_params(("arbitrary", "arbitrary")),
        name="dense_swiglu",
    )(*args)


def _run_copies(act, meta_ref, make_copy):
    for e in range(N_EXPERTS):
        n = pl.multiple_of(meta_ref[0, 0, N_EXPERTS + e], RUN_ALIGN)

        @pl.when(n > 0)
        def _(e=e, n=n):
            off = pl.multiple_of(meta_ref[0, 0, e], RUN_ALIGN)
            dst = pl.multiple_of(meta_ref[0, 0, 2 * N_EXPERTS + e], RUN_ALIGN)
            getattr(make_copy(off, n, dst), act)()


def _sort_matrix(route, n_rows):
    rec_t = route.T
    p1 = rec_t[_R_POS1:_R_POS1 + 1, :].astype(jnp.int32)
    p2 = rec_t[_R_POS2:_R_POS2 + 1, :].astype(jnp.int32)
    ip = lax.broadcasted_iota(jnp.int32, (n_rows, route.shape[0]), 0)
    return jnp.where((ip == p1) | (ip == p2), 1.0, 0.0).astype(BF16)


def _dispatch_kernel(tend_ref, meta_ref, prev_ref, h_ref, route_ref, out_ref, sort_ref, zero_ref,
                     sems, *, max_tail):
    n_tiles = out_ref.shape[0] // FFN_TM

    @pl.when(pl.program_id(0) == 0)
    def _():
        zero_ref[...] = jnp.zeros_like(zero_ref)

        def zero_tile(t):
            return pltpu.make_async_copy(zero_ref, out_ref.at[pl.ds(t * FFN_TM, FFN_TM)],
                                         sems.at[0])

        def has_tiles(e):
            return tend_ref[e] > (tend_ref[e - 1] if e else 0)

        for act in ("start", "wait"):
            for e in range(N_EXPERTS):
                @pl.when(has_tiles(e))
                def _(e=e, act=act):
                    getattr(zero_tile(tend_ref[e] - 1), act)()
            for k in range(max_tail):
                t = tend_ref[N_EXPERTS - 1] + k

                @pl.when(t < n_tiles)
                def _(t=t, act=act):
                    getattr(zero_tile(t), act)()

    i = pl.program_id(0)
    slot = i & 1

    def run_copy(s):
        return lambda off, n, dst: pltpu.make_async_copy(
            sort_ref.at[s, pl.ds(off, n)], out_ref.at[pl.ds(dst, n)], sems.at[s])

    sort_ref[slot] = jnp.dot(_sort_matrix(route_ref[...], SORT_ROWS), h_ref[...],
                             preferred_element_type=F32)
    _run_copies("start", meta_ref, run_copy(slot))

    @pl.when(i > 0)
    def _():
        _run_copies("wait", prev_ref, run_copy(1 - slot))

    @pl.when(i == pl.num_programs(0) - 1)
    def _():
        _run_copies("wait", meta_ref, run_copy(slot))


def _dispatch(h, route, meta, tile_end, n_slots):
    tokens = h.shape[0]
    tm = ROW_TM
    grid_spec = pltpu.PrefetchScalarGridSpec(
        num_scalar_prefetch=1, grid=(tokens // tm,),
        in_specs=[pl.BlockSpec((1, 1, 32), lambda i, te: (i, 0, 0), memory_space=pltpu.SMEM),
                  pl.BlockSpec((1, 1, 32), lambda i, te: (jnp.maximum(i - 1, 0), 0, 0),
                               memory_space=pltpu.SMEM),
                  pl.BlockSpec((tm, D_MODEL), lambda i, te: (i, 0)),
                  pl.BlockSpec((tm, LANES), lambda i, te: (i, 0))],
        out_specs=pl.BlockSpec(memory_space=pl.ANY),
        scratch_shapes=[pltpu.VMEM((2, SORT_ROWS, D_MODEL), F32),
                        pltpu.VMEM((FFN_TM, D_MODEL), F32), pltpu.SemaphoreType.DMA((2,))])
    max_tail = n_slots // FFN_TM - 2 * tokens // FFN_TM
    return pl.pallas_call(
        functools.partial(_dispatch_kernel, max_tail=max_tail), grid_spec=grid_spec,
        out_shape=jax.ShapeDtypeStruct((n_slots, D_MODEL), F32),
        compiler_params=_params(("arbitrary",)),
        name="moe_dispatch",
    )(tile_end, meta, meta, h, route)


def _grouped_kernel(texp_ref, nt_ref, h_ref, wg_ref, wu_ref, wd_ref, y_ref, hb_ref, acc_ref):
    del texp_ref
    i = pl.program_id(0)
    f = pl.program_id(1)

    @pl.when(i < nt_ref[0])
    def _():
        @pl.when(f == 0)
        def _():
            hb_ref[...] = h_ref[...].astype(BF16)
            acc_ref[...] = jnp.zeros_like(acc_ref)

        _swiglu_chunk(hb_ref[...], wg_ref.at[0], wu_ref.at[0], wd_ref.at[0], acc_ref)

        @pl.when(f == pl.num_programs(1) - 1)
        def _():
            y_ref[...] = acc_ref[...]

    @pl.when((i >= nt_ref[0]) & (f == 0))
    def _():
        y_ref[...] = jnp.zeros_like(y_ref)


def _grouped_swiglu(hs, tile_expert, n_tiles_used, wg, wu, wd):
    n_slots = hs.shape[0]
    tm, tf = FFN_TM, FFN_TF
    nf = D_FF // tf
    last = lambda i, nt: jnp.minimum(i, nt[0] - 1)
    fcol = lambda i, f, nt: jnp.where(i < nt[0], f, nf - 1)
    grid_spec = pltpu.PrefetchScalarGridSpec(
        num_scalar_prefetch=2, grid=(n_slots // tm, nf),
        in_specs=[pl.BlockSpec((tm, D_MODEL), lambda i, f, te, nt: (last(i, nt), 0)),
                  pl.BlockSpec((1, D_MODEL, tf), lambda i, f, te, nt: (te[i], 0, fcol(i, f, nt))),
                  pl.BlockSpec((1, D_MODEL, tf), lambda i, f, te, nt: (te[i], 0, fcol(i, f, nt))),
                  pl.BlockSpec((1, tf, D_MODEL), lambda i, f, te, nt: (te[i], fcol(i, f, nt), 0))],
        out_specs=pl.BlockSpec((tm, D_MODEL), lambda i, f, te, nt: (i, 0)),
        scratch_shapes=[pltpu.VMEM((tm, D_MODEL), BF16), pltpu.VMEM((tm, D_MODEL), F32)])
    return pl.pallas_call(
        _grouped_kernel, grid_spec=grid_spec,
        out_shape=jax.ShapeDtypeStruct((n_slots, D_MODEL), F32),
        compiler_params=_params(("arbitrary", "arbitrary")),
        name="moe_grouped_swiglu",
    )(tile_expert, n_tiles_used, hs, wg, wu, wd)


def _combine_kernel(*refs, final):
    meta_ref, next_ref, route_ref, x_ref, y_ref = refs[:5]
    if final:
        gfin_ref, o_ref, sort_ref, sems = refs[5:]
    else:
        o_ref, sort_ref, sems = refs[5:]
    tm = x_ref.shape[0]
    i = pl.program_id(0)
    slot = i & 1

    def fetch(act, m_ref, s):
        if act == "start":
            sort_ref[s, pl.ds(2 * tm, SORT_ROWS - 2 * tm), :] = jnp.zeros(
                (SORT_ROWS - 2 * tm, D_MODEL), F32)
        _run_copies(act, m_ref, lambda off, n, dst: pltpu.make_async_copy(
            y_ref.at[pl.ds(dst, n)], sort_ref.at[s, pl.ds(off, n)], sems.at[s]))

    @pl.when(i == 0)
    def _():
        fetch("start", meta_ref, slot)

    @pl.when(i + 1 < pl.num_programs(0))
    def _():
        fetch("start", next_ref, 1 - slot)

    fetch("wait", meta_ref, slot)
    ys = sort_ref[slot].astype(BF16)
    route = route_ref[...]
    ip = lax.broadcasted_iota(jnp.int32, (tm, SORT_ROWS), 1)

    def pick(col):
        pos = route[:, col:col + 1].astype(jnp.int32)
        return jnp.dot(jnp.where(ip == pos, 1.0, 0.0).astype(BF16), ys,
                       preferred_element_type=F32)

    out = (x_ref[...] + route[:, _R_G1:_R_G1 + 1] * pick(_R_POS1)
           + route[:, _R_G2:_R_G2 + 1] * pick(_R_POS2))
    o_ref[...] = _rms(out, gfin_ref[...]) if final else out


def _combine(y, route, meta, x2, final_gain=None):
    tokens = x2.shape[0]
    tm = ROW_TM
    final = final_gain is not None
    n_steps = tokens // tm
    in_specs = [pl.BlockSpec((1, 1, 32), lambda i: (i, 0, 0), memory_space=pltpu.SMEM),
                pl.BlockSpec((1, 1, 32), lambda i: (jnp.minimum(i + 1, n_steps - 1), 0, 0),
                             memory_space=pltpu.SMEM),
                pl.BlockSpec((tm, LANES), lambda i: (i, 0)),
                pl.BlockSpec((tm, D_MODEL), lambda i: (i, 0)),
                pl.BlockSpec(memory_space=pl.ANY)]
    args = [meta, meta, route, x2, y]
    if final:
        in_specs.append(pl.BlockSpec((1, D_MODEL), lambda i: (0, 0)))
        args.append(final_gain)
    return pl.pallas_call(
        functools.partial(_combine_kernel, final=final),
        grid=(tokens // tm,), in_specs=in_specs,
        out_specs=pl.BlockSpec((tm, D_MODEL), lambda i: (i, 0)),
        out_shape=jax.ShapeDtypeStruct((tokens, D_MODEL), F32),
        scratch_shapes=[pltpu.VMEM((2, SORT_ROWS, D_MODEL), F32),
                        pltpu.SemaphoreType.DMA((2,))],
        compiler_params=_params(("arbitrary",)),
        name="moe_combine",
    )(*args)


def _routing_tables(tile_rec, counts, tokens):
    tm = FFN_TM
    n_tok_tiles = tokens // ROW_TM
    max_rows = 2 * tokens + n_tok_tiles * N_EXPERTS * (RUN_ALIGN - 1)
    n_tiles = (max_rows + N_EXPERTS * (tm - 1)) // tm
    cnt = counts[0, :N_EXPERTS].astype(jnp.int32)
    tiles = (cnt + tm - 1) // tm
    tile_end = jnp.cumsum(tiles)
    start = (tile_end - tiles) * tm
    rec = tile_rec.reshape(n_tok_tiles, SUBLANES, LANES)[:, :, :N_EXPERTS].astype(jnp.int32)
    meta = jnp.concatenate([rec[:, _T_OFF], rec[:, _T_COUNT], start[None, :] + rec[:, _T_BEFORE],
                            jnp.zeros((n_tok_tiles, 32 - 3 * N_EXPERTS), jnp.int32)], axis=1)
    tile_ids = jnp.arange(n_tiles, dtype=jnp.int32)[:, None]
    tile_expert = jnp.minimum(jnp.sum(tile_ids >= tile_end[None, :], axis=1),
                              N_EXPERTS - 1).astype(jnp.int32)
    return (meta.reshape(n_tok_tiles, 1, 32), tile_expert, tile_end.astype(jnp.int32),
            n_tiles * tm)


def _moe(h, route, tile_rec, counts, x2, wg, wu, wd, final_gain=None):
    meta, tile_expert, tile_end, n_slots = _routing_tables(tile_rec, counts, x2.shape[0])
    hs = _dispatch(h, route, meta, tile_end, n_slots)
    y = _grouped_swiglu(hs, tile_expert, tile_end[-1:], wg, wu, wd)
    return _combine(y, route, meta, x2, final_gain)


def _widen_w_in(w):
    a = 3 * WIDTH_A + WIDTH_B
    kb = w[:, a:a + KV_WIDTH_B]
    vb = w[:, a + KV_WIDTH_B:]
    dup = lambda t: jnp.concatenate(
        [t[:, h * HEAD_DIM:(h + 1) * HEAD_DIM] for h in range(N_KV_B) for _ in range(2)], axis=1)
    return jnp.concatenate([w[:, :a], dup(kb), dup(vb)], axis=1).astype(BF16)


def kernel(x, positions, attn_norm, w_in, mix_norm_a, mix_norm_b, sinks, w_out, ffn_norm,
           dense_w_gate, dense_w_up, dense_w_down, router, moe_w_gate, moe_w_up, moe_w_down,
           final_norm):
    batch, seq, _ = x.shape
    depth = w_in.shape[0]
    tokens = batch * seq
    cos, sina, sinb = _rope_tables(positions)
    x2 = x.reshape(tokens, D_MODEL)
    seq3 = lambda t: t.reshape(batch, seq, t.shape[-1])
    dense_w = [w.astype(BF16) for w in (dense_w_gate, dense_w_up, dense_w_down)]
    n_moe = moe_w_gate.shape[0]
    for i in range(depth):
        jm = min(i // 2, n_moe - 1)
        qa, ka, va, qb, kb, vb, w_cast = _inproj(
            x2, attn_norm[i][None, :], _widen_w_in(w_in[i]), cos, sina, sinb, seq,
            moe_w_gate if i % 2 == 0 else moe_w_down, jm)
        if i % 2 == 0:
            wg_b = w_cast
        else:
            wd_b = w_cast
        oa = _dilated_attention(seq3(qa), seq3(ka), seq3(va)).reshape(tokens, WIDTH_A)
        ob = _swa_attention(seq3(qb), seq3(kb), seq3(vb), sinks[i]).reshape(tokens, WIDTH_B)
        j = i // 2
        last = i == depth - 1
        norms = (mix_norm_a[i][None, :], mix_norm_b[i][None, :], w_out[i].astype(BF16),
                 ffn_norm[i][None, :])
        if i % 2 == 0:
            x2, h, wu_b = _outproj(oa, ob, x2, *norms, cast=(moe_w_up, jm))
            x2 = _ffn(h, x2, *dense_w, j, final_gain=final_norm[None, :] if last else None)
        else:
            r_f32 = jnp.pad(router[j], ((0, 0), (0, LANES - N_EXPERTS)))
            r_hi = r_f32.astype(BF16)
            r_lo = (r_f32 - r_hi.astype(F32)).astype(BF16)
            router_pad = jnp.concatenate([r_hi, r_lo], axis=1)
            x2, h, route, tile_rec, counts = _outproj(oa, ob, x2, *norms, router_pad)
            x2 = _moe(h, route, tile_rec, counts, x2, wg_b, wu_b, wd_b,
                      final_gain=final_norm[None, :] if last else None)
    return x2.reshape(batch, seq, D_MODEL)
```

```python
import functools

import numpy as np
import jax
import jax.numpy as jnp
from jax import lax
from jax.experimental import pallas as pl
from jax.experimental.pallas import tpu as pltpu

D_MODEL = 1024
HEAD_DIM = 64
N_HEADS_A = 8
N_HEADS_B = 8
N_KV_B = 2
DILATED_BRANCHES = ((128, 1), (512, 4), (2048, 16))
SWA_WINDOW = 128
BLK = 128
ROPE_THETA = 10000.0
D_FF = 3584
N_EXPERTS = 8
RMS_EPS = 1e-5
WIDTH_A = N_HEADS_A * HEAD_DIM
WIDTH_B = N_HEADS_B * HEAD_DIM
KV_WIDTH_B = N_KV_B * HEAD_DIM

LANES = 128
SUBLANES = 8
NEG_BIG = -1e30
VMEM_LIMIT = 56 * 1024 * 1024
PROJ_TM = 1024

F32 = jnp.float32
BF16 = jnp.bfloat16


def _params(sem):
    return pltpu.CompilerParams(dimension_semantics=sem, vmem_limit_bytes=VMEM_LIMIT)


def _rms(x, g):
    return x * lax.rsqrt(jnp.mean(x * x, axis=-1, keepdims=True) + RMS_EPS) * g


def _rope_table_kernel(pos_ref, inv_ref, cos_ref, sina_ref, sinb_ref):
    ang = pos_ref[...].astype(F32) * inv_ref[...]
    c = jnp.cos(ang)
    s = jnp.sin(ang)
    lane = lax.broadcasted_iota(jnp.int32, ang.shape, 1)
    upper = (lane & (HEAD_DIM - 1)) >= HEAD_DIM // 2
    cos_ref[...] = c
    sina_ref[...] = jnp.where(upper, s, 0.0)
    sinb_ref[...] = jnp.where(upper, 0.0, -s)


def _rope_tables(positions):
    seq = positions.shape[0]
    inv = ROPE_THETA ** (-jnp.arange(0, HEAD_DIM, 2, dtype=F32) / HEAD_DIM)
    inv = jnp.tile(inv, LANES // (HEAD_DIM // 2))[None, :]
    rows = 1024
    tab = jax.ShapeDtypeStruct((seq, LANES), F32)
    return pl.pallas_call(
        _rope_table_kernel,
        grid=(seq // rows,),
        in_specs=[pl.BlockSpec((rows, 1), lambda i: (i, 0)),
                  pl.BlockSpec((1, LANES), lambda i: (0, 0))],
        out_specs=[pl.BlockSpec((rows, LANES), lambda i: (i, 0))] * 3,
        out_shape=[tab, tab, tab],
        compiler_params=_params(("arbitrary",)),
        name="rope_tables",
    )(positions.reshape(seq, 1), inv)


LOG2E = 1.4426950408889634
_QSCALE = HEAD_DIM ** -0.5 * LOG2E
_SEGS = (("qa", WIDTH_A, True, _QSCALE), ("ka", WIDTH_A, True, 1.0),
         ("va", WIDTH_A, False, 1.0), ("qb", WIDTH_B, True, _QSCALE),
         ("kb", 2 * KV_WIDTH_B, True, 1.0), ("vb", 2 * KV_WIDTH_B, False, 1.0))


def _cast_specs(w, layer, n_steps):
    _, n_exp, rows, cols = w.shape
    per = n_steps // n_exp
    rb = rows // per
    assert n_steps % n_exp == 0 and rows % per == 0 and rb % 16 == 0
    return (pl.BlockSpec((1, 1, rb, cols), lambda i: (layer, i // per, i % per, 0)),
            pl.BlockSpec((1, rb, cols), lambda i: (i // per, i % per, 0)),
            jax.ShapeDtypeStruct((n_exp, rows, cols), BF16))


def _inproj_kernel(x_ref, g_ref, w_ref, cos_ref, sina_ref, sinb_ref, cast_in_ref, *out_refs):
    out_refs[-1][0] = cast_in_ref[0, 0].astype(BF16)
    h = _rms(x_ref[...], g_ref[...]).astype(BF16)
    cos = cos_ref[...]
    sina = sina_ref[...]
    sinb = sinb_ref[...]
    col = 0
    for (_, width, roped, scale), o_ref in zip(_SEGS, out_refs):
        t = jnp.dot(h, w_ref[:, col:col + width], preferred_element_type=F32)
        col += width
        for j in range(width // LANES):
            tj = t[:, j * LANES:(j + 1) * LANES]
            if roped:
                tj = (tj * cos + pltpu.roll(tj, HEAD_DIM // 2, 1) * sina
                      + pltpu.roll(tj, LANES - HEAD_DIM // 2, 1) * sinb)
                if scale != 1.0:
                    tj = tj * scale
            o_ref[:, j * LANES:(j + 1) * LANES] = tj.astype(o_ref.dtype)


def _inproj(x2, g, w_ext, cos, sina, sinb, seq, cast_w, cast_layer):
    tokens = x2.shape[0]
    tm = PROJ_TM
    spt = seq // tm
    widths = [s[1] for s in _SEGS]
    total = sum(widths)
    cast_in, cast_out, cast_shape = _cast_specs(cast_w, cast_layer, tokens // tm)
    return pl.pallas_call(
        _inproj_kernel,
        grid=(tokens // tm,),
        in_specs=[pl.BlockSpec((tm, D_MODEL), lambda i: (i, 0)),
                  pl.BlockSpec((1, D_MODEL), lambda i: (0, 0)),
                  pl.BlockSpec((D_MODEL, total), lambda i: (0, 0)),
                  pl.BlockSpec((tm, LANES), lambda i: (i % spt, 0)),
                  pl.BlockSpec((tm, LANES), lambda i: (i % spt, 0)),
                  pl.BlockSpec((tm, LANES), lambda i: (i % spt, 0)),
                  cast_in],
        out_specs=[pl.BlockSpec((tm, w), lambda i: (i, 0)) for w in widths] + [cast_out],
        out_shape=[jax.ShapeDtypeStruct((tokens, w), BF16) for w in widths] + [cast_shape],
        compiler_params=_params(("arbitrary",)),
        name="inproj_rope",
    )(x2, g, w_ext, cos, sina, sinb, cast_w)


def _band_bias(max_dist):
    p = np.arange(BLK)[:, None]
    c = np.arange(2 * BLK)[None, :]
    first = c <= p
    dist = BLK + p - c
    main = (dist >= 0) & (dist <= max_dist)
    bias = np.where(np.stack([first, main]), 0.0, NEG_BIG).astype(np.float32)
    return jnp.asarray(np.concatenate([bias, bias], axis=1))


def _low_lanes():
    return lax.broadcasted_iota(jnp.int32, (BLK, LANES), 1) < HEAD_DIM


def _band_block(q, kb, vb, bias, sink=None):
    low = _low_lanes()
    zero = jnp.zeros_like(q)
    qq = jnp.concatenate([jnp.where(low, q, zero), jnp.where(low, zero, q)], axis=0)
    s = lax.dot_general(qq, kb, (((1,), (1,)), ((), ())), preferred_element_type=F32) + bias
    m = jnp.max(s, axis=1, keepdims=True)
    if sink is not None:
        m = jnp.maximum(m, sink)
    p = jnp.exp2(s - m).astype(BF16)
    ol = jnp.dot(p, jnp.concatenate([vb, jnp.ones_like(vb)], axis=1),
                 preferred_element_type=F32)
    acc = jnp.where(low, ol[:BLK, :LANES], ol[BLK:, :LANES])
    l = jnp.where(low, ol[:BLK, LANES:], ol[BLK:, LANES:])
    if sink is not None:
        ls = jnp.exp2(sink - m)
        l = l + jnp.where(low, ls[:BLK], ls[BLK:])
    return acc, jnp.where(low, m[:BLK], m[BLK:]), l


def _finish(acc, m, l):
    return acc * (1.0 / l), m + jnp.log2(l)


def _merge(o_s, lse_s, acc, m, l, want_lse=True):
    d = lse_s - m
    t = jnp.exp2(-jnp.abs(d))
    ws = jnp.where(d >= 0, 1.0, t)
    wb = jnp.where(d >= 0, t, 1.0)
    den = ws + wb * l
    o = (ws * o_s + wb * acc) * (1.0 / den)
    return (o, jnp.maximum(lse_s, m) + jnp.log2(den)) if want_lse else (o, None)


_CLASSES = 4
_LOOP_UNROLL = 32


def _dilated_kernel(q_ref, k_ref, v_ref, bias_ref, o_ref,
                    q4, k4, v4, tmp, so4, sl4, il_o, il_l, *, seq):
    cls = seq // _CLASSES
    nb4 = cls // BLK
    nb16 = seq // 16 // BLK
    sh4, sh16 = nb4.bit_length() - 1, nb16.bit_length() - 1
    chunk = 512
    shc = (cls // chunk).bit_length() - 1

    for src, dst in ((q_ref, q4), (k_ref, k4), (v_ref, v4)):
        def widen(i, c, src=src):
            r0 = pl.multiple_of(i * chunk, chunk)
            tmp[pl.ds(r0, chunk), :] = src[0, pl.ds(r0, chunk), :].astype(F32)
            return c
        lax.fori_loop(0, seq // chunk, widen, 0)

        def regroup(i, c, dst=dst):
            r = lax.shift_right_logical(i, shc)
            j = i & (cls // chunk - 1)
            d0 = pl.multiple_of(r * cls + j * chunk, chunk)
            dst[pl.ds(d0, chunk), :] = tmp[pl.ds(r + _CLASSES * chunk * j, chunk,
                                                 stride=_CLASSES), :]
            return c
        lax.fori_loop(0, seq // chunk, regroup, 0)

    def dil4(idx, c):
        r = lax.shift_right_logical(idx, sh4)
        n = idx & (nb4 - 1)
        q0 = pl.multiple_of(r * cls + n * BLK, BLK)
        k0 = pl.multiple_of(r * cls + jnp.maximum(n - 1, 0) * BLK, BLK)
        o, lse = _finish(*_band_block(q4[pl.ds(q0, BLK), :].astype(BF16),
                                      k4[pl.ds(k0, 2 * BLK), :].astype(BF16),
                                      v4[pl.ds(k0, 2 * BLK), :].astype(BF16),
                                      bias_ref[jnp.minimum(n, 1)]))
        so4[pl.ds(q0, BLK), :] = o
        sl4[pl.ds(q0, BLK), :] = lse
        return c
    lax.fori_loop(0, _CLASSES * nb4, dil4, 0, unroll=_LOOP_UNROLL)

    def dil16(idx, c):
        n = idx & (nb16 - 1)
        sub = lax.shift_right_logical(idx, sh16) & (_CLASSES - 1)
        r = lax.shift_right_logical(idx, sh16 + 2)
        q0 = r * cls + sub + _CLASSES * BLK * n
        k0 = r * cls + sub + _CLASSES * BLK * jnp.maximum(n - 1, 0)
        qs = pl.ds(q0, BLK, stride=_CLASSES)
        ks = pl.ds(k0, 2 * BLK, stride=_CLASSES)
        o_s, l_s = so4[qs, :], sl4[qs, :]
        o, lse = _merge(o_s, l_s, *_band_block(
            q4[qs, :].astype(BF16), k4[ks, :].astype(BF16), v4[ks, :].astype(BF16),
            bias_ref[jnp.minimum(n, 1)]))
        so4[qs, :] = o
        sl4[qs, :] = lse
        return c
    lax.fori_loop(0, 16 * nb16, dil16, 0, unroll=_LOOP_UNROLL)

    piece = BLK // _CLASSES

    def dil1(n, c):
        q0 = pl.multiple_of(n * BLK, BLK)
        k0 = pl.multiple_of(jnp.maximum(n - 1, 0) * BLK, BLK)
        for r in range(_CLASSES):
            s0 = pl.multiple_of(r * cls + n * piece, piece)
            il_o[pl.ds(r, piece, stride=_CLASSES), :] = so4[pl.ds(s0, piece), :]
            il_l[pl.ds(r, piece, stride=_CLASSES), :] = sl4[pl.ds(s0, piece), :]
        o_s, l_s = il_o[...], il_l[...]
        o, _ = _merge(o_s, l_s, *_band_block(
            q_ref[0, pl.ds(q0, BLK), :], k_ref[0, pl.ds(k0, 2 * BLK), :],
            v_ref[0, pl.ds(k0, 2 * BLK), :], bias_ref[jnp.minimum(n, 1)]), want_lse=False)
        o_ref[0, pl.ds(q0, BLK), :] = o.astype(o_ref.dtype)
        return c
    lax.fori_loop(0, seq // BLK, dil1, 0, unroll=_LOOP_UNROLL)


def _dilated_attention(q, k, v):
    assert DILATED_BRANCHES == ((128, 1), (512, 4), (2048, 16))
    batch, seq, width = q.shape
    assert seq % (16 * BLK) == 0 and (seq & (seq - 1)) == 0
    blk = pl.BlockSpec((1, seq, LANES), lambda b, c: (b, 0, c))
    rows = pltpu.VMEM((seq, LANES), F32)
    small = pltpu.VMEM((BLK, LANES), F32)
    return pl.pallas_call(
        functools.partial(_dilated_kernel, seq=seq),
        grid=(batch, width // LANES),
        in_specs=[blk, blk, blk, pl.BlockSpec((2, 2 * BLK, 2 * BLK), lambda b, c: (0, 0, 0))],
        out_specs=blk,
        out_shape=jax.ShapeDtypeStruct((batch, seq, width), BF16),
        scratch_shapes=[rows] * 6 + [small] * 2,
        compiler_params=_params(("arbitrary", "arbitrary")),
        name="dilated_attn",
    )(q, k, v, _band_bias(BLK))


def _swa_kernel(sink_ref, q_ref, k_ref, v_ref, bias_ref, o_ref, *, n_blocks):
    c = pl.program_id(1)
    row = lax.broadcasted_iota(jnp.int32, (2 * BLK, 1), 0)
    sink = jnp.where(row < BLK, sink_ref[2 * c], sink_ref[2 * c + 1]) * LOG2E

    def body(n, carry):
        q0 = pl.multiple_of(n * BLK, BLK)
        k0 = pl.multiple_of(jnp.maximum(n - 1, 0) * BLK, BLK)
        acc, _, l = _band_block(q_ref[0, pl.ds(q0, BLK), :], k_ref[0, pl.ds(k0, 2 * BLK), :],
                                v_ref[0, pl.ds(k0, 2 * BLK), :], bias_ref[jnp.minimum(n, 1)],
                                sink)
        o_ref[0, pl.ds(q0, BLK), :] = (acc * (1.0 / l)).astype(o_ref.dtype)
        return carry
    lax.fori_loop(0, n_blocks, body, 0, unroll=_LOOP_UNROLL)


def _swa_attention(q, k, v, sinks):
    batch, seq, width = q.shape
    qblk = pl.BlockSpec((1, seq, LANES), lambda b, c, s: (b, 0, c))
    kblk = pl.BlockSpec((1, seq, LANES), lambda b, c, s: (b, 0, c // 2))
    grid_spec = pltpu.PrefetchScalarGridSpec(
        num_scalar_prefetch=1, grid=(batch, width // LANES),
        in_specs=[qblk, kblk, kblk,
                  pl.BlockSpec((2, 2 * BLK, 2 * BLK), lambda b, c, s: (0, 0, 0))],
        out_specs=qblk)
    return pl.pallas_call(
        functools.partial(_swa_kernel, n_blocks=seq // BLK), grid_spec=grid_spec,
        out_shape=jax.ShapeDtypeStruct((batch, seq, width), BF16),
        compiler_params=_params(("arbitrary", "arbitrary")),
        name="swa_sink_attn",
    )(sinks, q, k, v, _band_bias(SWA_WINDOW - 1))


ROW_TM = 512
RUN_ALIGN = SUBLANES
SORT_ROWS = 2 * ROW_TM + N_EXPERTS * RUN_ALIGN
_R_G1, _R_G2, _R_POS1, _R_POS2 = range(4)
_T_COUNT, _T_OFF, _T_BEFORE = range(3)


def _route(logits, count_ref):
    tm = logits.shape[0]
    lane = lax.broadcasted_iota(jnp.int32, logits.shape, 1)
    v1 = jnp.max(logits, axis=1, keepdims=True)
    i1 = jnp.min(jnp.where(logits == v1, lane, LANES), axis=1, keepdims=True)
    rest = jnp.where(lane == i1, NEG_BIG, logits)
    v2 = jnp.max(rest, axis=1, keepdims=True)
    i2 = jnp.min(jnp.where(rest == v2, lane, LANES), axis=1, keepdims=True)
    e2 = jnp.exp(v2 - v1)
    g1 = 1.0 / (1.0 + e2)
    g2 = e2 / (1.0 + e2)
    sel = jnp.where((lane == i1) | (lane == i2), 1.0, 0.0)
    r = lax.broadcasted_iota(jnp.int32, (tm, tm), 0)
    c = lax.broadcasted_iota(jnp.int32, (tm, tm), 1)
    tri = jnp.where(c < r, 1.0, 0.0).astype(BF16)
    local = jnp.dot(tri, sel.astype(BF16), preferred_element_type=F32)
    count = jnp.sum(sel, axis=0, keepdims=True)
    padded = jnp.floor((count + (RUN_ALIGN - 1)) * (1.0 / RUN_ALIGN)) * RUN_ALIGN
    run_end = jnp.broadcast_to(padded, (SUBLANES, LANES))
    for shift in (1, 2, 4):
        run_end = run_end + pltpu.roll(run_end, shift, 1)
    offset = run_end[0:1] - padded
    pos = local + offset
    pos1 = jnp.sum(jnp.where(lane == i1, pos, 0.0), axis=1, keepdims=True)
    pos2 = jnp.sum(jnp.where(lane == i2, pos, 0.0), axis=1, keepdims=True)
    before = count_ref[...]
    count_ref[...] = before + padded
    rec = jnp.zeros(logits.shape, F32)
    for ln, col in ((_R_G1, g1), (_R_G2, g2), (_R_POS1, pos1), (_R_POS2, pos2)):
        rec = jnp.where(lane == ln, col, rec)
    row = lax.broadcasted_iota(jnp.int32, (SUBLANES, LANES), 0)
    tile = jnp.zeros((SUBLANES, LANES), F32)
    for rw, val in ((_T_COUNT, padded), (_T_OFF, offset), (_T_BEFORE, before)):
        tile = jnp.where(row == rw, val, tile)
    return rec, tile


def _outproj_kernel(*refs, with_router):
    oa_ref, ob_ref, x_ref, ga_ref, gb_ref, w_ref, gf_ref = refs[:7]
    rest = refs[7:]
    if with_router:
        r_ref, xo_ref, ho_ref, route_ref, tile_ref, count_ref = rest
    else:
        cast_in_ref, xo_ref, ho_ref, cast_out_ref = rest
        cast_out_ref[0] = cast_in_ref[0, 0].astype(BF16)
    na = _rms(oa_ref[...].astype(F32), ga_ref[...]).astype(BF16)
    nb = _rms(ob_ref[...].astype(F32), gb_ref[...]).astype(BF16)
    xn = (x_ref[...] + jnp.dot(na, w_ref[:WIDTH_A, :], preferred_element_type=F32)
          + jnp.dot(nb, w_ref[WIDTH_A:, :], preferred_element_type=F32))
    xo_ref[...] = xn
    hf = _rms(xn, gf_ref[...])
    h_hi = hf.astype(BF16)
    ho_ref[...] = h_hi
    if with_router:
        @pl.when(pl.program_id(0) == 0)
        def _():
            count_ref[...] = jnp.zeros_like(count_ref)
        h_lo = (hf - h_hi.astype(F32)).astype(BF16)
        t = jnp.dot(h_hi, r_ref[...], preferred_element_type=F32)
        logits = (t[:, :LANES] + t[:, LANES:]
                  + jnp.dot(h_lo, r_ref[:, :LANES], preferred_element_type=F32))
        lane = lax.broadcasted_iota(jnp.int32, logits.shape, 1)
        logits = jnp.where(lane < N_EXPERTS, logits, NEG_BIG)
        route_ref[...], tile_ref[...] = _route(logits, count_ref)


def _outproj(oa, ob, x2, ga, gb, w_out, gf, router_pad=None, cast=None):
    tokens = x2.shape[0]
    with_router = router_pad is not None
    assert with_router != (cast is not None)
    tm = ROW_TM if with_router else PROJ_TM
    row = lambda w: pl.BlockSpec((tm, w), lambda i: (i, 0))
    const = lambda a, b: pl.BlockSpec((a, b), lambda i: (0, 0))
    in_specs = [row(WIDTH_A), row(WIDTH_B), row(D_MODEL), const(1, WIDTH_A), const(1, WIDTH_B),
                const(D_MODEL, D_MODEL), const(1, D_MODEL)]
    out_specs = [row(D_MODEL), row(D_MODEL)]
    out_shape = [jax.ShapeDtypeStruct((tokens, D_MODEL), F32),
                 jax.ShapeDtypeStruct((tokens, D_MODEL), BF16)]
    args = [oa, ob, x2, ga, gb, w_out, gf]
    if with_router:
        in_specs.append(const(D_MODEL, 2 * LANES))
        out_specs += [row(LANES), pl.BlockSpec((SUBLANES, LANES), lambda i: (i, 0)),
                      const(1, LANES)]
        out_shape += [jax.ShapeDtypeStruct((tokens, LANES), F32),
                      jax.ShapeDtypeStruct((tokens // tm * SUBLANES, LANES), F32),
                      jax.ShapeDtypeStruct((1, LANES), F32)]
        args.append(router_pad)
    else:
        cast_in, cast_out, cast_shape = _cast_specs(*cast, tokens // tm)
        in_specs.append(cast_in)
        out_specs.append(cast_out)
        out_shape.append(cast_shape)
        args.append(cast[0])
    return pl.pallas_call(
        functools.partial(_outproj_kernel, with_router=with_router),
        grid=(tokens // tm,), in_specs=in_specs, out_specs=out_specs, out_shape=out_shape,
        compiler_params=_params(("arbitrary",)),
        name="mix_outproj_router" if with_router else "mix_outproj",
    )(*args)


def _silu(x):
    return x * (1.0 / (1.0 + jnp.exp(-x)))


FFN_TM = 1024
FFN_TF = 1792
FFN_TC = 512


def _swiglu_step(h, wg_ref, wu_ref, wd_ref, acc_ref):
    for c0 in range(0, FFN_TF, FFN_TC):
        c1 = min(c0 + FFN_TC, FFN_TF)
        g = jnp.dot(h, wg_ref[:, c0:c1], preferred_element_type=F32)
        u = jnp.dot(h, wu_ref[:, c0:c1], preferred_element_type=F32)
        acc_ref[...] += jnp.dot((_silu(g) * u).astype(BF16), wd_ref[c0:c1, :],
                                preferred_element_type=F32)


def _ffn_kernel(*refs, final):
    h_ref, wg_ref, wu_ref, wd_ref, x_ref = refs[:5]
    if final:
        gfin_ref, o_ref = refs[5:]
    else:
        (o_ref,) = refs[5:]
    f = pl.program_id(1)

    @pl.when(f == 0)
    def _():
        o_ref[...] = x_ref[...]

    _swiglu_step(h_ref[...], wg_ref.at[0], wu_ref.at[0], wd_ref.at[0], o_ref)
    if final:
        @pl.when(f == pl.num_programs(1) - 1)
        def _():
            o_ref[...] = _rms(o_ref[...], gfin_ref[...])


def _ffn(h, x2, wg, wu, wd, layer, final_gain=None):
    tokens = x2.shape[0]
    tm, tf = FFN_TM, FFN_TF
    final = final_gain is not None
    tok = lambda w: pl.BlockSpec((tm, w), lambda i, f: (i, 0))
    in_specs = [tok(D_MODEL),
                pl.BlockSpec((1, D_MODEL, tf), lambda i, f: (layer, 0, f)),
                pl.BlockSpec((1, D_MODEL, tf), lambda i, f: (layer, 0, f)),
                pl.BlockSpec((1, tf, D_MODEL), lambda i, f: (layer, f, 0)),
                tok(D_MODEL)]
    args = [h, wg, wu, wd, x2]
    if final:
        in_specs.append(pl.BlockSpec((1, D_MODEL), lambda i, f: (0, 0)))
        args.append(final_gain)
    return pl.pallas_call(
        functools.partial(_ffn_kernel, final=final),
        grid=(tokens // tm, D_FF // tf), in_specs=in_specs, out_specs=tok(D_MODEL),
        out_shape=jax.ShapeDtypeStruct((tokens, D_MODEL), F32),
        compiler_params=_params(("arbitrary", "arbitrary")),
        name="dense_swiglu",
    )(*args)


def _run_copies(act, meta_ref, make_copy):
    for e in range(N_EXPERTS):
        n = pl.multiple_of(meta_ref[0, 0, N_EXPERTS + e], RUN_ALIGN)

        @pl.when(n > 0)
        def _(e=e, n=n):
            off = pl.multiple_of(meta_ref[0, 0, e], RUN_ALIGN)
            dst = pl.multiple_of(meta_ref[0, 0, 2 * N_EXPERTS + e], RUN_ALIGN)
            getattr(make_copy(off, n, dst), act)()


def _sort_matrix(route, n_rows):
    rec_t = route.T
    p1 = rec_t[_R_POS1:_R_POS1 + 1, :].astype(jnp.int32)
    p2 = rec_t[_R_POS2:_R_POS2 + 1, :].astype(jnp.int32)
    ip = lax.broadcasted_iota(jnp.int32, (n_rows, route.shape[0]), 0)
    return jnp.where((ip == p1) | (ip == p2), 1.0, 0.0).astype(BF16)


def _dispatch_kernel(tend_ref, meta_ref, prev_ref, h_ref, route_ref, out_ref, sort_ref, zero_ref,
                     sems, *, max_tail):
    n_tiles = out_ref.shape[0] // FFN_TM

    @pl.when(pl.program_id(0) == 0)
    def _():
        zero_ref[...] = jnp.zeros_like(zero_ref)

        def zero_tile(t):
            return pltpu.make_async_copy(zero_ref, out_ref.at[pl.ds(t * FFN_TM, FFN_TM)],
                                         sems.at[0])

        def has_tiles(e):
            return tend_ref[e] > (tend_ref[e - 1] if e else 0)

        for act in ("start", "wait"):
            for e in range(N_EXPERTS):
                @pl.when(has_tiles(e))
                def _(e=e, act=act):
                    getattr(zero_tile(tend_ref[e] - 1), act)()
            for k in range(max_tail):
                t = tend_ref[N_EXPERTS - 1] + k

                @pl.when(t < n_tiles)
                def _(t=t, act=act):
                    getattr(zero_tile(t), act)()

    i = pl.program_id(0)
    slot = i & 1

    def run_copy(s):
        return lambda off, n, dst: pltpu.make_async_copy(
            sort_ref.at[s, pl.ds(off, n)], out_ref.at[pl.ds(dst, n)], sems.at[s])

    sort_ref[slot] = jnp.dot(_sort_matrix(route_ref[...], SORT_ROWS), h_ref[...],
                             preferred_element_type=F32)
    _run_copies("start", meta_ref, run_copy(slot))

    @pl.when(i > 0)
    def _():
        _run_copies("wait", prev_ref, run_copy(1 - slot))

    @pl.when(i == pl.num_programs(0) - 1)
    def _():
        _run_copies("wait", meta_ref, run_copy(slot))


def _dispatch(h, route, meta, tile_end, n_slots):
    tokens = h.shape[0]
    tm = ROW_TM
    grid_spec = pltpu.PrefetchScalarGridSpec(
        num_scalar_prefetch=1, grid=(tokens // tm,),
        in_specs=[pl.BlockSpec((1, 1, 32), lambda i, te: (i, 0, 0), memory_space=pltpu.SMEM),
                  pl.BlockSpec((1, 1, 32), lambda i, te: (jnp.maximum(i - 1, 0), 0, 0),
                               memory_space=pltpu.SMEM),
                  pl.BlockSpec((tm, D_MODEL), lambda i, te: (i, 0)),
                  pl.BlockSpec((tm, LANES), lambda i, te: (i, 0))],
        out_specs=pl.BlockSpec(memory_space=pl.ANY),
        scratch_shapes=[pltpu.VMEM((2, SORT_ROWS, D_MODEL), F32),
                        pltpu.VMEM((FFN_TM, D_MODEL), F32), pltpu.SemaphoreType.DMA((2,))])
    max_tail = n_slots // FFN_TM - 2 * tokens // FFN_TM
    return pl.pallas_call(
        functools.partial(_dispatch_kernel, max_tail=max_tail), grid_spec=grid_spec,
        out_shape=jax.ShapeDtypeStruct((n_slots, D_MODEL), F32),
        compiler_params=_params(("arbitrary",)),
        name="moe_dispatch",
    )(tile_end, meta, meta, h, route)


def _grouped_kernel(texp_ref, nt_ref, h_ref, wg_ref, wu_ref, wd_ref, y_ref):
    del texp_ref

    @pl.when(pl.program_id(1) == 0)
    def _():
        y_ref[...] = jnp.zeros_like(y_ref)

    @pl.when(pl.program_id(0) < nt_ref[0])
    def _():
        _swiglu_step(h_ref[...].astype(BF16), wg_ref.at[0], wu_ref.at[0], wd_ref.at[0], y_ref)


def _grouped_swiglu(hs, tile_expert, n_tiles_used, wg, wu, wd):
    n_slots = hs.shape[0]
    tm, tf = FFN_TM, FFN_TF
    nf = D_FF // tf
    last = lambda i, nt: jnp.minimum(i, nt[0] - 1)
    fcol = lambda i, f, nt: jnp.where(i < nt[0], f, nf - 1)
    grid_spec = pltpu.PrefetchScalarGridSpec(
        num_scalar_prefetch=2, grid=(n_slots // tm, nf),
        in_specs=[pl.BlockSpec((tm, D_MODEL), lambda i, f, te, nt: (last(i, nt), 0)),
                  pl.BlockSpec((1, D_MODEL, tf), lambda i, f, te, nt: (te[i], 0, fcol(i, f, nt))),
                  pl.BlockSpec((1, D_MODEL, tf), lambda i, f, te, nt: (te[i], 0, fcol(i, f, nt))),
                  pl.BlockSpec((1, tf, D_MODEL), lambda i, f, te, nt: (te[i], fcol(i, f, nt), 0))],
        out_specs=pl.BlockSpec((tm, D_MODEL), lambda i, f, te, nt: (i, 0)))
    return pl.pallas_call(
        _grouped_kernel, grid_spec=grid_spec,
        out_shape=jax.ShapeDtypeStruct((n_slots, D_MODEL), F32),
        compiler_params=_params(("arbitrary", "arbitrary")),
        name="moe_grouped_swiglu",
    )(tile_expert, n_tiles_used, hs, wg, wu, wd)


def _combine_kernel(*refs, final):
    meta_ref, next_ref, route_ref, x_ref, y_ref = refs[:5]
    if final:
        gfin_ref, o_ref, sort_ref, sems = refs[5:]
    else:
        o_ref, sort_ref, sems = refs[5:]
    tm = x_ref.shape[0]
    i = pl.program_id(0)
    slot = i & 1

    def fetch(act, m_ref, s):
        if act == "start":
            sort_ref[s, pl.ds(2 * tm, SORT_ROWS - 2 * tm), :] = jnp.zeros(
                (SORT_ROWS - 2 * tm, D_MODEL), F32)
        _run_copies(act, m_ref, lambda off, n, dst: pltpu.make_async_copy(
            y_ref.at[pl.ds(dst, n)], sort_ref.at[s, pl.ds(off, n)], sems.at[s]))

    @pl.when(i == 0)
    def _():
        fetch("start", meta_ref, slot)

    @pl.when(i + 1 < pl.num_programs(0))
    def _():
        fetch("start", next_ref, 1 - slot)

    fetch("wait", meta_ref, slot)
    ys = sort_ref[slot].astype(BF16)
    route = route_ref[...]
    ip = lax.broadcasted_iota(jnp.int32, (tm, SORT_ROWS), 1)

    def pick(col):
        pos = route[:, col:col + 1].astype(jnp.int32)
        return jnp.dot(jnp.where(ip == pos, 1.0, 0.0).astype(BF16), ys,
                       preferred_element_type=F32)

    out = (x_ref[...] + route[:, _R_G1:_R_G1 + 1] * pick(_R_POS1)
           + route[:, _R_G2:_R_G2 + 1] * pick(_R_POS2))
    o_ref[...] = _rms(out, gfin_ref[...]) if final else out


def _combine(y, route, meta, x2, final_gain=None):
    tokens = x2.shape[0]
    tm = ROW_TM
    final = final_gain is not None
    n_steps = tokens // tm
    in_specs = [pl.BlockSpec((1, 1, 32), lambda i: (i, 0, 0), memory_space=pltpu.SMEM),
                pl.BlockSpec((1, 1, 32), lambda i: (jnp.minimum(i + 1, n_steps - 1), 0, 0),
                             memory_space=pltpu.SMEM),
                pl.BlockSpec((tm, LANES), lambda i: (i, 0)),
                pl.BlockSpec((tm, D_MODEL), lambda i: (i, 0)),
                pl.BlockSpec(memory_space=pl.ANY)]
    args = [meta, meta, route, x2, y]
    if final:
        in_specs.append(pl.BlockSpec((1, D_MODEL), lambda i: (0, 0)))
        args.append(final_gain)
    return pl.pallas_call(
        functools.partial(_combine_kernel, final=final),
        grid=(tokens // tm,), in_specs=in_specs,
        out_specs=pl.BlockSpec((tm, D_MODEL), lambda i: (i, 0)),
        out_shape=jax.ShapeDtypeStruct((tokens, D_MODEL), F32),
        scratch_shapes=[pltpu.VMEM((2, SORT_ROWS, D_MODEL), F32),
                        pltpu.SemaphoreType.DMA((2,))],
        compiler_params=_params(("arbitrary",)),
        name="moe_combine",
    )(*args)


def _routing_tables(tile_rec, counts, tokens):
    tm = FFN_TM
    n_tok_tiles = tokens // ROW_TM
    max_rows = 2 * tokens + n_tok_tiles * N_EXPERTS * (RUN_ALIGN - 1)
    n_tiles = (max_rows + N_EXPERTS * (tm - 1)) // tm
    cnt = counts[0, :N_EXPERTS].astype(jnp.int32)
    tiles = (cnt + tm - 1) // tm
    tile_end = jnp.cumsum(tiles)
    start = (tile_end - tiles) * tm
    rec = tile_rec.reshape(n_tok_tiles, SUBLANES, LANES)[:, :, :N_EXPERTS].astype(jnp.int32)
    meta = jnp.concatenate([rec[:, _T_OFF], rec[:, _T_COUNT], start[None, :] + rec[:, _T_BEFORE],
                            jnp.zeros((n_tok_tiles, 32 - 3 * N_EXPERTS), jnp.int32)], axis=1)
    tile_ids = jnp.arange(n_tiles, dtype=jnp.int32)[:, None]
    tile_expert = jnp.minimum(jnp.sum(tile_ids >= tile_end[None, :], axis=1),
                              N_EXPERTS - 1).astype(jnp.int32)
    return (meta.reshape(n_tok_tiles, 1, 32), tile_expert, tile_end.astype(jnp.int32),
            n_tiles * tm)


def _moe(h, route, tile_rec, counts, x2, wg, wu, wd, final_gain=None):
    meta, tile_expert, tile_end, n_slots = _routing_tables(tile_rec, counts, x2.shape[0])
    hs = _dispatch(h, route, meta, tile_end, n_slots)
    y = _grouped_swiglu(hs, tile_expert, tile_end[-1:], wg, wu, wd)
    return _combine(y, route, meta, x2, final_gain)


def _widen_w_in(w):
    a = 3 * WIDTH_A + WIDTH_B
    kb = w[:, a:a + KV_WIDTH_B]
    vb = w[:, a + KV_WIDTH_B:]
    dup = lambda t: jnp.concatenate(
        [t[:, h * HEAD_DIM:(h + 1) * HEAD_DIM] for h in range(N_KV_B) for _ in range(2)], axis=1)
    return jnp.concatenate([w[:, :a], dup(kb), dup(vb)], axis=1).astype(BF16)


def kernel(x, positions, attn_norm, w_in, mix_norm_a, mix_norm_b, sinks, w_out, ffn_norm,
           dense_w_gate, dense_w_up, dense_w_down, router, moe_w_gate, moe_w_up, moe_w_down,
           final_norm):
    batch, seq, _ = x.shape
    depth = w_in.shape[0]
    tokens = batch * seq
    cos, sina, sinb = _rope_tables(positions)
    x2 = x.reshape(tokens, D_MODEL)
    seq3 = lambda t: t.reshape(batch, seq, t.shape[-1])
    dense_w = [w.astype(BF16) for w in (dense_w_gate, dense_w_up, dense_w_down)]
    n_moe = moe_w_gate.shape[0]
    for i in range(depth):
        jm = min(i // 2, n_moe - 1)
        qa, ka, va, qb, kb, vb, w_cast = _inproj(
            x2, attn_norm[i][None, :], _widen_w_in(w_in[i]), cos, sina, sinb, seq,
            moe_w_gate if i % 2 == 0 else moe_w_down, jm)
        if i % 2 == 0:
            wg_b = w_cast
        else:
            wd_b = w_cast
        oa = _dilated_attention(seq3(qa), seq3(ka), seq3(va)).reshape(tokens, WIDTH_A)
        ob = _swa_attention(seq3(qb), seq3(kb), seq3(vb), sinks[i]).reshape(tokens, WIDTH_B)
        j = i // 2
        last = i == depth - 1
        norms = (mix_norm_a[i][None, :], mix_norm_b[i][None, :], w_out[i].astype(BF16),
                 ffn_norm[i][None, :])
        if i % 2 == 0:
            x2, h, wu_b = _outproj(oa, ob, x2, *norms, cast=(moe_w_up, jm))
            x2 = _ffn(h, x2, *dense_w, j, final_gain=final_norm[None, :] if last else None)
        else:
            r_f32 = jnp.pad(router[j], ((0, 0), (0, LANES - N_EXPERTS)))
            r_hi = r_f32.astype(BF16)
            r_lo = (r_f32 - r_hi.astype(F32)).astype(BF16)
            router_pad = jnp.concatenate([r_hi, r_lo], axis=1)
            x2, h, route, tile_rec, counts = _outproj(oa, ob, x2, *norms, router_pad)
            x2 = _moe(h, route, tile_rec, counts, x2, wg_b, wu_b, wd_b,
                      final_gain=final_norm[None, :] if last else None)
    return x2.reshape(batch, seq, D_MODEL)
```

```python
import functools

import numpy as np
import jax
import jax.numpy as jnp
from jax import lax
from jax.experimental import pallas as pl
from jax.experimental.pallas import tpu as pltpu

D_MODEL = 1024
HEAD_DIM = 64
N_HEADS_A = 8
N_HEADS_B = 8
N_KV_B = 2
DILATED_BRANCHES = ((128, 1), (512, 4), (2048, 16))
SWA_WINDOW = 128
BLK = 128
ROPE_THETA = 10000.0
D_FF = 3584
N_EXPERTS = 8
RMS_EPS = 1e-5
WIDTH_A = N_HEADS_A * HEAD_DIM
WIDTH_B = N_HEADS_B * HEAD_DIM
KV_WIDTH_B = N_KV_B * HEAD_DIM

LANES = 128
SUBLANES = 8
NEG_BIG = -1e30
VMEM_LIMIT = 56 * 1024 * 1024
PROJ_TM = 1024

F32 = jnp.float32
BF16 = jnp.bfloat16


def _params(sem):
    return pltpu.CompilerParams(dimension_semantics=sem, vmem_limit_bytes=VMEM_LIMIT)


def _rms(x, g):
    return x * lax.rsqrt(jnp.mean(x * x, axis=-1, keepdims=True) + RMS_EPS) * g


def _rope_table_kernel(pos_ref, inv_ref, cos_ref, sina_ref, sinb_ref):
    ang = pos_ref[...].astype(F32) * inv_ref[...]
    c = jnp.cos(ang)
    s = jnp.sin(ang)
    lane = lax.broadcasted_iota(jnp.int32, ang.shape, 1)
    upper = (lane & (HEAD_DIM - 1)) >= HEAD_DIM // 2
    cos_ref[...] = c
    sina_ref[...] = jnp.where(upper, s, 0.0)
    sinb_ref[...] = jnp.where(upper, 0.0, -s)


def _rope_tables(positions):
    seq = positions.shape[0]
    inv = ROPE_THETA ** (-jnp.arange(0, HEAD_DIM, 2, dtype=F32) / HEAD_DIM)
    inv = jnp.tile(inv, LANES // (HEAD_DIM // 2))[None, :]
    rows = 1024
    tab = jax.ShapeDtypeStruct((seq, LANES), F32)
    return pl.pallas_call(
        _rope_table_kernel,
        grid=(seq // rows,),
        in_specs=[pl.BlockSpec((rows, 1), lambda i: (i, 0)),
                  pl.BlockSpec((1, LANES), lambda i: (0, 0))],
        out_specs=[pl.BlockSpec((rows, LANES), lambda i: (i, 0))] * 3,
        out_shape=[tab, tab, tab],
        compiler_params=_params(("arbitrary",)),
        name="rope_tables",
    )(positions.reshape(seq, 1), inv)


LOG2E = 1.4426950408889634
_QSCALE = HEAD_DIM ** -0.5 * LOG2E
_SEGS = (("qa", WIDTH_A, True, _QSCALE), ("ka", WIDTH_A, True, 1.0),
         ("va", WIDTH_A, False, 1.0), ("qb", WIDTH_B, True, _QSCALE),
         ("kb", 2 * KV_WIDTH_B, True, 1.0), ("vb", 2 * KV_WIDTH_B, False, 1.0))


def _cast_specs(w, layer, n_steps):
    _, n_exp, rows, cols = w.shape
    per = n_steps // n_exp
    rb = rows // per
    assert n_steps % n_exp == 0 and rows % per == 0 and rb % 16 == 0
    return (pl.BlockSpec((1, 1, rb, cols), lambda i: (layer, i // per, i % per, 0)),
            pl.BlockSpec((1, rb, cols), lambda i: (i // per, i % per, 0)),
            jax.ShapeDtypeStruct((n_exp, rows, cols), BF16))


def _inproj_kernel(x_ref, g_ref, w_ref, cos_ref, sina_ref, sinb_ref, cast_in_ref, *out_refs):
    out_refs[-1][0] = cast_in_ref[0, 0].astype(BF16)
    h = _rms(x_ref[...], g_ref[...]).astype(BF16)
    cos = cos_ref[...]
    sina = sina_ref[...]
    sinb = sinb_ref[...]
    col = 0
    for (_, width, roped, scale), o_ref in zip(_SEGS, out_refs):
        t = jnp.dot(h, w_ref[:, col:col + width], preferred_element_type=F32)
        col += width
        for j in range(width // LANES):
            tj = t[:, j * LANES:(j + 1) * LANES]
            if roped:
                tj = (tj * cos + pltpu.roll(tj, HEAD_DIM // 2, 1) * sina
                      + pltpu.roll(tj, LANES - HEAD_DIM // 2, 1) * sinb)
                if scale != 1.0:
                    tj = tj * scale
            o_ref[:, j * LANES:(j + 1) * LANES] = tj.astype(o_ref.dtype)


def _inproj(x2, g, w_ext, cos, sina, sinb, seq, cast_w, cast_layer):
    tokens = x2.shape[0]
    tm = PROJ_TM
    spt = seq // tm
    widths = [s[1] for s in _SEGS]
    total = sum(widths)
    cast_in, cast_out, cast_shape = _cast_specs(cast_w, cast_layer, tokens // tm)
    return pl.pallas_call(
        _inproj_kernel,
        grid=(tokens // tm,),
        in_specs=[pl.BlockSpec((tm, D_MODEL), lambda i: (i, 0)),
                  pl.BlockSpec((1, D_MODEL), lambda i: (0, 0)),
                  pl.BlockSpec((D_MODEL, total), lambda i: (0, 0)),
                  pl.BlockSpec((tm, LANES), lambda i: (i % spt, 0)),
                  pl.BlockSpec((tm, LANES), lambda i: (i % spt, 0)),
                  pl.BlockSpec((tm, LANES), lambda i: (i % spt, 0)),
                  cast_in],
        out_specs=[pl.BlockSpec((tm, w), lambda i: (i, 0)) for w in widths] + [cast_out],
        out_shape=[jax.ShapeDtypeStruct((tokens, w), BF16) for w in widths] + [cast_shape],
        compiler_params=_params(("arbitrary",)),
        name="inproj_rope",
    )(x2, g, w_ext, cos, sina, sinb, cast_w)


def _band_bias(max_dist):
    p = np.arange(BLK)[:, None]
    c = np.arange(2 * BLK)[None, :]
    first = c <= p
    dist = BLK + p - c
    main = (dist >= 0) & (dist <= max_dist)
    bias = np.where(np.stack([first, main]), 0.0, NEG_BIG).astype(np.float32)
    return jnp.asarray(np.concatenate([bias, bias], axis=1))


def _low_lanes():
    return lax.broadcasted_iota(jnp.int32, (BLK, LANES), 1) < HEAD_DIM


def _band_block(q, kb, vb, bias, sink=None):
    low = _low_lanes()
    zero = jnp.zeros_like(q)
    qq = jnp.concatenate([jnp.where(low, q, zero), jnp.where(low, zero, q)], axis=0)
    s = lax.dot_general(qq, kb, (((1,), (1,)), ((), ())), preferred_element_type=F32) + bias
    m = jnp.max(s, axis=1, keepdims=True)
    if sink is not None:
        m = jnp.maximum(m, sink)
    p = jnp.exp2(s - m).astype(BF16)
    ol = jnp.dot(p, jnp.concatenate([vb, jnp.ones_like(vb)], axis=1),
                 preferred_element_type=F32)
    acc = jnp.where(low, ol[:BLK, :LANES], ol[BLK:, :LANES])
    l = jnp.where(low, ol[:BLK, LANES:], ol[BLK:, LANES:])
    if sink is not None:
        ls = jnp.exp2(sink - m)
        l = l + jnp.where(low, ls[:BLK], ls[BLK:])
    return acc, jnp.where(low, m[:BLK], m[BLK:]), l


def _finish(acc, m, l):
    return acc * (1.0 / l), m + jnp.log2(l)


def _merge(o_s, lse_s, acc, m, l, want_lse=True):
    d = lse_s - m
    t = jnp.exp2(-jnp.abs(d))
    ws = jnp.where(d >= 0, 1.0, t)
    wb = jnp.where(d >= 0, t, 1.0)
    den = ws + wb * l
    o = (ws * o_s + wb * acc) * (1.0 / den)
    return (o, jnp.maximum(lse_s, m) + jnp.log2(den)) if want_lse else (o, None)


_CLASSES = 4
_LOOP_UNROLL = 32


def _dilated_kernel(q_ref, k_ref, v_ref, bias_ref, o_ref,
                    q4, k4, v4, tmp, so4, sl4, il_o, il_l, *, seq):
    cls = seq // _CLASSES
    nb4 = cls // BLK
    nb16 = seq // 16 // BLK
    sh4, sh16 = nb4.bit_length() - 1, nb16.bit_length() - 1
    chunk = 512
    shc = (cls // chunk).bit_length() - 1

    for src, dst in ((q_ref, q4), (k_ref, k4), (v_ref, v4)):
        def widen(i, c, src=src):
            r0 = pl.multiple_of(i * chunk, chunk)
            tmp[pl.ds(r0, chunk), :] = src[0, pl.ds(r0, chunk), :].astype(F32)
            return c
        lax.fori_loop(0, seq // chunk, widen, 0)

        def regroup(i, c, dst=dst):
            r = lax.shift_right_logical(i, shc)
            j = i & (cls // chunk - 1)
            d0 = pl.multiple_of(r * cls + j * chunk, chunk)
            dst[pl.ds(d0, chunk), :] = tmp[pl.ds(r + _CLASSES * chunk * j, chunk,
                                                 stride=_CLASSES), :]
            return c
        lax.fori_loop(0, seq // chunk, regroup, 0)

    def dil4(idx, c):
        r = lax.shift_right_logical(idx, sh4)
        n = idx & (nb4 - 1)
        q0 = pl.multiple_of(r * cls + n * BLK, BLK)
        k0 = pl.multiple_of(r * cls + jnp.maximum(n - 1, 0) * BLK, BLK)
        o, lse = _finish(*_band_block(q4[pl.ds(q0, BLK), :].astype(BF16),
                                      k4[pl.ds(k0, 2 * BLK), :].astype(BF16),
                                      v4[pl.ds(k0, 2 * BLK), :].astype(BF16),
                                      bias_ref[jnp.minimum(n, 1)]))
        so4[pl.ds(q0, BLK), :] = o
        sl4[pl.ds(q0, BLK), :] = lse
        return c
    lax.fori_loop(0, _CLASSES * nb4, dil4, 0, unroll=_LOOP_UNROLL)

    def dil16(idx, c):
        n = idx & (nb16 - 1)
        sub = lax.shift_right_logical(idx, sh16) & (_CLASSES - 1)
        r = lax.shift_right_logical(idx, sh16 + 2)
        q0 = r * cls + sub + _CLASSES * BLK * n
        k0 = r * cls + sub + _CLASSES * BLK * jnp.maximum(n - 1, 0)
        qs = pl.ds(q0, BLK, stride=_CLASSES)
        ks = pl.ds(k0, 2 * BLK, stride=_CLASSES)
        o_s, l_s = so4[qs, :], sl4[qs, :]
        o, lse = _merge(o_s, l_s, *_band_block(
            q4[qs, :].astype(BF16), k4[ks, :].astype(BF16), v4[ks, :].astype(BF16),
            bias_ref[jnp.minimum(n, 1)]))
        so4[qs, :] = o
        sl4[qs, :] = lse
        return c
    lax.fori_loop(0, 16 * nb16, dil16, 0, unroll=_LOOP_UNROLL)

    piece = BLK // _CLASSES

    def dil1(n, c):
        q0 = pl.multiple_of(n * BLK, BLK)
        k0 = pl.multiple_of(jnp.maximum(n - 1, 0) * BLK, BLK)
        for r in range(_CLASSES):
            s0 = pl.multiple_of(r * cls + n * piece, piece)
            il_o[pl.ds(r, piece, stride=_CLASSES), :] = so4[pl.ds(s0, piece), :]
            il_l[pl.ds(r, piece, stride=_CLASSES), :] = sl4[pl.ds(s0, piece), :]
        o_s, l_s = il_o[...], il_l[...]
        o, _ = _merge(o_s, l_s, *_band_block(
            q_ref[0, pl.ds(q0, BLK), :], k_ref[0, pl.ds(k0, 2 * BLK), :],
            v_ref[0, pl.ds(k0, 2 * BLK), :], bias_ref[jnp.minimum(n, 1)]), want_lse=False)
        o_ref[0, pl.ds(q0, BLK), :] = o.astype(o_ref.dtype)
        return c
    lax.fori_loop(0, seq // BLK, dil1, 0, unroll=_LOOP_UNROLL)


def _dilated_attention(q, k, v):
    assert DILATED_BRANCHES == ((128, 1), (512, 4), (2048, 16))
    batch, seq, width = q.shape
    assert seq % (16 * BLK) == 0 and (seq & (seq - 1)) == 0
    blk = pl.BlockSpec((1, seq, LANES), lambda b, c: (b, 0, c))
    rows = pltpu.VMEM((seq, LANES), F32)
    small = pltpu.VMEM((BLK, LANES), F32)
    return pl.pallas_call(
        functools.partial(_dilated_kernel, seq=seq),
        grid=(batch, width // LANES),
        in_specs=[blk, blk, blk, pl.BlockSpec((2, 2 * BLK, 2 * BLK), lambda b, c: (0, 0, 0))],
        out_specs=blk,
        out_shape=jax.ShapeDtypeStruct((batch, seq, width), BF16),
        scratch_shapes=[rows] * 6 + [small] * 2,
        compiler_params=_params(("arbitrary", "arbitrary")),
        name="dilated_attn",
    )(q, k, v, _band_bias(BLK))


def _swa_kernel(sink_ref, q_ref, k_ref, v_ref, bias_ref, o_ref, *, n_blocks):
    c = pl.program_id(1)
    row = lax.broadcasted_iota(jnp.int32, (2 * BLK, 1), 0)
    sink = jnp.where(row < BLK, sink_ref[2 * c], sink_ref[2 * c + 1]) * LOG2E

    def body(n, carry):
        q0 = pl.multiple_of(n * BLK, BLK)
        k0 = pl.multiple_of(jnp.maximum(n - 1, 0) * BLK, BLK)
        acc, _, l = _band_block(q_ref[0, pl.ds(q0, BLK), :], k_ref[0, pl.ds(k0, 2 * BLK), :],
                                v_ref[0, pl.ds(k0, 2 * BLK), :], bias_ref[jnp.minimum(n, 1)],
                                sink)
        o_ref[0, pl.ds(q0, BLK), :] = (acc * (1.0 / l)).astype(o_ref.dtype)
        return carry
    lax.fori_loop(0, n_blocks, body, 0, unroll=_LOOP_UNROLL)


def _swa_attention(q, k, v, sinks):
    batch, seq, width = q.shape
    qblk = pl.BlockSpec((1, seq, LANES), lambda b, c, s: (b, 0, c))
    kblk = pl.BlockSpec((1, seq, LANES), lambda b, c, s: (b, 0, c // 2))
    grid_spec = pltpu.PrefetchScalarGridSpec(
        num_scalar_prefetch=1, grid=(batch, width // LANES),
        in_specs=[qblk, kblk, kblk,
                  pl.BlockSpec((2, 2 * BLK, 2 * BLK), lambda b, c, s: (0, 0, 0))],
        out_specs=qblk)
    return pl.pallas_call(
        functools.partial(_swa_kernel, n_blocks=seq // BLK), grid_spec=grid_spec,
        out_shape=jax.ShapeDtypeStruct((batch, seq, width), BF16),
        compiler_params=_params(("arbitrary", "arbitrary")),
        name="swa_sink_attn",
    )(sinks, q, k, v, _band_bias(SWA_WINDOW - 1))


ROW_TM = 512
RUN_ALIGN = SUBLANES
SORT_ROWS = 2 * ROW_TM + N_EXPERTS * RUN_ALIGN
_R_G1, _R_G2, _R_POS1, _R_POS2 = range(4)
_T_COUNT, _T_OFF, _T_BEFORE = range(3)


def _route(logits, count_ref):
    tm = logits.shape[0]
    lane = lax.broadcasted_iota(jnp.int32, logits.shape, 1)
    v1 = jnp.max(logits, axis=1, keepdims=True)
    i1 = jnp.min(jnp.where(logits == v1, lane, LANES), axis=1, keepdims=True)
    rest = jnp.where(lane == i1, NEG_BIG, logits)
    v2 = jnp.max(rest, axis=1, keepdims=True)
    i2 = jnp.min(jnp.where(rest == v2, lane, LANES), axis=1, keepdims=True)
    e2 = jnp.exp(v2 - v1)
    g1 = 1.0 / (1.0 + e2)
    g2 = e2 / (1.0 + e2)
    sel = jnp.where((lane == i1) | (lane == i2), 1.0, 0.0)
    r = lax.broadcasted_iota(jnp.int32, (tm, tm), 0)
    c = lax.broadcasted_iota(jnp.int32, (tm, tm), 1)
    tri = jnp.where(c < r, 1.0, 0.0).astype(BF16)
    local = jnp.dot(tri, sel.astype(BF16), preferred_element_type=F32)
    count = jnp.sum(sel, axis=0, keepdims=True)
    padded = jnp.floor((count + (RUN_ALIGN - 1)) * (1.0 / RUN_ALIGN)) * RUN_ALIGN
    run_end = jnp.broadcast_to(padded, (SUBLANES, LANES))
    for shift in (1, 2, 4):
        run_end = run_end + pltpu.roll(run_end, shift, 1)
    offset = run_end[0:1] - padded
    pos = local + offset
    pos1 = jnp.sum(jnp.where(lane == i1, pos, 0.0), axis=1, keepdims=True)
    pos2 = jnp.sum(jnp.where(lane == i2, pos, 0.0), axis=1, keepdims=True)
    before = count_ref[...]
    count_ref[...] = before + padded
    rec = jnp.zeros(logits.shape, F32)
    for ln, col in ((_R_G1, g1), (_R_G2, g2), (_R_POS1, pos1), (_R_POS2, pos2)):
        rec = jnp.where(lane == ln, col, rec)
    row = lax.broadcasted_iota(jnp.int32, (SUBLANES, LANES), 0)
    tile = jnp.zeros((SUBLANES, LANES), F32)
    for rw, val in ((_T_COUNT, padded), (_T_OFF, offset), (_T_BEFORE, before)):
        tile = jnp.where(row == rw, val, tile)
    return rec, tile


def _outproj_kernel(*refs, with_router):
    oa_ref, ob_ref, x_ref, ga_ref, gb_ref, w_ref, gf_ref = refs[:7]
    rest = refs[7:]
    if with_router:
        r_ref, xo_ref, ho_ref, route_ref, tile_ref, count_ref = rest
    else:
        cast_in_ref, xo_ref, ho_ref, cast_out_ref = rest
        cast_out_ref[0] = cast_in_ref[0, 0].astype(BF16)
    na = _rms(oa_ref[...].astype(F32), ga_ref[...]).astype(BF16)
    nb = _rms(ob_ref[...].astype(F32), gb_ref[...]).astype(BF16)
    xn = (x_ref[...] + jnp.dot(na, w_ref[:WIDTH_A, :], preferred_element_type=F32)
          + jnp.dot(nb, w_ref[WIDTH_A:, :], preferred_element_type=F32))
    xo_ref[...] = xn
    hf = _rms(xn, gf_ref[...])
    h_hi = hf.astype(BF16)
    ho_ref[...] = h_hi
    if with_router:
        @pl.when(pl.program_id(0) == 0)
        def _():
            count_ref[...] = jnp.zeros_like(count_ref)
        h_lo = (hf - h_hi.astype(F32)).astype(BF16)
        t = jnp.dot(h_hi, r_ref[...], preferred_element_type=F32)
        logits = (t[:, :LANES] + t[:, LANES:]
                  + jnp.dot(h_lo, r_ref[:, :LANES], preferred_element_type=F32))
        lane = lax.broadcasted_iota(jnp.int32, logits.shape, 1)
        logits = jnp.where(lane < N_EXPERTS, logits, NEG_BIG)
        route_ref[...], tile_ref[...] = _route(logits, count_ref)


def _outproj(oa, ob, x2, ga, gb, w_out, gf, router_pad=None, cast=None):
    tokens = x2.shape[0]
    with_router = router_pad is not None
    assert with_router != (cast is not None)
    tm = ROW_TM if with_router else PROJ_TM
    row = lambda w: pl.BlockSpec((tm, w), lambda i: (i, 0))
    const = lambda a, b: pl.BlockSpec((a, b), lambda i: (0, 0))
    in_specs = [row(WIDTH_A), row(WIDTH_B), row(D_MODEL), const(1, WIDTH_A), const(1, WIDTH_B),
                const(D_MODEL, D_MODEL), const(1, D_MODEL)]
    out_specs = [row(D_MODEL), row(D_MODEL)]
    out_shape = [jax.ShapeDtypeStruct((tokens, D_MODEL), F32),
                 jax.ShapeDtypeStruct((tokens, D_MODEL), BF16)]
    args = [oa, ob, x2, ga, gb, w_out, gf]
    if with_router:
        in_specs.append(const(D_MODEL, 2 * LANES))
        out_specs += [row(LANES), pl.BlockSpec((SUBLANES, LANES), lambda i: (i, 0)),
                      const(1, LANES)]
        out_shape += [jax.ShapeDtypeStruct((tokens, LANES), F32),
                      jax.ShapeDtypeStruct((tokens // tm * SUBLANES, LANES), F32),
                      jax.ShapeDtypeStruct((1, LANES), F32)]
        args.append(router_pad)
    else:
        cast_in, cast_out, cast_shape = _cast_specs(*cast, tokens // tm)
        in_specs.append(cast_in)
        out_specs.append(cast_out)
        out_shape.append(cast_shape)
        args.append(cast[0])
    return pl.pallas_call(
        functools.partial(_outproj_kernel, with_router=with_router),
        grid=(tokens // tm,), in_specs=in_specs, out_specs=out_specs, out_shape=out_shape,
        compiler_params=_params(("arbitrary",)),
        name="mix_outproj_router" if with_router else "mix_outproj",
    )(*args)


def _silu(x):
    return x * (1.0 / (1.0 + jnp.exp(-x)))


FFN_TM = 1024
GROUP_TM = 768
FFN_TF = 1792
FFN_TC = 512


def _swiglu_step(h, wg_ref, wu_ref, wd_ref, acc_ref):
    for c0 in range(0, FFN_TF, FFN_TC):
        c1 = min(c0 + FFN_TC, FFN_TF)
        g = jnp.dot(h, wg_ref[:, c0:c1], preferred_element_type=F32)
        u = jnp.dot(h, wu_ref[:, c0:c1], preferred_element_type=F32)
        acc_ref[...] += jnp.dot((_silu(g) * u).astype(BF16), wd_ref[c0:c1, :],
                                preferred_element_type=F32)


def _ffn_kernel(*refs, final):
    h_ref, wg_ref, wu_ref, wd_ref, x_ref = refs[:5]
    if final:
        gfin_ref, o_ref = refs[5:]
    else:
        (o_ref,) = refs[5:]
    f = pl.program_id(1)

    @pl.when(f == 0)
    def _():
        o_ref[...] = x_ref[...]

    _swiglu_step(h_ref[...], wg_ref.at[0], wu_ref.at[0], wd_ref.at[0], o_ref)
    if final:
        @pl.when(f == pl.num_programs(1) - 1)
        def _():
            o_ref[...] = _rms(o_ref[...], gfin_ref[...])


def _ffn(h, x2, wg, wu, wd, layer, final_gain=None):
    tokens = x2.shape[0]
    tm, tf = FFN_TM, FFN_TF
    final = final_gain is not None
    tok = lambda w: pl.BlockSpec((tm, w), lambda i, f: (i, 0))
    in_specs = [tok(D_MODEL),
                pl.BlockSpec((1, D_MODEL, tf), lambda i, f: (layer, 0, f)),
                pl.BlockSpec((1, D_MODEL, tf), lambda i, f: (layer, 0, f)),
                pl.BlockSpec((1, tf, D_MODEL), lambda i, f: (layer, f, 0)),
                tok(D_MODEL)]
    args = [h, wg, wu, wd, x2]
    if final:
        in_specs.append(pl.BlockSpec((1, D_MODEL), lambda i, f: (0, 0)))
        args.append(final_gain)
    return pl.pallas_call(
        functools.partial(_ffn_kernel, final=final),
        grid=(tokens // tm, D_FF // tf), in_specs=in_specs, out_specs=tok(D_MODEL),
        out_shape=jax.ShapeDtypeStruct((tokens, D_MODEL), F32),
        compiler_params=_params(("arbitrary", "arbitrary")),
        name="dense_swiglu",
    )(*args)


def _run_copies(act, meta_ref, make_copy):
    for e in range(N_EXPERTS):
        n = pl.multiple_of(meta_ref[0, 0, N_EXPERTS + e], RUN_ALIGN)

        @pl.when(n > 0)
        def _(e=e, n=n):
            off = pl.multiple_of(meta_ref[0, 0, e], RUN_ALIGN)
            dst = pl.multiple_of(meta_ref[0, 0, 2 * N_EXPERTS + e], RUN_ALIGN)
            getattr(make_copy(off, n, dst), act)()


def _sort_matrix(route, n_rows):
    rec_t = route.T
    p1 = rec_t[_R_POS1:_R_POS1 + 1, :].astype(jnp.int32)
    p2 = rec_t[_R_POS2:_R_POS2 + 1, :].astype(jnp.int32)
    ip = lax.broadcasted_iota(jnp.int32, (n_rows, route.shape[0]), 0)
    return jnp.where((ip == p1) | (ip == p2), 1.0, 0.0).astype(BF16)


def _dispatch_kernel(tend_ref, meta_ref, prev_ref, h_ref, route_ref, out_ref, sort_ref, zero_ref,
                     sems, *, max_tail):
    n_tiles = out_ref.shape[0] // GROUP_TM

    @pl.when(pl.program_id(0) == 0)
    def _():
        zero_ref[...] = jnp.zeros_like(zero_ref)

        def zero_tile(t):
            return pltpu.make_async_copy(zero_ref, out_ref.at[pl.ds(t * GROUP_TM, GROUP_TM)],
                                         sems.at[0])

        def has_tiles(e):
            return tend_ref[e] > (tend_ref[e - 1] if e else 0)

        for act in ("start", "wait"):
            for e in range(N_EXPERTS):
                @pl.when(has_tiles(e))
                def _(e=e, act=act):
                    getattr(zero_tile(tend_ref[e] - 1), act)()
            for k in range(max_tail):
                t = tend_ref[N_EXPERTS - 1] + k

                @pl.when(t < n_tiles)
                def _(t=t, act=act):
                    getattr(zero_tile(t), act)()

    i = pl.program_id(0)
    slot = i & 1

    def run_copy(s):
        return lambda off, n, dst: pltpu.make_async_copy(
            sort_ref.at[s, pl.ds(off, n)], out_ref.at[pl.ds(dst, n)], sems.at[s])

    sort_ref[slot] = jnp.dot(_sort_matrix(route_ref[...], SORT_ROWS), h_ref[...],
                             preferred_element_type=F32)
    _run_copies("start", meta_ref, run_copy(slot))

    @pl.when(i > 0)
    def _():
        _run_copies("wait", prev_ref, run_copy(1 - slot))

    @pl.when(i == pl.num_programs(0) - 1)
    def _():
        _run_copies("wait", meta_ref, run_copy(slot))


def _dispatch(h, route, meta, tile_end, n_slots):
    tokens = h.shape[0]
    tm = ROW_TM
    grid_spec = pltpu.PrefetchScalarGridSpec(
        num_scalar_prefetch=1, grid=(tokens // tm,),
        in_specs=[pl.BlockSpec((1, 1, 32), lambda i, te: (i, 0, 0), memory_space=pltpu.SMEM),
                  pl.BlockSpec((1, 1, 32), lambda i, te: (jnp.maximum(i - 1, 0), 0, 0),
                               memory_space=pltpu.SMEM),
                  pl.BlockSpec((tm, D_MODEL), lambda i, te: (i, 0)),
                  pl.BlockSpec((tm, LANES), lambda i, te: (i, 0))],
        out_specs=pl.BlockSpec(memory_space=pl.ANY),
        scratch_shapes=[pltpu.VMEM((2, SORT_ROWS, D_MODEL), F32),
                        pltpu.VMEM((GROUP_TM, D_MODEL), F32), pltpu.SemaphoreType.DMA((2,))])
    max_tail = n_slots // GROUP_TM - -(-2 * tokens // GROUP_TM)
    return pl.pallas_call(
        functools.partial(_dispatch_kernel, max_tail=max_tail), grid_spec=grid_spec,
        out_shape=jax.ShapeDtypeStruct((n_slots, D_MODEL), F32),
        compiler_params=_params(("arbitrary",)),
        name="moe_dispatch",
    )(tile_end, meta, meta, h, route)


def _grouped_kernel(texp_ref, nt_ref, h_ref, wg_ref, wu_ref, wd_ref, y_ref):
    del texp_ref

    @pl.when(pl.program_id(1) == 0)
    def _():
        y_ref[...] = jnp.zeros_like(y_ref)

    @pl.when(pl.program_id(0) < nt_ref[0])
    def _():
        _swiglu_step(h_ref[...].astype(BF16), wg_ref.at[0], wu_ref.at[0], wd_ref.at[0], y_ref)


def _grouped_swiglu(hs, tile_expert, n_tiles_used, wg, wu, wd):
    n_slots = hs.shape[0]
    tm, tf = GROUP_TM, FFN_TF
    nf = D_FF // tf
    last = lambda i, nt: jnp.minimum(i, nt[0] - 1)
    fcol = lambda i, f, nt: jnp.where(i < nt[0], f, nf - 1)
    grid_spec = pltpu.PrefetchScalarGridSpec(
        num_scalar_prefetch=2, grid=(n_slots // tm, nf),
        in_specs=[pl.BlockSpec((tm, D_MODEL), lambda i, f, te, nt: (last(i, nt), 0)),
                  pl.BlockSpec((1, D_MODEL, tf), lambda i, f, te, nt: (te[i], 0, fcol(i, f, nt))),
                  pl.BlockSpec((1, D_MODEL, tf), lambda i, f, te, nt: (te[i], 0, fcol(i, f, nt))),
                  pl.BlockSpec((1, tf, D_MODEL), lambda i, f, te, nt: (te[i], fcol(i, f, nt), 0))],
        out_specs=pl.BlockSpec((tm, D_MODEL), lambda i, f, te, nt: (i, 0)))
    return pl.pallas_call(
        _grouped_kernel, grid_spec=grid_spec,
        out_shape=jax.ShapeDtypeStruct((n_slots, D_MODEL), F32),
        compiler_params=_params(("arbitrary", "arbitrary")),
        name="moe_grouped_swiglu",
    )(tile_expert, n_tiles_used, hs, wg, wu, wd)


def _combine_kernel(*refs, final):
    meta_ref, next_ref, route_ref, x_ref, y_ref = refs[:5]
    if final:
        gfin_ref, o_ref, sort_ref, sems = refs[5:]
    else:
        o_ref, sort_ref, sems = refs[5:]
    tm = x_ref.shape[0]
    i = pl.program_id(0)
    slot = i & 1

    def fetch(act, m_ref, s):
        if act == "start":
            sort_ref[s, pl.ds(2 * tm, SORT_ROWS - 2 * tm), :] = jnp.zeros(
                (SORT_ROWS - 2 * tm, D_MODEL), F32)
        _run_copies(act, m_ref, lambda off, n, dst: pltpu.make_async_copy(
            y_ref.at[pl.ds(dst, n)], sort_ref.at[s, pl.ds(off, n)], sems.at[s]))

    @pl.when(i == 0)
    def _():
        fetch("start", meta_ref, slot)

    @pl.when(i + 1 < pl.num_programs(0))
    def _():
        fetch("start", next_ref, 1 - slot)

    fetch("wait", meta_ref, slot)
    ys = sort_ref[slot].astype(BF16)
    route = route_ref[...]
    ip = lax.broadcasted_iota(jnp.int32, (tm, SORT_ROWS), 1)

    def pick(col):
        pos = route[:, col:col + 1].astype(jnp.int32)
        return jnp.dot(jnp.where(ip == pos, 1.0, 0.0).astype(BF16), ys,
                       preferred_element_type=F32)

    out = (x_ref[...] + route[:, _R_G1:_R_G1 + 1] * pick(_R_POS1)
           + route[:, _R_G2:_R_G2 + 1] * pick(_R_POS2))
    o_ref[...] = _rms(out, gfin_ref[...]) if final else out


def _combine(y, route, meta, x2, final_gain=None):
    tokens = x2.shape[0]
    tm = ROW_TM
    final = final_gain is not None
    n_steps = tokens // tm
    in_specs = [pl.BlockSpec((1, 1, 32), lambda i: (i, 0, 0), memory_space=pltpu.SMEM),
                pl.BlockSpec((1, 1, 32), lambda i: (jnp.minimum(i + 1, n_steps - 1), 0, 0),
                             memory_space=pltpu.SMEM),
                pl.BlockSpec((tm, LANES), lambda i: (i, 0)),
                pl.BlockSpec((tm, D_MODEL), lambda i: (i, 0)),
                pl.BlockSpec(memory_space=pl.ANY)]
    args = [meta, meta, route, x2, y]
    if final:
        in_specs.append(pl.BlockSpec((1, D_MODEL), lambda i: (0, 0)))
        args.append(final_gain)
    return pl.pallas_call(
        functools.partial(_combine_kernel, final=final),
        grid=(tokens // tm,), in_specs=in_specs,
        out_specs=pl.BlockSpec((tm, D_MODEL), lambda i: (i, 0)),
        out_shape=jax.ShapeDtypeStruct((tokens, D_MODEL), F32),
        scratch_shapes=[pltpu.VMEM((2, SORT_ROWS, D_MODEL), F32),
                        pltpu.SemaphoreType.DMA((2,))],
        compiler_params=_params(("arbitrary",)),
        name="moe_combine",
    )(*args)


def _routing_tables(tile_rec, counts, tokens):
    tm = GROUP_TM
    n_tok_tiles = tokens // ROW_TM
    max_rows = 2 * tokens + n_tok_tiles * N_EXPERTS * (RUN_ALIGN - 1)
    n_tiles = (max_rows + N_EXPERTS * (tm - 1)) // tm
    cnt = counts[0, :N_EXPERTS].astype(jnp.int32)
    tiles = (cnt + tm - 1) // tm
    tile_end = jnp.cumsum(tiles)
    start = (tile_end - tiles) * tm
    rec = tile_rec.reshape(n_tok_tiles, SUBLANES, LANES)[:, :, :N_EXPERTS].astype(jnp.int32)
    meta = jnp.concatenate([rec[:, _T_OFF], rec[:, _T_COUNT], start[None, :] + rec[:, _T_BEFORE],
                            jnp.zeros((n_tok_tiles, 32 - 3 * N_EXPERTS), jnp.int32)], axis=1)
    tile_ids = jnp.arange(n_tiles, dtype=jnp.int32)[:, None]
    tile_expert = jnp.minimum(jnp.sum(tile_ids >= tile_end[None, :], axis=1),
                              N_EXPERTS - 1).astype(jnp.int32)
    return (meta.reshape(n_tok_tiles, 1, 32), tile_expert, tile_end.astype(jnp.int32),
            n_tiles * tm)


def _moe(h, route, tile_rec, counts, x2, wg, wu, wd, final_gain=None):
    meta, tile_expert, tile_end, n_slots = _routing_tables(tile_rec, counts, x2.shape[0])
    hs = _dispatch(h, route, meta, tile_end, n_slots)
    y = _grouped_swiglu(hs, tile_expert, tile_end[-1:], wg, wu, wd)
    return _combine(y, route, meta, x2, final_gain)


def _widen_w_in(w):
    a = 3 * WIDTH_A + WIDTH_B
    kb = w[:, a:a + KV_WIDTH_B]
    vb = w[:, a + KV_WIDTH_B:]
    dup = lambda t: jnp.concatenate(
        [t[:, h * HEAD_DIM:(h + 1) * HEAD_DIM] for h in range(N_KV_B) for _ in range(2)], axis=1)
    return jnp.concatenate([w[:, :a], dup(kb), dup(vb)], axis=1).astype(BF16)


def kernel(x, positions, attn_norm, w_in, mix_norm_a, mix_norm_b, sinks, w_out, ffn_norm,
           dense_w_gate, dense_w_up, dense_w_down, router, moe_w_gate, moe_w_up, moe_w_down,
           final_norm):
    batch, seq, _ = x.shape
    depth = w_in.shape[0]
    tokens = batch * seq
    cos, sina, sinb = _rope_tables(positions)
    x2 = x.reshape(tokens, D_MODEL)
    seq3 = lambda t: t.reshape(batch, seq, t.shape[-1])
    dense_w = [w.astype(BF16) for w in (dense_w_gate, dense_w_up, dense_w_down)]
    n_moe = moe_w_gate.shape[0]
    for i in range(depth):
        jm = min(i // 2, n_moe - 1)
        qa, ka, va, qb, kb, vb, w_cast = _inproj(
            x2, attn_norm[i][None, :], _widen_w_in(w_in[i]), cos, sina, sinb, seq,
            moe_w_gate if i % 2 == 0 else moe_w_down, jm)
        if i % 2 == 0:
            wg_b = w_cast
        else:
            wd_b = w_cast
        oa = _dilated_attention(seq3(qa), seq3(ka), seq3(va)).reshape(tokens, WIDTH_A)
        ob = _swa_attention(seq3(qb), seq3(kb), seq3(vb), sinks[i]).reshape(tokens, WIDTH_B)
        j = i // 2
        last = i == depth - 1
        norms = (mix_norm_a[i][None, :], mix_norm_b[i][None, :], w_out[i].astype(BF16),
                 ffn_norm[i][None, :])
        if i % 2 == 0:
            x2, h, wu_b = _outproj(oa, ob, x2, *norms, cast=(moe_w_up, jm))
            x2 = _ffn(h, x2, *dense_w, j, final_gain=final_norm[None, :] if last else None)
        else:
            r_f32 = jnp.pad(router[j], ((0, 0), (0, LANES - N_EXPERTS)))
            r_hi = r_f32.astype(BF16)
            r_lo = (r_f32 - r_hi.astype(F32)).astype(BF16)
            router_pad = jnp.concatenate([r_hi, r_lo], axis=1)
            x2, h, route, tile_rec, counts = _outproj(oa, ob, x2, *norms, router_pad)
            x2 = _moe(h, route, tile_rec, counts, x2, wg_b, wu_b, wd_b,
                      final_gain=final_norm[None, :] if last else None)
    return x2.reshape(batch, seq, D_MODEL)
```

```python
import functools

import numpy as np
import jax
import jax.numpy as jnp
from jax import lax
from jax.experimental import pallas as pl
from jax.experimental.pallas import tpu as pltpu

D_MODEL = 1024
HEAD_DIM = 64
N_HEADS_A = 8
N_HEADS_B = 8
N_KV_B = 2
DILATED_BRANCHES = ((128, 1), (512, 4), (2048, 16))
SWA_WINDOW = 128
BLK = 128
ROPE_THETA = 10000.0
D_FF = 3584
N_EXPERTS = 8
RMS_EPS = 1e-5
WIDTH_A = N_HEADS_A * HEAD_DIM
WIDTH_B = N_HEADS_B * HEAD_DIM
KV_WIDTH_B = N_KV_B * HEAD_DIM

LANES = 128
SUBLANES = 8
NEG_BIG = -1e30
VMEM_LIMIT = 56 * 1024 * 1024
PROJ_TM = 1024

F32 = jnp.float32
BF16 = jnp.bfloat16


def _params(sem):
    return pltpu.CompilerParams(dimension_semantics=sem, vmem_limit_bytes=VMEM_LIMIT)


def _rms(x, g):
    return x * lax.rsqrt(jnp.mean(x * x, axis=-1, keepdims=True) + RMS_EPS) * g


def _rope_table_kernel(pos_ref, inv_ref, cos_ref, sina_ref, sinb_ref):
    ang = pos_ref[...].astype(F32) * inv_ref[...]
    c = jnp.cos(ang)
    s = jnp.sin(ang)
    lane = lax.broadcasted_iota(jnp.int32, ang.shape, 1)
    upper = (lane & (HEAD_DIM - 1)) >= HEAD_DIM // 2
    cos_ref[...] = c
    sina_ref[...] = jnp.where(upper, s, 0.0)
    sinb_ref[...] = jnp.where(upper, 0.0, -s)


def _rope_tables(positions):
    seq = positions.shape[0]
    inv = ROPE_THETA ** (-jnp.arange(0, HEAD_DIM, 2, dtype=F32) / HEAD_DIM)
    inv = jnp.tile(inv, LANES // (HEAD_DIM // 2))[None, :]
    rows = 1024
    tab = jax.ShapeDtypeStruct((seq, LANES), F32)
    return pl.pallas_call(
        _rope_table_kernel,
        grid=(seq // rows,),
        in_specs=[pl.BlockSpec((rows, 1), lambda i: (i, 0)),
                  pl.BlockSpec((1, LANES), lambda i: (0, 0))],
        out_specs=[pl.BlockSpec((rows, LANES), lambda i: (i, 0))] * 3,
        out_shape=[tab, tab, tab],
        compiler_params=_params(("arbitrary",)),
        name="rope_tables",
    )(positions.reshape(seq, 1), inv)


LOG2E = 1.4426950408889634
_QSCALE = HEAD_DIM ** -0.5 * LOG2E
_SEGS = (("qa", WIDTH_A, True, _QSCALE), ("ka", WIDTH_A, True, 1.0),
         ("va", WIDTH_A, False, 1.0), ("qb", WIDTH_B, True, _QSCALE),
         ("kb", 2 * KV_WIDTH_B, True, 1.0), ("vb", 2 * KV_WIDTH_B, False, 1.0))


def _cast_specs(w, layer, n_steps):
    _, n_exp, rows, cols = w.shape
    per = n_steps // n_exp
    rb = rows // per
    assert n_steps % n_exp == 0 and rows % per == 0 and rb % 16 == 0
    return (pl.BlockSpec((1, 1, rb, cols), lambda i: (layer, i // per, i % per, 0)),
            pl.BlockSpec((1, rb, cols), lambda i: (i // per, i % per, 0)),
            jax.ShapeDtypeStruct((n_exp, rows, cols), BF16))


def _inproj_kernel(x_ref, g_ref, w_ref, cos_ref, sina_ref, sinb_ref, cast_in_ref, *out_refs):
    out_refs[-1][0] = cast_in_ref[0, 0].astype(BF16)
    h = _rms(x_ref[...], g_ref[...]).astype(BF16)
    cos = cos_ref[...]
    sina = sina_ref[...]
    sinb = sinb_ref[...]
    col = 0
    for (_, width, roped, scale), o_ref in zip(_SEGS, out_refs):
        t = jnp.dot(h, w_ref[:, col:col + width], preferred_element_type=F32)
        col += width
        for j in range(width // LANES):
            tj = t[:, j * LANES:(j + 1) * LANES]
            if roped:
                tj = (tj * cos + pltpu.roll(tj, HEAD_DIM // 2, 1) * sina
                      + pltpu.roll(tj, LANES - HEAD_DIM // 2, 1) * sinb)
                if scale != 1.0:
                    tj = tj * scale
            o_ref[:, j * LANES:(j + 1) * LANES] = tj.astype(o_ref.dtype)


def _inproj(x2, g, w_ext, cos, sina, sinb, seq, cast_w, cast_layer):
    tokens = x2.shape[0]
    tm = PROJ_TM
    spt = seq // tm
    widths = [s[1] for s in _SEGS]
    total = sum(widths)
    cast_in, cast_out, cast_shape = _cast_specs(cast_w, cast_layer, tokens // tm)
    return pl.pallas_call(
        _inproj_kernel,
        grid=(tokens // tm,),
        in_specs=[pl.BlockSpec((tm, D_MODEL), lambda i: (i, 0)),
                  pl.BlockSpec((1, D_MODEL), lambda i: (0, 0)),
                  pl.BlockSpec((D_MODEL, total), lambda i: (0, 0)),
                  pl.BlockSpec((tm, LANES), lambda i: (i % spt, 0)),
                  pl.BlockSpec((tm, LANES), lambda i: (i % spt, 0)),
                  pl.BlockSpec((tm, LANES), lambda i: (i % spt, 0)),
                  cast_in],
        out_specs=[pl.BlockSpec((tm, w), lambda i: (i, 0)) for w in widths] + [cast_out],
        out_shape=[jax.ShapeDtypeStruct((tokens, w), BF16) for w in widths] + [cast_shape],
        compiler_params=_params(("arbitrary",)),
        name="inproj_rope",
    )(x2, g, w_ext, cos, sina, sinb, cast_w)


def _band_bias(max_dist):
    p = np.arange(BLK)[:, None]
    c = np.arange(2 * BLK)[None, :]
    first = c <= p
    dist = BLK + p - c
    main = (dist >= 0) & (dist <= max_dist)
    bias = np.where(np.stack([first, main]), 0.0, NEG_BIG).astype(np.float32)
    return jnp.asarray(np.concatenate([bias, bias], axis=1))


def _low_lanes():
    return lax.broadcasted_iota(jnp.int32, (BLK, LANES), 1) < HEAD_DIM


def _band_block(q, kb, vb, bias, sink=None):
    low = _low_lanes()
    zero = jnp.zeros_like(q)
    qq = jnp.concatenate([jnp.where(low, q, zero), jnp.where(low, zero, q)], axis=0)
    s = lax.dot_general(qq, kb, (((1,), (1,)), ((), ())), preferred_element_type=F32) + bias
    m = jnp.max(s, axis=1, keepdims=True)
    if sink is not None:
        m = jnp.maximum(m, sink)
    p = jnp.exp2(s - m).astype(BF16)
    ol = jnp.dot(p, jnp.concatenate([vb, jnp.ones_like(vb)], axis=1),
                 preferred_element_type=F32)
    acc = jnp.where(low, ol[:BLK, :LANES], ol[BLK:, :LANES])
    l = jnp.where(low, ol[:BLK, LANES:], ol[BLK:, LANES:])
    if sink is not None:
        ls = jnp.exp2(sink - m)
        l = l + jnp.where(low, ls[:BLK], ls[BLK:])
    return acc, jnp.where(low, m[:BLK], m[BLK:]), l


def _finish(acc, m, l):
    return acc * (1.0 / l), m + jnp.log2(l)


def _merge(o_s, lse_s, acc, m, l, want_lse=True):
    d = lse_s - m
    t = jnp.exp2(-jnp.abs(d))
    ws = jnp.where(d >= 0, 1.0, t)
    wb = jnp.where(d >= 0, t, 1.0)
    den = ws + wb * l
    o = (ws * o_s + wb * acc) * (1.0 / den)
    return (o, jnp.maximum(lse_s, m) + jnp.log2(den)) if want_lse else (o, None)


_CLASSES = 4
_LOOP_UNROLL = 32


def _dilated_kernel(q_ref, k_ref, v_ref, bias_ref, o_ref,
                    q4, k4, v4, tmp, so4, sl4, il_o, il_l, *, seq):
    cls = seq // _CLASSES
    nb4 = cls // BLK
    nb16 = seq // 16 // BLK
    sh4, sh16 = nb4.bit_length() - 1, nb16.bit_length() - 1
    chunk = 512
    shc = (cls // chunk).bit_length() - 1

    for src, dst in ((q_ref, q4), (k_ref, k4), (v_ref, v4)):
        def widen(i, c, src=src):
            r0 = pl.multiple_of(i * chunk, chunk)
            tmp[pl.ds(r0, chunk), :] = src[0, pl.ds(r0, chunk), :].astype(F32)
            return c
        lax.fori_loop(0, seq // chunk, widen, 0)

        def regroup(i, c, dst=dst):
            r = lax.shift_right_logical(i, shc)
            j = i & (cls // chunk - 1)
            d0 = pl.multiple_of(r * cls + j * chunk, chunk)
            dst[pl.ds(d0, chunk), :] = tmp[pl.ds(r + _CLASSES * chunk * j, chunk,
                                                 stride=_CLASSES), :]
            return c
        lax.fori_loop(0, seq // chunk, regroup, 0)

    def dil4(idx, c):
        r = lax.shift_right_logical(idx, sh4)
        n = idx & (nb4 - 1)
        q0 = pl.multiple_of(r * cls + n * BLK, BLK)
        k0 = pl.multiple_of(r * cls + jnp.maximum(n - 1, 0) * BLK, BLK)
        o, lse = _finish(*_band_block(q4[pl.ds(q0, BLK), :].astype(BF16),
                                      k4[pl.ds(k0, 2 * BLK), :].astype(BF16),
                                      v4[pl.ds(k0, 2 * BLK), :].astype(BF16),
                                      bias_ref[jnp.minimum(n, 1)]))
        so4[pl.ds(q0, BLK), :] = o
        sl4[pl.ds(q0, BLK), :] = lse
        return c
    lax.fori_loop(0, _CLASSES * nb4, dil4, 0, unroll=_LOOP_UNROLL)

    def dil16(idx, c):
        n = idx & (nb16 - 1)
        sub = lax.shift_right_logical(idx, sh16) & (_CLASSES - 1)
        r = lax.shift_right_logical(idx, sh16 + 2)
        q0 = r * cls + sub + _CLASSES * BLK * n
        k0 = r * cls + sub + _CLASSES * BLK * jnp.maximum(n - 1, 0)
        qs = pl.ds(q0, BLK, stride=_CLASSES)
        ks = pl.ds(k0, 2 * BLK, stride=_CLASSES)
        o_s, l_s = so4[qs, :], sl4[qs, :]
        o, lse = _merge(o_s, l_s, *_band_block(
            q4[qs, :].astype(BF16), k4[ks, :].astype(BF16), v4[ks, :].astype(BF16),
            bias_ref[jnp.minimum(n, 1)]))
        so4[qs, :] = o
        sl4[qs, :] = lse
        return c
    lax.fori_loop(0, 16 * nb16, dil16, 0, unroll=_LOOP_UNROLL)

    piece = BLK // _CLASSES

    def dil1(n, c):
        q0 = pl.multiple_of(n * BLK, BLK)
        k0 = pl.multiple_of(jnp.maximum(n - 1, 0) * BLK, BLK)
        for r in range(_CLASSES):
            s0 = pl.multiple_of(r * cls + n * piece, piece)
            il_o[pl.ds(r, piece, stride=_CLASSES), :] = so4[pl.ds(s0, piece), :]
            il_l[pl.ds(r, piece, stride=_CLASSES), :] = sl4[pl.ds(s0, piece), :]
        o_s, l_s = il_o[...], il_l[...]
        o, _ = _merge(o_s, l_s, *_band_block(
            q_ref[0, pl.ds(q0, BLK), :], k_ref[0, pl.ds(k0, 2 * BLK), :],
            v_ref[0, pl.ds(k0, 2 * BLK), :], bias_ref[jnp.minimum(n, 1)]), want_lse=False)
        o_ref[0, pl.ds(q0, BLK), :] = o.astype(o_ref.dtype)
        return c
    lax.fori_loop(0, seq // BLK, dil1, 0, unroll=_LOOP_UNROLL)


def _dilated_attention(q, k, v):
    assert DILATED_BRANCHES == ((128, 1), (512, 4), (2048, 16))
    batch, seq, width = q.shape
    assert seq % (16 * BLK) == 0 and (seq & (seq - 1)) == 0
    blk = pl.BlockSpec((1, seq, LANES), lambda b, c: (b, 0, c))
    rows = pltpu.VMEM((seq, LANES), F32)
    small = pltpu.VMEM((BLK, LANES), F32)
    return pl.pallas_call(
        functools.partial(_dilated_kernel, seq=seq),
        grid=(batch, width // LANES),
        in_specs=[blk, blk, blk, pl.BlockSpec((2, 2 * BLK, 2 * BLK), lambda b, c: (0, 0, 0))],
        out_specs=blk,
        out_shape=jax.ShapeDtypeStruct((batch, seq, width), BF16),
        scratch_shapes=[rows] * 6 + [small] * 2,
        compiler_params=_params(("arbitrary", "arbitrary")),
        name="dilated_attn",
    )(q, k, v, _band_bias(BLK))


def _swa_kernel(sink_ref, q_ref, k_ref, v_ref, bias_ref, o_ref, *, n_blocks):
    c = pl.program_id(1)
    row = lax.broadcasted_iota(jnp.int32, (2 * BLK, 1), 0)
    sink = jnp.where(row < BLK, sink_ref[2 * c], sink_ref[2 * c + 1]) * LOG2E

    def body(n, carry):
        q0 = pl.multiple_of(n * BLK, BLK)
        k0 = pl.multiple_of(jnp.maximum(n - 1, 0) * BLK, BLK)
        acc, _, l = _band_block(q_ref[0, pl.ds(q0, BLK), :], k_ref[0, pl.ds(k0, 2 * BLK), :],
                                v_ref[0, pl.ds(k0, 2 * BLK), :], bias_ref[jnp.minimum(n, 1)],
                                sink)
        o_ref[0, pl.ds(q0, BLK), :] = (acc * (1.0 / l)).astype(o_ref.dtype)
        return carry
    lax.fori_loop(0, n_blocks, body, 0, unroll=_LOOP_UNROLL)


def _swa_attention(q, k, v, sinks):
    batch, seq, width = q.shape
    qblk = pl.BlockSpec((1, seq, LANES), lambda b, c, s: (b, 0, c))
    kblk = pl.BlockSpec((1, seq, LANES), lambda b, c, s: (b, 0, c // 2))
    grid_spec = pltpu.PrefetchScalarGridSpec(
        num_scalar_prefetch=1, grid=(batch, width // LANES),
        in_specs=[qblk, kblk, kblk,
                  pl.BlockSpec((2, 2 * BLK, 2 * BLK), lambda b, c, s: (0, 0, 0))],
        out_specs=qblk)
    return pl.pallas_call(
        functools.partial(_swa_kernel, n_blocks=seq // BLK), grid_spec=grid_spec,
        out_shape=jax.ShapeDtypeStruct((batch, seq, width), BF16),
        compiler_params=_params(("arbitrary", "arbitrary")),
        name="swa_sink_attn",
    )(sinks, q, k, v, _band_bias(SWA_WINDOW - 1))


ROW_TM = 512
RUN_ALIGN = SUBLANES
SORT_ROWS = 2 * ROW_TM + N_EXPERTS * RUN_ALIGN
_R_G1, _R_G2, _R_POS1, _R_POS2 = range(4)
_T_COUNT, _T_OFF, _T_BEFORE = range(3)
META_WORDS = 32


def _route(logits, count_ref):
    tm = logits.shape[0]
    lane = lax.broadcasted_iota(jnp.int32, logits.shape, 1)
    v1 = jnp.max(logits, axis=1, keepdims=True)
    i1 = jnp.min(jnp.where(logits == v1, lane, LANES), axis=1, keepdims=True)
    rest = jnp.where(lane == i1, NEG_BIG, logits)
    v2 = jnp.max(rest, axis=1, keepdims=True)
    i2 = jnp.min(jnp.where(rest == v2, lane, LANES), axis=1, keepdims=True)
    e2 = jnp.exp(v2 - v1)
    g1 = 1.0 / (1.0 + e2)
    g2 = e2 / (1.0 + e2)
    sel = jnp.where((lane == i1) | (lane == i2), 1.0, 0.0)
    r = lax.broadcasted_iota(jnp.int32, (tm, tm), 0)
    c = lax.broadcasted_iota(jnp.int32, (tm, tm), 1)
    tri = jnp.where(c < r, 1.0, 0.0).astype(BF16)
    local = jnp.dot(tri, sel.astype(BF16), preferred_element_type=F32)
    count = jnp.sum(sel, axis=0, keepdims=True)
    padded = jnp.floor((count + (RUN_ALIGN - 1)) * (1.0 / RUN_ALIGN)) * RUN_ALIGN
    run_end = jnp.broadcast_to(padded, (SUBLANES, LANES))
    for shift in (1, 2, 4):
        run_end = run_end + pltpu.roll(run_end, shift, 1)
    offset = run_end[0:1] - padded
    pos = local + offset
    pos1 = jnp.sum(jnp.where(lane == i1, pos, 0.0), axis=1, keepdims=True)
    pos2 = jnp.sum(jnp.where(lane == i2, pos, 0.0), axis=1, keepdims=True)
    before = count_ref[...]
    count_ref[...] = before + padded
    rec = jnp.zeros(logits.shape, F32)
    for ln, col in ((_R_G1, g1), (_R_G2, g2), (_R_POS1, pos1), (_R_POS2, pos2)):
        rec = jnp.where(lane == ln, col, rec)
    row = lax.broadcasted_iota(jnp.int32, (SUBLANES, LANES), 0)
    tile = jnp.zeros((SUBLANES, LANES), F32)
    for rw, val in ((_T_COUNT, padded), (_T_OFF, offset), (_T_BEFORE, before)):
        tile = jnp.where(row == rw, val, tile)
    return rec, tile


def _outproj_kernel(*refs, with_router):
    oa_ref, ob_ref, x_ref, ga_ref, gb_ref, w_ref, gf_ref = refs[:7]
    rest = refs[7:]
    if with_router:
        r_ref, xo_ref, ho_ref, route_ref, tile_ref, count_ref = rest
    else:
        cast_in_ref, xo_ref, ho_ref, cast_out_ref = rest
        cast_out_ref[0] = cast_in_ref[0, 0].astype(BF16)
    na = _rms(oa_ref[...].astype(F32), ga_ref[...]).astype(BF16)
    nb = _rms(ob_ref[...].astype(F32), gb_ref[...]).astype(BF16)
    xn = (x_ref[...] + jnp.dot(na, w_ref[:WIDTH_A, :], preferred_element_type=F32)
          + jnp.dot(nb, w_ref[WIDTH_A:, :], preferred_element_type=F32))
    xo_ref[...] = xn
    hf = _rms(xn, gf_ref[...])
    h_hi = hf.astype(BF16)
    ho_ref[...] = h_hi
    if with_router:
        @pl.when(pl.program_id(0) == 0)
        def _():
            count_ref[...] = jnp.zeros_like(count_ref)
        h_lo = (hf - h_hi.astype(F32)).astype(BF16)
        t = jnp.dot(h_hi, r_ref[...], preferred_element_type=F32)
        logits = (t[:, :LANES] + t[:, LANES:]
                  + jnp.dot(h_lo, r_ref[:, :LANES], preferred_element_type=F32))
        lane = lax.broadcasted_iota(jnp.int32, logits.shape, 1)
        logits = jnp.where(lane < N_EXPERTS, logits, NEG_BIG)
        route_ref[...], tile_ref[...] = _route(logits, count_ref)


def _outproj(oa, ob, x2, ga, gb, w_out, gf, router_pad=None, cast=None):
    tokens = x2.shape[0]
    with_router = router_pad is not None
    assert with_router != (cast is not None)
    tm = ROW_TM if with_router else PROJ_TM
    row = lambda w: pl.BlockSpec((tm, w), lambda i: (i, 0))
    const = lambda a, b: pl.BlockSpec((a, b), lambda i: (0, 0))
    in_specs = [row(WIDTH_A), row(WIDTH_B), row(D_MODEL), const(1, WIDTH_A), const(1, WIDTH_B),
                const(D_MODEL, D_MODEL), const(1, D_MODEL)]
    out_specs = [row(D_MODEL), row(D_MODEL)]
    out_shape = [jax.ShapeDtypeStruct((tokens, D_MODEL), F32),
                 jax.ShapeDtypeStruct((tokens, D_MODEL), BF16)]
    args = [oa, ob, x2, ga, gb, w_out, gf]
    if with_router:
        in_specs.append(const(D_MODEL, 2 * LANES))
        out_specs += [row(LANES), pl.BlockSpec((SUBLANES, LANES), lambda i: (i, 0)),
                      const(1, LANES)]
        out_shape += [jax.ShapeDtypeStruct((tokens, LANES), F32),
                      jax.ShapeDtypeStruct((tokens // tm * SUBLANES, LANES), F32),
                      jax.ShapeDtypeStruct((1, LANES), F32)]
        args.append(router_pad)
    else:
        cast_in, cast_out, cast_shape = _cast_specs(*cast, tokens // tm)
        in_specs.append(cast_in)
        out_specs.append(cast_out)
        out_shape.append(cast_shape)
        args.append(cast[0])
    return pl.pallas_call(
        functools.partial(_outproj_kernel, with_router=with_router),
        grid=(tokens // tm,), in_specs=in_specs, out_specs=out_specs, out_shape=out_shape,
        compiler_params=_params(("arbitrary",)),
        name="mix_outproj_router" if with_router else "mix_outproj",
    )(*args)


def _silu(x):
    return x * (1.0 / (1.0 + jnp.exp(-x)))


FFN_TM = 1024
GROUP_TM = 768
FFN_TF = 1792
FFN_TC = 1024


def _swiglu_step(h, wg_ref, wu_ref, wd_ref, acc_ref):
    for c0 in range(0, FFN_TF, FFN_TC):
        c1 = min(c0 + FFN_TC, FFN_TF)
        g = jnp.dot(h, wg_ref[:, c0:c1], preferred_element_type=F32)
        u = jnp.dot(h, wu_ref[:, c0:c1], preferred_element_type=F32)
        acc_ref[...] += jnp.dot((_silu(g) * u).astype(BF16), wd_ref[c0:c1, :],
                                preferred_element_type=F32)


def _ffn_kernel(*refs, final):
    h_ref, wg_ref, wu_ref, wd_ref, x_ref = refs[:5]
    if final:
        gfin_ref, o_ref = refs[5:]
    else:
        (o_ref,) = refs[5:]
    f = pl.program_id(1)

    @pl.when(f == 0)
    def _():
        o_ref[...] = x_ref[...]

    _swiglu_step(h_ref[...], wg_ref.at[0], wu_ref.at[0], wd_ref.at[0], o_ref)
    if final:
        @pl.when(f == pl.num_programs(1) - 1)
        def _():
            o_ref[...] = _rms(o_ref[...], gfin_ref[...])


def _ffn(h, x2, wg, wu, wd, layer, final_gain=None):
    tokens = x2.shape[0]
    tm, tf = FFN_TM, FFN_TF
    final = final_gain is not None
    tok = lambda w: pl.BlockSpec((tm, w), lambda i, f: (i, 0))
    in_specs = [tok(D_MODEL),
                pl.BlockSpec((1, D_MODEL, tf), lambda i, f: (layer, 0, f)),
                pl.BlockSpec((1, D_MODEL, tf), lambda i, f: (layer, 0, f)),
                pl.BlockSpec((1, tf, D_MODEL), lambda i, f: (layer, f, 0)),
                tok(D_MODEL)]
    args = [h, wg, wu, wd, x2]
    if final:
        in_specs.append(pl.BlockSpec((1, D_MODEL), lambda i, f: (0, 0)))
        args.append(final_gain)
    return pl.pallas_call(
        functools.partial(_ffn_kernel, final=final),
        grid=(tokens // tm, D_FF // tf), in_specs=in_specs, out_specs=tok(D_MODEL),
        out_shape=jax.ShapeDtypeStruct((tokens, D_MODEL), F32),
        compiler_params=_params(("arbitrary", "arbitrary")),
        name="dense_swiglu",
    )(*args)


def _run_copies(act, meta_ref, make_copy):
    for e in range(N_EXPERTS):
        n = pl.multiple_of(meta_ref[0, 0, N_EXPERTS + e], RUN_ALIGN)

        @pl.when(n > 0)
        def _(e=e, n=n):
            off = pl.multiple_of(meta_ref[0, 0, e], RUN_ALIGN)
            dst = pl.multiple_of(meta_ref[0, 0, 2 * N_EXPERTS + e], RUN_ALIGN)
            getattr(make_copy(off, n, dst), act)()


def _sort_matrix(route, n_rows):
    rec_t = route.T
    p1 = rec_t[_R_POS1:_R_POS1 + 1, :].astype(jnp.int32)
    p2 = rec_t[_R_POS2:_R_POS2 + 1, :].astype(jnp.int32)
    ip = lax.broadcasted_iota(jnp.int32, (n_rows, route.shape[0]), 0)
    return jnp.where((ip == p1) | (ip == p2), 1.0, 0.0).astype(BF16)


def _dispatch_kernel(tend_ref, meta_ref, prev_ref, h_ref, route_ref, out_ref, sort_ref, zero_ref,
                     sems, *, max_tail):
    n_tiles = out_ref.shape[0] // GROUP_TM

    @pl.when(pl.program_id(0) == 0)
    def _():
        zero_ref[...] = jnp.zeros_like(zero_ref)

        def zero_tile(t):
            return pltpu.make_async_copy(zero_ref, out_ref.at[pl.ds(t * GROUP_TM, GROUP_TM)],
                                         sems.at[0])

        def has_tiles(e):
            return tend_ref[e] > (tend_ref[e - 1] if e else 0)

        for act in ("start", "wait"):
            for e in range(N_EXPERTS):
                @pl.when(has_tiles(e))
                def _(e=e, act=act):
                    getattr(zero_tile(tend_ref[e] - 1), act)()
            for k in range(max_tail):
                t = tend_ref[N_EXPERTS - 1] + k

                @pl.when(t < n_tiles)
                def _(t=t, act=act):
                    getattr(zero_tile(t), act)()

    i = pl.program_id(0)
    slot = i & 1

    def run_copy(s):
        return lambda off, n, dst: pltpu.make_async_copy(
            sort_ref.at[s, pl.ds(off, n)], out_ref.at[pl.ds(dst, n)], sems.at[s])

    sort_ref[slot] = jnp.dot(_sort_matrix(route_ref[...], SORT_ROWS), h_ref[...],
                             preferred_element_type=F32)
    _run_copies("start", meta_ref, run_copy(slot))

    @pl.when(i > 0)
    def _():
        _run_copies("wait", prev_ref, run_copy(1 - slot))

    @pl.when(i == pl.num_programs(0) - 1)
    def _():
        _run_copies("wait", meta_ref, run_copy(slot))


def _dispatch(h, route, meta, tile_end, n_slots):
    tokens = h.shape[0]
    tm = ROW_TM
    grid_spec = pltpu.PrefetchScalarGridSpec(
        num_scalar_prefetch=1, grid=(tokens // tm,),
        in_specs=[pl.BlockSpec((1, 1, META_WORDS), lambda i, te: (i, 0, 0), memory_space=pltpu.SMEM),
                  pl.BlockSpec((1, 1, META_WORDS), lambda i, te: (jnp.maximum(i - 1, 0), 0, 0),
                               memory_space=pltpu.SMEM),
                  pl.BlockSpec((tm, D_MODEL), lambda i, te: (i, 0)),
                  pl.BlockSpec((tm, LANES), lambda i, te: (i, 0))],
        out_specs=pl.BlockSpec(memory_space=pl.ANY),
        scratch_shapes=[pltpu.VMEM((2, SORT_ROWS, D_MODEL), F32),
                        pltpu.VMEM((GROUP_TM, D_MODEL), F32), pltpu.SemaphoreType.DMA((2,))])
    max_tail = n_slots // GROUP_TM - -(-2 * tokens // GROUP_TM)
    return pl.pallas_call(
        functools.partial(_dispatch_kernel, max_tail=max_tail), grid_spec=grid_spec,
        out_shape=jax.ShapeDtypeStruct((n_slots, D_MODEL), F32),
        compiler_params=_params(("arbitrary",)),
        name="moe_dispatch",
    )(tile_end, meta, meta, h, route)


def _grouped_kernel(texp_ref, nt_ref, h_ref, wg_ref, wu_ref, wd_ref, y_ref):
    del texp_ref

    @pl.when(pl.program_id(1) == 0)
    def _():
        y_ref[...] = jnp.zeros_like(y_ref)

    @pl.when(pl.program_id(0) < nt_ref[0])
    def _():
        _swiglu_step(h_ref[...].astype(BF16), wg_ref.at[0], wu_ref.at[0], wd_ref.at[0], y_ref)


def _grouped_swiglu(hs, tile_expert, n_tiles_used, wg, wu, wd):
    n_slots = hs.shape[0]
    tm, tf = GROUP_TM, FFN_TF
    nf = D_FF // tf
    last = lambda i, nt: jnp.minimum(i, nt[0] - 1)
    fcol = lambda i, f, nt: jnp.where(i < nt[0], f, nf - 1)
    grid_spec = pltpu.PrefetchScalarGridSpec(
        num_scalar_prefetch=2, grid=(n_slots // tm, nf),
        in_specs=[pl.BlockSpec((tm, D_MODEL), lambda i, f, te, nt: (last(i, nt), 0)),
                  pl.BlockSpec((1, D_MODEL, tf), lambda i, f, te, nt: (te[i], 0, fcol(i, f, nt))),
                  pl.BlockSpec((1, D_MODEL, tf), lambda i, f, te, nt: (te[i], 0, fcol(i, f, nt))),
                  pl.BlockSpec((1, tf, D_MODEL), lambda i, f, te, nt: (te[i], fcol(i, f, nt), 0))],
        out_specs=pl.BlockSpec((tm, D_MODEL), lambda i, f, te, nt: (i, 0)))
    return pl.pallas_call(
        _grouped_kernel, grid_spec=grid_spec,
        out_shape=jax.ShapeDtypeStruct((n_slots, D_MODEL), F32),
        compiler_params=_params(("arbitrary", "arbitrary")),
        name="moe_grouped_swiglu",
    )(tile_expert, n_tiles_used, hs, wg, wu, wd)


def _combine_kernel(*refs, final):
    meta_ref, next_ref, route_ref, x_ref, y_ref = refs[:5]
    if final:
        gfin_ref, o_ref, sort_ref, sems = refs[5:]
    else:
        o_ref, sort_ref, sems = refs[5:]
    tm = x_ref.shape[0]
    i = pl.program_id(0)
    slot = i & 1

    def fetch(act, m_ref, s):
        if act == "start":
            sort_ref[s, pl.ds(2 * tm, SORT_ROWS - 2 * tm), :] = jnp.zeros(
                (SORT_ROWS - 2 * tm, D_MODEL), F32)
        _run_copies(act, m_ref, lambda off, n, dst: pltpu.make_async_copy(
            y_ref.at[pl.ds(dst, n)], sort_ref.at[s, pl.ds(off, n)], sems.at[s]))

    @pl.when(i == 0)
    def _():
        fetch("start", meta_ref, slot)

    @pl.when(i + 1 < pl.num_programs(0))
    def _():
        fetch("start", next_ref, 1 - slot)

    fetch("wait", meta_ref, slot)
    ys = sort_ref[slot].astype(BF16)
    route = route_ref[...]
    ip = lax.broadcasted_iota(jnp.int32, (tm, SORT_ROWS), 1)

    def pick(col):
        pos = route[:, col:col + 1].astype(jnp.int32)
        return jnp.dot(jnp.where(ip == pos, 1.0, 0.0).astype(BF16), ys,
                       preferred_element_type=F32)

    out = (x_ref[...] + route[:, _R_G1:_R_G1 + 1] * pick(_R_POS1)
           + route[:, _R_G2:_R_G2 + 1] * pick(_R_POS2))
    o_ref[...] = _rms(out, gfin_ref[...]) if final else out


def _combine(y, route, meta, x2, final_gain=None):
    tokens = x2.shape[0]
    tm = ROW_TM
    final = final_gain is not None
    n_steps = tokens // tm
    in_specs = [pl.BlockSpec((1, 1, META_WORDS), lambda i: (i, 0, 0), memory_space=pltpu.SMEM),
                pl.BlockSpec((1, 1, META_WORDS), lambda i: (jnp.minimum(i + 1, n_steps - 1), 0, 0),
                             memory_space=pltpu.SMEM),
                pl.BlockSpec((tm, LANES), lambda i: (i, 0)),
                pl.BlockSpec((tm, D_MODEL), lambda i: (i, 0)),
                pl.BlockSpec(memory_space=pl.ANY)]
    args = [meta, meta, route, x2, y]
    if final:
        in_specs.append(pl.BlockSpec((1, D_MODEL), lambda i: (0, 0)))
        args.append(final_gain)
    return pl.pallas_call(
        functools.partial(_combine_kernel, final=final),
        grid=(tokens // tm,), in_specs=in_specs,
        out_specs=pl.BlockSpec((tm, D_MODEL), lambda i: (i, 0)),
        out_shape=jax.ShapeDtypeStruct((tokens, D_MODEL), F32),
        scratch_shapes=[pltpu.VMEM((2, SORT_ROWS, D_MODEL), F32),
                        pltpu.SemaphoreType.DMA((2,))],
        compiler_params=_params(("arbitrary",)),
        name="moe_combine",
    )(*args)


def _routing_tables(tile_rec, counts, tokens):
    tm = GROUP_TM
    n_tok_tiles = tokens // ROW_TM
    max_rows = 2 * tokens + n_tok_tiles * N_EXPERTS * (RUN_ALIGN - 1)
    n_tiles = (max_rows + N_EXPERTS * (tm - 1)) // tm
    cnt = counts[0, :N_EXPERTS].astype(jnp.int32)
    tiles = (cnt + tm - 1) // tm
    tile_end = jnp.cumsum(tiles)
    start = (tile_end - tiles) * tm
    rec = tile_rec.reshape(n_tok_tiles, SUBLANES, LANES)[:, :, :N_EXPERTS].astype(jnp.int32)
    meta = jnp.concatenate([rec[:, _T_OFF], rec[:, _T_COUNT], start[None, :] + rec[:, _T_BEFORE],
                            jnp.zeros((n_tok_tiles, META_WORDS - 3 * N_EXPERTS), jnp.int32)],
                           axis=1)
    tile_ids = jnp.arange(n_tiles, dtype=jnp.int32)[:, None]
    tile_expert = jnp.minimum(jnp.sum(tile_ids >= tile_end[None, :], axis=1),
                              N_EXPERTS - 1).astype(jnp.int32)
    return (meta.reshape(n_tok_tiles, 1, META_WORDS), tile_expert, tile_end.astype(jnp.int32),
            n_tiles * tm)


def _moe(h, route, tile_rec, counts, x2, wg, wu, wd, final_gain=None):
    meta, tile_expert, tile_end, n_slots = _routing_tables(tile_rec, counts, x2.shape[0])
    hs = _dispatch(h, route, meta, tile_end, n_slots)
    y = _grouped_swiglu(hs, tile_expert, tile_end[-1:], wg, wu, wd)
    return _combine(y, route, meta, x2, final_gain)


def _widen_w_in(w):
    a = 3 * WIDTH_A + WIDTH_B
    kb = w[:, a:a + KV_WIDTH_B]
    vb = w[:, a + KV_WIDTH_B:]
    dup = lambda t: jnp.concatenate(
        [t[:, h * HEAD_DIM:(h + 1) * HEAD_DIM] for h in range(N_KV_B) for _ in range(2)], axis=1)
    return jnp.concatenate([w[:, :a], dup(kb), dup(vb)], axis=1).astype(BF16)


def kernel(x, positions, attn_norm, w_in, mix_norm_a, mix_norm_b, sinks, w_out, ffn_norm,
           dense_w_gate, dense_w_up, dense_w_down, router, moe_w_gate, moe_w_up, moe_w_down,
           final_norm):
    batch, seq, _ = x.shape
    depth = w_in.shape[0]
    tokens = batch * seq
    cos, sina, sinb = _rope_tables(positions)
    x2 = x.reshape(tokens, D_MODEL)
    seq3 = lambda t: t.reshape(batch, seq, t.shape[-1])
    dense_w = [w.astype(BF16) for w in (dense_w_gate, dense_w_up, dense_w_down)]
    n_moe = moe_w_gate.shape[0]
    for i in range(depth):
        jm = min(i // 2, n_moe - 1)
        qa, ka, va, qb, kb, vb, w_cast = _inproj(
            x2, attn_norm[i][None, :], _widen_w_in(w_in[i]), cos, sina, sinb, seq,
            moe_w_gate if i % 2 == 0 else moe_w_down, jm)
        if i % 2 == 0:
            wg_b = w_cast
        else:
            wd_b = w_cast
        oa = _dilated_attention(seq3(qa), seq3(ka), seq3(va)).reshape(tokens, WIDTH_A)
        ob = _swa_attention(seq3(qb), seq3(kb), seq3(vb), sinks[i]).reshape(tokens, WIDTH_B)
        j = i // 2
        last = i == depth - 1
        norms = (mix_norm_a[i][None, :], mix_norm_b[i][None, :], w_out[i].astype(BF16),
                 ffn_norm[i][None, :])
        if i % 2 == 0:
            x2, h, wu_b = _outproj(oa, ob, x2, *norms, cast=(moe_w_up, jm))
            x2 = _ffn(h, x2, *dense_w, j, final_gain=final_norm[None, :] if last else None)
        else:
            r_f32 = jnp.pad(router[j], ((0, 0), (0, LANES - N_EXPERTS)))
            r_hi = r_f32.astype(BF16)
            r_lo = (r_f32 - r_hi.astype(F32)).astype(BF16)
            router_pad = jnp.concatenate([r_hi, r_lo], axis=1)
            x2, h, route, tile_rec, counts = _outproj(oa, ob, x2, *norms, router_pad)
            x2 = _moe(h, route, tile_rec, counts, x2, wg_b, wu_b, wd_b,
                      final_gain=final_norm[None, :] if last else None)
    return x2.reshape(batch, seq, D_MODEL)
```

```python
import functools

import numpy as np
import jax
import jax.numpy as jnp
from jax import lax
from jax.experimental import pallas as pl
from jax.experimental.pallas import tpu as pltpu

D_MODEL = 1024
HEAD_DIM = 64
N_HEADS_A = 8
N_HEADS_B = 8
N_KV_B = 2
DILATED_BRANCHES = ((128, 1), (512, 4), (2048, 16))
SWA_WINDOW = 128
BLK = 128
ROPE_THETA = 10000.0
D_FF = 3584
N_EXPERTS = 8
RMS_EPS = 1e-5
WIDTH_A = N_HEADS_A * HEAD_DIM
WIDTH_B = N_HEADS_B * HEAD_DIM
KV_WIDTH_B = N_KV_B * HEAD_DIM

LANES = 128
SUBLANES = 8
NEG_BIG = -1e30
VMEM_LIMIT = 56 * 1024 * 1024
PROJ_TM = 1024

F32 = jnp.float32
BF16 = jnp.bfloat16


def _params(sem):
    return pltpu.CompilerParams(dimension_semantics=sem, vmem_limit_bytes=VMEM_LIMIT)


def _rms(x, g):
    return x * lax.rsqrt(jnp.mean(x * x, axis=-1, keepdims=True) + RMS_EPS) * g


def _rope_table_kernel(pos_ref, inv_ref, cos_ref, sina_ref, sinb_ref):
    ang = pos_ref[...].astype(F32) * inv_ref[...]
    c = jnp.cos(ang)
    s = jnp.sin(ang)
    lane = lax.broadcasted_iota(jnp.int32, ang.shape, 1)
    upper = (lane & (HEAD_DIM - 1)) >= HEAD_DIM // 2
    cos_ref[...] = c
    sina_ref[...] = jnp.where(upper, s, 0.0)
    sinb_ref[...] = jnp.where(upper, 0.0, -s)


def _rope_tables(positions):
    seq = positions.shape[0]
    inv = ROPE_THETA ** (-jnp.arange(0, HEAD_DIM, 2, dtype=F32) / HEAD_DIM)
    inv = jnp.tile(inv, LANES // (HEAD_DIM // 2))[None, :]
    rows = 1024
    tab = jax.ShapeDtypeStruct((seq, LANES), F32)
    return pl.pallas_call(
        _rope_table_kernel,
        grid=(seq // rows,),
        in_specs=[pl.BlockSpec((rows, 1), lambda i: (i, 0)),
                  pl.BlockSpec((1, LANES), lambda i: (0, 0))],
        out_specs=[pl.BlockSpec((rows, LANES), lambda i: (i, 0))] * 3,
        out_shape=[tab, tab, tab],
        compiler_params=_params(("arbitrary",)),
        name="rope_tables",
    )(positions.reshape(seq, 1), inv)


LOG2E = 1.4426950408889634
_QSCALE = HEAD_DIM ** -0.5 * LOG2E
_SEGS = (("qa", WIDTH_A, True, _QSCALE), ("ka", WIDTH_A, True, 1.0),
         ("va", WIDTH_A, False, 1.0), ("qb", WIDTH_B, True, _QSCALE),
         ("kb", 2 * KV_WIDTH_B, True, 1.0), ("vb", 2 * KV_WIDTH_B, False, 1.0))


def _cast_specs(w, layer, n_steps):
    _, n_exp, rows, cols = w.shape
    per = n_steps // n_exp
    rb = rows // per
    assert n_steps % n_exp == 0 and rows % per == 0 and rb % 16 == 0
    return (pl.BlockSpec((1, 1, rb, cols), lambda i: (layer, i // per, i % per, 0)),
            pl.BlockSpec((1, rb, cols), lambda i: (i // per, i % per, 0)),
            jax.ShapeDtypeStruct((n_exp, rows, cols), BF16))


def _inproj_kernel(x_ref, g_ref, w_ref, cos_ref, sina_ref, sinb_ref, cast_in_ref, *out_refs):
    out_refs[-1][0] = cast_in_ref[0, 0].astype(BF16)
    h = _rms(x_ref[...], g_ref[...]).astype(BF16)
    cos = cos_ref[...]
    sina = sina_ref[...]
    sinb = sinb_ref[...]
    col = 0
    for (_, width, roped, scale), o_ref in zip(_SEGS, out_refs):
        t = jnp.dot(h, w_ref[:, col:col + width], preferred_element_type=F32)
        col += width
        for j in range(width // LANES):
            tj = t[:, j * LANES:(j + 1) * LANES]
            if roped:
                tj = (tj * cos + pltpu.roll(tj, HEAD_DIM // 2, 1) * sina
                      + pltpu.roll(tj, LANES - HEAD_DIM // 2, 1) * sinb)
                if scale != 1.0:
                    tj = tj * scale
            o_ref[:, j * LANES:(j + 1) * LANES] = tj.astype(o_ref.dtype)


def _inproj(x2, g, w_ext, cos, sina, sinb, seq, cast_w, cast_layer):
    tokens = x2.shape[0]
    tm = PROJ_TM
    spt = seq // tm
    widths = [s[1] for s in _SEGS]
    total = sum(widths)
    cast_in, cast_out, cast_shape = _cast_specs(cast_w, cast_layer, tokens // tm)
    return pl.pallas_call(
        _inproj_kernel,
        grid=(tokens // tm,),
        in_specs=[pl.BlockSpec((tm, D_MODEL), lambda i: (i, 0)),
                  pl.BlockSpec((1, D_MODEL), lambda i: (0, 0)),
                  pl.BlockSpec((D_MODEL, total), lambda i: (0, 0)),
                  pl.BlockSpec((tm, LANES), lambda i: (i % spt, 0)),
                  pl.BlockSpec((tm, LANES), lambda i: (i % spt, 0)),
                  pl.BlockSpec((tm, LANES), lambda i: (i % spt, 0)),
                  cast_in],
        out_specs=[pl.BlockSpec((tm, w), lambda i: (i, 0)) for w in widths] + [cast_out],
        out_shape=[jax.ShapeDtypeStruct((tokens, w), BF16) for w in widths] + [cast_shape],
        compiler_params=_params(("arbitrary",)),
        name="inproj_rope",
    )(x2, g, w_ext, cos, sina, sinb, cast_w)


def _band_bias(max_dist):
    p = np.arange(BLK)[:, None]
    c = np.arange(2 * BLK)[None, :]
    first = c <= p
    dist = BLK + p - c
    main = (dist >= 0) & (dist <= max_dist)
    bias = np.where(np.stack([first, main]), 0.0, NEG_BIG).astype(np.float32)
    return jnp.asarray(np.concatenate([bias, bias], axis=1))


def _low_lanes():
    return lax.broadcasted_iota(jnp.int32, (BLK, LANES), 1) < HEAD_DIM


def _band_block(q, kb, vb, bias, sink=None):
    low = _low_lanes()
    zero = jnp.zeros_like(q)
    qq = jnp.concatenate([jnp.where(low, q, zero), jnp.where(low, zero, q)], axis=0)
    s = lax.dot_general(qq, kb, (((1,), (1,)), ((), ())), preferred_element_type=F32) + bias
    m = jnp.max(s, axis=1, keepdims=True)
    if sink is not None:
        m = jnp.maximum(m, sink)
    p = jnp.exp2(s - m).astype(BF16)
    ol = jnp.dot(p, jnp.concatenate([vb, jnp.ones_like(vb)], axis=1),
                 preferred_element_type=F32)
    acc = jnp.where(low, ol[:BLK, :LANES], ol[BLK:, :LANES])
    l = jnp.where(low, ol[:BLK, LANES:], ol[BLK:, LANES:])
    if sink is not None:
        ls = jnp.exp2(sink - m)
        l = l + jnp.where(low, ls[:BLK], ls[BLK:])
    return acc, jnp.where(low, m[:BLK], m[BLK:]), l


def _finish(acc, m, l):
    return acc * (1.0 / l), m + jnp.log2(l)


def _merge(o_s, lse_s, acc, m, l, want_lse=True):
    hi = jnp.maximum(lse_s, m)
    ws = jnp.exp2(lse_s - hi)
    wb = jnp.exp2(m - hi)
    den = ws + wb * l
    o = (ws * o_s + wb * acc) * (1.0 / den)
    return (o, hi + jnp.log2(den)) if want_lse else (o, None)


_CLASSES = 4
_LOOP_UNROLL = 32


def _dilated_kernel(q_ref, k_ref, v_ref, bias_ref, o_ref,
                    q4, k4, v4, tmp, so4, sl4, il_o, il_l, *, seq):
    cls = seq // _CLASSES
    nb4 = cls // BLK
    nb16 = seq // 16 // BLK
    sh4, sh16 = nb4.bit_length() - 1, nb16.bit_length() - 1
    chunk = 512
    shc = (cls // chunk).bit_length() - 1

    for src, dst in ((q_ref, q4), (k_ref, k4), (v_ref, v4)):
        def widen(i, c, src=src):
            r0 = pl.multiple_of(i * chunk, chunk)
            tmp[pl.ds(r0, chunk), :] = src[0, pl.ds(r0, chunk), :].astype(F32)
            return c
        lax.fori_loop(0, seq // chunk, widen, 0)

        def regroup(i, c, dst=dst):
            r = lax.shift_right_logical(i, shc)
            j = i & (cls // chunk - 1)
            d0 = pl.multiple_of(r * cls + j * chunk, chunk)
            dst[pl.ds(d0, chunk), :] = tmp[pl.ds(r + _CLASSES * chunk * j, chunk,
                                                 stride=_CLASSES), :]
            return c
        lax.fori_loop(0, seq // chunk, regroup, 0)

    def dil4(idx, c):
        r = lax.shift_right_logical(idx, sh4)
        n = idx & (nb4 - 1)
        q0 = pl.multiple_of(r * cls + n * BLK, BLK)
        k0 = pl.multiple_of(r * cls + jnp.maximum(n - 1, 0) * BLK, BLK)
        o, lse = _finish(*_band_block(q4[pl.ds(q0, BLK), :].astype(BF16),
                                      k4[pl.ds(k0, 2 * BLK), :].astype(BF16),
                                      v4[pl.ds(k0, 2 * BLK), :].astype(BF16),
                                      bias_ref[jnp.minimum(n, 1)]))
        so4[pl.ds(q0, BLK), :] = o
        sl4[pl.ds(q0, BLK), :] = lse
        return c
    lax.fori_loop(0, _CLASSES * nb4, dil4, 0, unroll=_LOOP_UNROLL)

    def dil16(idx, c):
        n = idx & (nb16 - 1)
        sub = lax.shift_right_logical(idx, sh16) & (_CLASSES - 1)
        r = lax.shift_right_logical(idx, sh16 + 2)
        q0 = r * cls + sub + _CLASSES * BLK * n
        k0 = r * cls + sub + _CLASSES * BLK * jnp.maximum(n - 1, 0)
        qs = pl.ds(q0, BLK, stride=_CLASSES)
        ks = pl.ds(k0, 2 * BLK, stride=_CLASSES)
        o_s, l_s = so4[qs, :], sl4[qs, :]
        o, lse = _merge(o_s, l_s, *_band_block(
            q4[qs, :].astype(BF16), k4[ks, :].astype(BF16), v4[ks, :].astype(BF16),
            bias_ref[jnp.minimum(n, 1)]))
        so4[qs, :] = o
        sl4[qs, :] = lse
        return c
    lax.fori_loop(0, 16 * nb16, dil16, 0, unroll=_LOOP_UNROLL)

    piece = BLK // _CLASSES

    def dil1(n, c):
        q0 = pl.multiple_of(n * BLK, BLK)
        k0 = pl.multiple_of(jnp.maximum(n - 1, 0) * BLK, BLK)
        for r in range(_CLASSES):
            s0 = pl.multiple_of(r * cls + n * piece, piece)
            il_o[pl.ds(r, piece, stride=_CLASSES), :] = so4[pl.ds(s0, piece), :]
            il_l[pl.ds(r, piece, stride=_CLASSES), :] = sl4[pl.ds(s0, piece), :]
        o_s, l_s = il_o[...], il_l[...]
        o, _ = _merge(o_s, l_s, *_band_block(
            q_ref[0, pl.ds(q0, BLK), :], k_ref[0, pl.ds(k0, 2 * BLK), :],
            v_ref[0, pl.ds(k0, 2 * BLK), :], bias_ref[jnp.minimum(n, 1)]), want_lse=False)
        o_ref[0, pl.ds(q0, BLK), :] = o.astype(o_ref.dtype)
        return c
    lax.fori_loop(0, seq // BLK, dil1, 0, unroll=_LOOP_UNROLL)


def _dilated_attention(q, k, v):
    assert DILATED_BRANCHES == ((128, 1), (512, 4), (2048, 16))
    batch, seq, width = q.shape
    assert seq % (16 * BLK) == 0 and (seq & (seq - 1)) == 0
    blk = pl.BlockSpec((1, seq, LANES), lambda b, c: (b, 0, c))
    rows = pltpu.VMEM((seq, LANES), F32)
    small = pltpu.VMEM((BLK, LANES), F32)
    return pl.pallas_call(
        functools.partial(_dilated_kernel, seq=seq),
        grid=(batch, width // LANES),
        in_specs=[blk, blk, blk, pl.BlockSpec((2, 2 * BLK, 2 * BLK), lambda b, c: (0, 0, 0))],
        out_specs=blk,
        out_shape=jax.ShapeDtypeStruct((batch, seq, width), BF16),
        scratch_shapes=[rows] * 6 + [small] * 2,
        compiler_params=_params(("arbitrary", "arbitrary")),
        name="dilated_attn",
    )(q, k, v, _band_bias(BLK))


def _swa_kernel(sink_ref, q_ref, k_ref, v_ref, bias_ref, o_ref, *, n_blocks):
    c = pl.program_id(1)
    row = lax.broadcasted_iota(jnp.int32, (2 * BLK, 1), 0)
    sink = jnp.where(row < BLK, sink_ref[2 * c], sink_ref[2 * c + 1]) * LOG2E

    def body(n, carry):
        q0 = pl.multiple_of(n * BLK, BLK)
        k0 = pl.multiple_of(jnp.maximum(n - 1, 0) * BLK, BLK)
        acc, _, l = _band_block(q_ref[0, pl.ds(q0, BLK), :], k_ref[0, pl.ds(k0, 2 * BLK), :],
                                v_ref[0, pl.ds(k0, 2 * BLK), :], bias_ref[jnp.minimum(n, 1)],
                                sink)
        o_ref[0, pl.ds(q0, BLK), :] = (acc * (1.0 / l)).astype(o_ref.dtype)
        return carry
    lax.fori_loop(0, n_blocks, body, 0, unroll=_LOOP_UNROLL)


def _swa_attention(q, k, v, sinks):
    batch, seq, width = q.shape
    qblk = pl.BlockSpec((1, seq, LANES), lambda b, c, s: (b, 0, c))
    kblk = pl.BlockSpec((1, seq, LANES), lambda b, c, s: (b, 0, c // 2))
    grid_spec = pltpu.PrefetchScalarGridSpec(
        num_scalar_prefetch=1, grid=(batch, width // LANES),
        in_specs=[qblk, kblk, kblk,
                  pl.BlockSpec((2, 2 * BLK, 2 * BLK), lambda b, c, s: (0, 0, 0))],
        out_specs=qblk)
    return pl.pallas_call(
        functools.partial(_swa_kernel, n_blocks=seq // BLK), grid_spec=grid_spec,
        out_shape=jax.ShapeDtypeStruct((batch, seq, width), BF16),
        compiler_params=_params(("arbitrary", "arbitrary")),
        name="swa_sink_attn",
    )(sinks, q, k, v, _band_bias(SWA_WINDOW - 1))


ROW_TM = 512
RUN_ALIGN = SUBLANES
SORT_ROWS = 2 * ROW_TM + N_EXPERTS * RUN_ALIGN
_R_G1, _R_G2, _R_POS1, _R_POS2 = range(4)
_T_COUNT, _T_OFF, _T_BEFORE = range(3)
META_WORDS = 32


def _route(logits, count_ref):
    tm = logits.shape[0]
    lane = lax.broadcasted_iota(jnp.int32, logits.shape, 1)
    v1 = jnp.max(logits, axis=1, keepdims=True)
    i1 = jnp.min(jnp.where(logits == v1, lane, LANES), axis=1, keepdims=True)
    rest = jnp.where(lane == i1, NEG_BIG, logits)
    v2 = jnp.max(rest, axis=1, keepdims=True)
    i2 = jnp.min(jnp.where(rest == v2, lane, LANES), axis=1, keepdims=True)
    e2 = jnp.exp(v2 - v1)
    g1 = 1.0 / (1.0 + e2)
    g2 = e2 / (1.0 + e2)
    sel = jnp.where((lane == i1) | (lane == i2), 1.0, 0.0)
    r = lax.broadcasted_iota(jnp.int32, (tm, tm), 0)
    c = lax.broadcasted_iota(jnp.int32, (tm, tm), 1)
    tri = jnp.where(c < r, 1.0, 0.0).astype(BF16)
    local = jnp.dot(tri, sel.astype(BF16), preferred_element_type=F32)
    count = jnp.sum(sel, axis=0, keepdims=True)
    padded = jnp.floor((count + (RUN_ALIGN - 1)) * (1.0 / RUN_ALIGN)) * RUN_ALIGN
    run_end = jnp.broadcast_to(padded, (SUBLANES, LANES))
    for shift in (1, 2, 4):
        run_end = run_end + pltpu.roll(run_end, shift, 1)
    offset = run_end[0:1] - padded
    pos = local + offset
    pos1 = jnp.sum(jnp.where(lane == i1, pos, 0.0), axis=1, keepdims=True)
    pos2 = jnp.sum(jnp.where(lane == i2, pos, 0.0), axis=1, keepdims=True)
    before = count_ref[...]
    count_ref[...] = before + padded
    rec = jnp.zeros(logits.shape, F32)
    for ln, col in ((_R_G1, g1), (_R_G2, g2), (_R_POS1, pos1), (_R_POS2, pos2)):
        rec = jnp.where(lane == ln, col, rec)
    row = lax.broadcasted_iota(jnp.int32, (SUBLANES, LANES), 0)
    tile = jnp.zeros((SUBLANES, LANES), F32)
    for rw, val in ((_T_COUNT, padded), (_T_OFF, offset), (_T_BEFORE, before)):
        tile = jnp.where(row == rw, val, tile)
    return rec, tile


def _outproj_kernel(*refs, with_router):
    oa_ref, ob_ref, x_ref, ga_ref, gb_ref, w_ref, gf_ref = refs[:7]
    rest = refs[7:]
    if with_router:
        r_ref, xo_ref, ho_ref, route_ref, tile_ref, count_ref = rest
    else:
        cast_in_ref, xo_ref, ho_ref, cast_out_ref = rest
        cast_out_ref[0] = cast_in_ref[0, 0].astype(BF16)
    na = _rms(oa_ref[...].astype(F32), ga_ref[...]).astype(BF16)
    nb = _rms(ob_ref[...].astype(F32), gb_ref[...]).astype(BF16)
    xn = (x_ref[...] + jnp.dot(na, w_ref[:WIDTH_A, :], preferred_element_type=F32)
          + jnp.dot(nb, w_ref[WIDTH_A:, :], preferred_element_type=F32))
    xo_ref[...] = xn
    hf = _rms(xn, gf_ref[...])
    h_hi = hf.astype(BF16)
    ho_ref[...] = h_hi
    if with_router:
        @pl.when(pl.program_id(0) == 0)
        def _():
            count_ref[...] = jnp.zeros_like(count_ref)
        h_lo = (hf - h_hi.astype(F32)).astype(BF16)
        t = jnp.dot(h_hi, r_ref[...], preferred_element_type=F32)
        logits = (t[:, :LANES] + t[:, LANES:]
                  + jnp.dot(h_lo, r_ref[:, :LANES], preferred_element_type=F32))
        lane = lax.broadcasted_iota(jnp.int32, logits.shape, 1)
        logits = jnp.where(lane < N_EXPERTS, logits, NEG_BIG)
        route_ref[...], tile_ref[...] = _route(logits, count_ref)


def _outproj(oa, ob, x2, ga, gb, w_out, gf, router_pad=None, cast=None):
    tokens = x2.shape[0]
    with_router = router_pad is not None
    assert with_router != (cast is not None)
    tm = ROW_TM if with_router else PROJ_TM
    row = lambda w: pl.BlockSpec((tm, w), lambda i: (i, 0))
    const = lambda a, b: pl.BlockSpec((a, b), lambda i: (0, 0))
    in_specs = [row(WIDTH_A), row(WIDTH_B), row(D_MODEL), const(1, WIDTH_A), const(1, WIDTH_B),
                const(D_MODEL, D_MODEL), const(1, D_MODEL)]
    out_specs = [row(D_MODEL), row(D_MODEL)]
    out_shape = [jax.ShapeDtypeStruct((tokens, D_MODEL), F32),
                 jax.ShapeDtypeStruct((tokens, D_MODEL), BF16)]
    args = [oa, ob, x2, ga, gb, w_out, gf]
    if with_router:
        in_specs.append(const(D_MODEL, 2 * LANES))
        out_specs += [row(LANES), pl.BlockSpec((SUBLANES, LANES), lambda i: (i, 0)),
                      const(1, LANES)]
        out_shape += [jax.ShapeDtypeStruct((tokens, LANES), F32),
                      jax.ShapeDtypeStruct((tokens // tm * SUBLANES, LANES), F32),
                      jax.ShapeDtypeStruct((1, LANES), F32)]
        args.append(router_pad)
    else:
        cast_in, cast_out, cast_shape = _cast_specs(*cast, tokens // tm)
        in_specs.append(cast_in)
        out_specs.append(cast_out)
        out_shape.append(cast_shape)
        args.append(cast[0])
    return pl.pallas_call(
        functools.partial(_outproj_kernel, with_router=with_router),
        grid=(tokens // tm,), in_specs=in_specs, out_specs=out_specs, out_shape=out_shape,
        compiler_params=_params(("arbitrary",)),
        name="mix_outproj_router" if with_router else "mix_outproj",
    )(*args)


def _silu(x):
    return x * (1.0 / (1.0 + jnp.exp(-x)))


FFN_TM = 1024
GROUP_TM = 768
FFN_TF = 1792
FFN_TC = 1024


def _swiglu_step(h, wg_ref, wu_ref, wd_ref, acc_ref):
    for c0 in range(0, FFN_TF, FFN_TC):
        c1 = min(c0 + FFN_TC, FFN_TF)
        g = jnp.dot(h, wg_ref[:, c0:c1], preferred_element_type=F32)
        u = jnp.dot(h, wu_ref[:, c0:c1], preferred_element_type=F32)
        acc_ref[...] += jnp.dot((_silu(g) * u).astype(BF16), wd_ref[c0:c1, :],
                                preferred_element_type=F32)


def _ffn_kernel(*refs, final):
    h_ref, wg_ref, wu_ref, wd_ref, x_ref = refs[:5]
    if final:
        gfin_ref, o_ref = refs[5:]
    else:
        (o_ref,) = refs[5:]
    f = pl.program_id(1)

    @pl.when(f == 0)
    def _():
        o_ref[...] = x_ref[...]

    _swiglu_step(h_ref[...], wg_ref.at[0], wu_ref.at[0], wd_ref.at[0], o_ref)
    if final:
        @pl.when(f == pl.num_programs(1) - 1)
        def _():
            o_ref[...] = _rms(o_ref[...], gfin_ref[...])


def _ffn(h, x2, wg, wu, wd, layer, final_gain=None):
    tokens = x2.shape[0]
    tm, tf = FFN_TM, FFN_TF
    final = final_gain is not None
    tok = lambda w: pl.BlockSpec((tm, w), lambda i, f: (i, 0))
    in_specs = [tok(D_MODEL),
                pl.BlockSpec((1, D_MODEL, tf), lambda i, f: (layer, 0, f)),
                pl.BlockSpec((1, D_MODEL, tf), lambda i, f: (layer, 0, f)),
                pl.BlockSpec((1, tf, D_MODEL), lambda i, f: (layer, f, 0)),
                tok(D_MODEL)]
    args = [h, wg, wu, wd, x2]
    if final:
        in_specs.append(pl.BlockSpec((1, D_MODEL), lambda i, f: (0, 0)))
        args.append(final_gain)
    return pl.pallas_call(
        functools.partial(_ffn_kernel, final=final),
        grid=(tokens // tm, D_FF // tf), in_specs=in_specs, out_specs=tok(D_MODEL),
        out_shape=jax.ShapeDtypeStruct((tokens, D_MODEL), F32),
        compiler_params=_params(("arbitrary", "arbitrary")),
        name="dense_swiglu",
    )(*args)


def _run_copies(act, meta_ref, make_copy):
    for e in range(N_EXPERTS):
        n = pl.multiple_of(meta_ref[0, 0, N_EXPERTS + e], RUN_ALIGN)

        @pl.when(n > 0)
        def _(e=e, n=n):
            off = pl.multiple_of(meta_ref[0, 0, e], RUN_ALIGN)
            dst = pl.multiple_of(meta_ref[0, 0, 2 * N_EXPERTS + e], RUN_ALIGN)
            getattr(make_copy(off, n, dst), act)()


def _sort_matrix(route, n_rows):
    rec_t = route.T
    p1 = rec_t[_R_POS1:_R_POS1 + 1, :].astype(jnp.int32)
    p2 = rec_t[_R_POS2:_R_POS2 + 1, :].astype(jnp.int32)
    ip = lax.broadcasted_iota(jnp.int32, (n_rows, route.shape[0]), 0)
    return jnp.where((ip == p1) | (ip == p2), 1.0, 0.0).astype(BF16)


def _dispatch_kernel(tend_ref, meta_ref, prev_ref, h_ref, route_ref, out_ref, sort_ref, zero_ref,
                     sems, *, max_tail):
    n_tiles = out_ref.shape[0] // GROUP_TM

    @pl.when(pl.program_id(0) == 0)
    def _():
        zero_ref[...] = jnp.zeros_like(zero_ref)

        def zero_tile(t):
            return pltpu.make_async_copy(zero_ref, out_ref.at[pl.ds(t * GROUP_TM, GROUP_TM)],
                                         sems.at[0])

        def has_tiles(e):
            return tend_ref[e] > (tend_ref[e - 1] if e else 0)

        for act in ("start", "wait"):
            for e in range(N_EXPERTS):
                @pl.when(has_tiles(e))
                def _(e=e, act=act):
                    getattr(zero_tile(tend_ref[e] - 1), act)()
            for k in range(max_tail):
                t = tend_ref[N_EXPERTS - 1] + k

                @pl.when(t < n_tiles)
                def _(t=t, act=act):
                    getattr(zero_tile(t), act)()

    i = pl.program_id(0)
    slot = i & 1

    def run_copy(s):
        return lambda off, n, dst: pltpu.make_async_copy(
            sort_ref.at[s, pl.ds(off, n)], out_ref.at[pl.ds(dst, n)], sems.at[s])

    sort_ref[slot] = jnp.dot(_sort_matrix(route_ref[...], SORT_ROWS), h_ref[...],
                             preferred_element_type=F32)
    _run_copies("start", meta_ref, run_copy(slot))

    @pl.when(i > 0)
    def _():
        _run_copies("wait", prev_ref, run_copy(1 - slot))

    @pl.when(i == pl.num_programs(0) - 1)
    def _():
        _run_copies("wait", meta_ref, run_copy(slot))


def _dispatch(h, route, meta, tile_end, n_slots):
    tokens = h.shape[0]
    tm = ROW_TM
    grid_spec = pltpu.PrefetchScalarGridSpec(
        num_scalar_prefetch=1, grid=(tokens // tm,),
        in_specs=[pl.BlockSpec((1, 1, META_WORDS), lambda i, te: (i, 0, 0), memory_space=pltpu.SMEM),
                  pl.BlockSpec((1, 1, META_WORDS), lambda i, te: (jnp.maximum(i - 1, 0), 0, 0),
                               memory_space=pltpu.SMEM),
                  pl.BlockSpec((tm, D_MODEL), lambda i, te: (i, 0)),
                  pl.BlockSpec((tm, LANES), lambda i, te: (i, 0))],
        out_specs=pl.BlockSpec(memory_space=pl.ANY),
        scratch_shapes=[pltpu.VMEM((2, SORT_ROWS, D_MODEL), F32),
                        pltpu.VMEM((GROUP_TM, D_MODEL), F32), pltpu.SemaphoreType.DMA((2,))])
    max_tail = n_slots // GROUP_TM - -(-2 * tokens // GROUP_TM)
    return pl.pallas_call(
        functools.partial(_dispatch_kernel, max_tail=max_tail), grid_spec=grid_spec,
        out_shape=jax.ShapeDtypeStruct((n_slots, D_MODEL), F32),
        compiler_params=_params(("arbitrary",)),
        name="moe_dispatch",
    )(tile_end, meta, meta, h, route)


def _grouped_kernel(texp_ref, nt_ref, h_ref, wg_ref, wu_ref, wd_ref, y_ref):
    del texp_ref

    @pl.when(pl.program_id(1) == 0)
    def _():
        y_ref[...] = jnp.zeros_like(y_ref)

    @pl.when(pl.program_id(0) < nt_ref[0])
    def _():
        _swiglu_step(h_ref[...].astype(BF16), wg_ref.at[0], wu_ref.at[0], wd_ref.at[0], y_ref)


def _grouped_swiglu(hs, tile_expert, n_tiles_used, wg, wu, wd):
    n_slots = hs.shape[0]
    tm, tf = GROUP_TM, FFN_TF
    nf = D_FF // tf
    last = lambda i, nt: jnp.minimum(i, nt[0] - 1)
    fcol = lambda i, f, nt: jnp.where(i < nt[0], f, nf - 1)
    grid_spec = pltpu.PrefetchScalarGridSpec(
        num_scalar_prefetch=2, grid=(n_slots // tm, nf),
        in_specs=[pl.BlockSpec((tm, D_MODEL), lambda i, f, te, nt: (last(i, nt), 0)),
                  pl.BlockSpec((1, D_MODEL, tf), lambda i, f, te, nt: (te[i], 0, fcol(i, f, nt))),
                  pl.BlockSpec((1, D_MODEL, tf), lambda i, f, te, nt: (te[i], 0, fcol(i, f, nt))),
                  pl.BlockSpec((1, tf, D_MODEL), lambda i, f, te, nt: (te[i], fcol(i, f, nt), 0))],
        out_specs=pl.BlockSpec((tm, D_MODEL), lambda i, f, te, nt: (i, 0)))
    return pl.pallas_call(
        _grouped_kernel, grid_spec=grid_spec,
        out_shape=jax.ShapeDtypeStruct((n_slots, D_MODEL), F32),
        compiler_params=_params(("arbitrary", "arbitrary")),
        name="moe_grouped_swiglu",
    )(tile_expert, n_tiles_used, hs, wg, wu, wd)


def _combine_kernel(*refs, final):
    meta_ref, next_ref, route_ref, x_ref, y_ref = refs[:5]
    if final:
        gfin_ref, o_ref, sort_ref, sems = refs[5:]
    else:
        o_ref, sort_ref, sems = refs[5:]
    tm = x_ref.shape[0]
    i = pl.program_id(0)
    slot = i & 1

    def fetch(act, m_ref, s):
        if act == "start":
            sort_ref[s, pl.ds(2 * tm, SORT_ROWS - 2 * tm), :] = jnp.zeros(
                (SORT_ROWS - 2 * tm, D_MODEL), F32)
        _run_copies(act, m_ref, lambda off, n, dst: pltpu.make_async_copy(
            y_ref.at[pl.ds(dst, n)], sort_ref.at[s, pl.ds(off, n)], sems.at[s]))

    @pl.when(i == 0)
    def _():
        fetch("start", meta_ref, slot)

    @pl.when(i + 1 < pl.num_programs(0))
    def _():
        fetch("start", next_ref, 1 - slot)

    fetch("wait", meta_ref, slot)
    ys = sort_ref[slot].astype(BF16)
    route = route_ref[...]
    ip = lax.broadcasted_iota(jnp.int32, (tm, SORT_ROWS), 1)

    def pick(col):
        pos = route[:, col:col + 1].astype(jnp.int32)
        return jnp.dot(jnp.where(ip == pos, 1.0, 0.0).astype(BF16), ys,
                       preferred_element_type=F32)

    out = (x_ref[...] + route[:, _R_G1:_R_G1 + 1] * pick(_R_POS1)
           + route[:, _R_G2:_R_G2 + 1] * pick(_R_POS2))
    o_ref[...] = _rms(out, gfin_ref[...]) if final else out


def _combine(y, route, meta, x2, final_gain=None):
    tokens = x2.shape[0]
    tm = ROW_TM
    final = final_gain is not None
    n_steps = tokens // tm
    in_specs = [pl.BlockSpec((1, 1, META_WORDS), lambda i: (i, 0, 0), memory_space=pltpu.SMEM),
                pl.BlockSpec((1, 1, META_WORDS), lambda i: (jnp.minimum(i + 1, n_steps - 1), 0, 0),
                             memory_space=pltpu.SMEM),
                pl.BlockSpec((tm, LANES), lambda i: (i, 0)),
                pl.BlockSpec((tm, D_MODEL), lambda i: (i, 0)),
                pl.BlockSpec(memory_space=pl.ANY)]
    args = [meta, meta, route, x2, y]
    if final:
        in_specs.append(pl.BlockSpec((1, D_MODEL), lambda i: (0, 0)))
        args.append(final_gain)
    return pl.pallas_call(
        functools.partial(_combine_kernel, final=final),
        grid=(tokens // tm,), in_specs=in_specs,
        out_specs=pl.BlockSpec((tm, D_MODEL), lambda i: (i, 0)),
        out_shape=jax.ShapeDtypeStruct((tokens, D_MODEL), F32),
        scratch_shapes=[pltpu.VMEM((2, SORT_ROWS, D_MODEL), F32),
                        pltpu.SemaphoreType.DMA((2,))],
        compiler_params=_params(("arbitrary",)),
        name="moe_combine",
    )(*args)


def _routing_tables(tile_rec, counts, tokens):
    tm = GROUP_TM
    n_tok_tiles = tokens // ROW_TM
    max_rows = 2 * tokens + n_tok_tiles * N_EXPERTS * (RUN_ALIGN - 1)
    n_tiles = (max_rows + N_EXPERTS * (tm - 1)) // tm
    cnt = counts[0, :N_EXPERTS].astype(jnp.int32)
    tiles = (cnt + tm - 1) // tm
    tile_end = jnp.cumsum(tiles)
    start = (tile_end - tiles) * tm
    rec = tile_rec.reshape(n_tok_tiles, SUBLANES, LANES)[:, :, :N_EXPERTS].astype(jnp.int32)
    meta = jnp.concatenate([rec[:, _T_OFF], rec[:, _T_COUNT], start[None, :] + rec[:, _T_BEFORE],
                            jnp.zeros((n_tok_tiles, META_WORDS - 3 * N_EXPERTS), jnp.int32)],
                           axis=1)
    tile_ids = jnp.arange(n_tiles, dtype=jnp.int32)[:, None]
    tile_expert = jnp.minimum(jnp.sum(tile_ids >= tile_end[None, :], axis=1),
                              N_EXPERTS - 1).astype(jnp.int32)
    return (meta.reshape(n_tok_tiles, 1, META_WORDS), tile_expert, tile_end.astype(jnp.int32),
            n_tiles * tm)


def _moe(h, route, tile_rec, counts, x2, wg, wu, wd, final_gain=None):
    meta, tile_expert, tile_end, n_slots = _routing_tables(tile_rec, counts, x2.shape[0])
    hs = _dispatch(h, route, meta, tile_end, n_slots)
    y = _grouped_swiglu(hs, tile_expert, tile_end[-1:], wg, wu, wd)
    return _combine(y, route, meta, x2, final_gain)


def _widen_w_in(w):
    a = 3 * WIDTH_A + WIDTH_B
    kb = w[:, a:a + KV_WIDTH_B]
    vb = w[:, a + KV_WIDTH_B:]
    dup = lambda t: jnp.concatenate(
        [t[:, h * HEAD_DIM:(h + 1) * HEAD_DIM] for h in range(N_KV_B) for _ in range(2)], axis=1)
    return jnp.concatenate([w[:, :a], dup(kb), dup(vb)], axis=1).astype(BF16)


def kernel(x, positions, attn_norm, w_in, mix_norm_a, mix_norm_b, sinks, w_out, ffn_norm,
           dense_w_gate, dense_w_up, dense_w_down, router, moe_w_gate, moe_w_up, moe_w_down,
           final_norm):
    batch, seq, _ = x.shape
    depth = w_in.shape[0]
    tokens = batch * seq
    cos, sina, sinb = _rope_tables(positions)
    x2 = x.reshape(tokens, D_MODEL)
    seq3 = lambda t: t.reshape(batch, seq, t.shape[-1])
    dense_w = [w.astype(BF16) for w in (dense_w_gate, dense_w_up, dense_w_down)]
    n_moe = moe_w_gate.shape[0]
    for i in range(depth):
        jm = min(i // 2, n_moe - 1)
        qa, ka, va, qb, kb, vb, w_cast = _inproj(
            x2, attn_norm[i][None, :], _widen_w_in(w_in[i]), cos, sina, sinb, seq,
            moe_w_gate if i % 2 == 0 else moe_w_down, jm)
        if i % 2 == 0:
            wg_b = w_cast
        else:
            wd_b = w_cast
        oa = _dilated_attention(seq3(qa), seq3(ka), seq3(va)).reshape(tokens, WIDTH_A)
        ob = _swa_attention(seq3(qb), seq3(kb), seq3(vb), sinks[i]).reshape(tokens, WIDTH_B)
        j = i // 2
        last = i == depth - 1
        norms = (mix_norm_a[i][None, :], mix_norm_b[i][None, :], w_out[i].astype(BF16),
                 ffn_norm[i][None, :])
        if i % 2 == 0:
            x2, h, wu_b = _outproj(oa, ob, x2, *norms, cast=(moe_w_up, jm))
            x2 = _ffn(h, x2, *dense_w, j, final_gain=final_norm[None, :] if last else None)
        else:
            r_f32 = jnp.pad(router[j], ((0, 0), (0, LANES - N_EXPERTS)))
            r_hi = r_f32.astype(BF16)
            r_lo = (r_f32 - r_hi.astype(F32)).astype(BF16)
            router_pad = jnp.concatenate([r_hi, r_lo], axis=1)
            x2, h, route, tile_rec, counts = _outproj(oa, ob, x2, *norms, router_pad)
            x2 = _moe(h, route, tile_rec, counts, x2, wg_b, wu_b, wd_b,
                      final_gain=final_norm[None, :] if last else None)
    return x2.reshape(batch, seq, D_MODEL)
```

```python
import functools

import numpy as np
import jax
import jax.numpy as jnp
from jax import lax
from jax.experimental import pallas as pl
from jax.experimental.pallas import tpu as pltpu

D_MODEL = 1024
HEAD_DIM = 64
N_HEADS_A = 8
N_HEADS_B = 8
N_KV_B = 2
DILATED_BRANCHES = ((128, 1), (512, 4), (2048, 16))
SWA_WINDOW = 128
BLK = 128
ROPE_THETA = 10000.0
D_FF = 3584
N_EXPERTS = 8
RMS_EPS = 1e-5
WIDTH_A = N_HEADS_A * HEAD_DIM
WIDTH_B = N_HEADS_B * HEAD_DIM
KV_WIDTH_B = N_KV_B * HEAD_DIM

LANES = 128
SUBLANES = 8
NEG_BIG = -1e30
VMEM_LIMIT = 56 * 1024 * 1024
PROJ_TM = 1024

F32 = jnp.float32
BF16 = jnp.bfloat16


def _params(sem):
    return pltpu.CompilerParams(dimension_semantics=sem, vmem_limit_bytes=VMEM_LIMIT)


def _rms(x, g):
    return x * lax.rsqrt(jnp.mean(x * x, axis=-1, keepdims=True) + RMS_EPS) * g


def _rope_table_kernel(pos_ref, inv_ref, cos_ref, sina_ref, sinb_ref):
    ang = pos_ref[...].astype(F32) * inv_ref[...]
    c = jnp.cos(ang)
    s = jnp.sin(ang)
    lane = lax.broadcasted_iota(jnp.int32, ang.shape, 1)
    upper = (lane & (HEAD_DIM - 1)) >= HEAD_DIM // 2
    cos_ref[...] = c
    sina_ref[...] = jnp.where(upper, s, 0.0)
    sinb_ref[...] = jnp.where(upper, 0.0, -s)


def _rope_tables(positions):
    seq = positions.shape[0]
    inv = ROPE_THETA ** (-jnp.arange(0, HEAD_DIM, 2, dtype=F32) / HEAD_DIM)
    inv = jnp.tile(inv, LANES // (HEAD_DIM // 2))[None, :]
    rows = 1024
    tab = jax.ShapeDtypeStruct((seq, LANES), F32)
    return pl.pallas_call(
        _rope_table_kernel,
        grid=(seq // rows,),
        in_specs=[pl.BlockSpec((rows, 1), lambda i: (i, 0)),
                  pl.BlockSpec((1, LANES), lambda i: (0, 0))],
        out_specs=[pl.BlockSpec((rows, LANES), lambda i: (i, 0))] * 3,
        out_shape=[tab, tab, tab],
        compiler_params=_params(("arbitrary",)),
        name="rope_tables",
    )(positions.reshape(seq, 1), inv)


LOG2E = 1.4426950408889634
_QSCALE = HEAD_DIM ** -0.5 * LOG2E
_SEGS = (("qa", WIDTH_A, True, _QSCALE), ("ka", WIDTH_A, True, 1.0),
         ("va", WIDTH_A, False, 1.0), ("qb", WIDTH_B, True, _QSCALE),
         ("kb", 2 * KV_WIDTH_B, True, 1.0), ("vb", 2 * KV_WIDTH_B, False, 1.0))


def _cast_specs(w, layer, n_steps):
    _, n_exp, rows, cols = w.shape
    per = n_steps // n_exp
    rb = rows // per
    assert n_steps % n_exp == 0 and rows % per == 0 and rb % 16 == 0
    return (pl.BlockSpec((1, 1, rb, cols), lambda i: (layer, i // per, i % per, 0)),
            pl.BlockSpec((1, rb, cols), lambda i: (i // per, i % per, 0)),
            jax.ShapeDtypeStruct((n_exp, rows, cols), BF16))


def _inproj_kernel(x_ref, g_ref, w_ref, cos_ref, sina_ref, sinb_ref, cast_in_ref, *out_refs):
    out_refs[-1][0] = cast_in_ref[0, 0].astype(BF16)
    h = _rms(x_ref[...], g_ref[...]).astype(BF16)
    cos = cos_ref[...]
    sina = sina_ref[...]
    sinb = sinb_ref[...]
    col = 0
    for (_, width, roped, scale), o_ref in zip(_SEGS, out_refs):
        t = jnp.dot(h, w_ref[:, col:col + width], preferred_element_type=F32)
        col += width
        for j in range(width // LANES):
            tj = t[:, j * LANES:(j + 1) * LANES]
            if roped:
                tj = (tj * cos + pltpu.roll(tj, HEAD_DIM // 2, 1) * sina
                      + pltpu.roll(tj, LANES - HEAD_DIM // 2, 1) * sinb)
                if scale != 1.0:
                    tj = tj * scale
            o_ref[:, j * LANES:(j + 1) * LANES] = tj.astype(o_ref.dtype)


def _inproj(x2, g, w_ext, cos, sina, sinb, seq, cast_w, cast_layer):
    tokens = x2.shape[0]
    tm = PROJ_TM
    spt = seq // tm
    widths = [s[1] for s in _SEGS]
    total = sum(widths)
    cast_in, cast_out, cast_shape = _cast_specs(cast_w, cast_layer, tokens // tm)
    return pl.pallas_call(
        _inproj_kernel,
        grid=(tokens // tm,),
        in_specs=[pl.BlockSpec((tm, D_MODEL), lambda i: (i, 0)),
                  pl.BlockSpec((1, D_MODEL), lambda i: (0, 0)),
                  pl.BlockSpec((D_MODEL, total), lambda i: (0, 0)),
                  pl.BlockSpec((tm, LANES), lambda i: (i % spt, 0)),
                  pl.BlockSpec((tm, LANES), lambda i: (i % spt, 0)),
                  pl.BlockSpec((tm, LANES), lambda i: (i % spt, 0)),
                  cast_in],
        out_specs=[pl.BlockSpec((tm, w), lambda i: (i, 0)) for w in widths] + [cast_out],
        out_shape=[jax.ShapeDtypeStruct((tokens, w), BF16) for w in widths] + [cast_shape],
        compiler_params=_params(("arbitrary",)),
        name="inproj_rope",
    )(x2, g, w_ext, cos, sina, sinb, cast_w)


def _band_bias(max_dist):
    p = np.arange(BLK)[:, None]
    c = np.arange(2 * BLK)[None, :]
    first = c <= p
    dist = BLK + p - c
    main = (dist >= 0) & (dist <= max_dist)
    bias = np.where(np.stack([first, main]), 0.0, NEG_BIG).astype(np.float32)
    return jnp.asarray(np.concatenate([bias, bias], axis=1))


def _low_lanes():
    return lax.broadcasted_iota(jnp.int32, (BLK, LANES), 1) < HEAD_DIM


def _band_block(q, kb, vb, bias, sink=None):
    low = _low_lanes()
    zero = jnp.zeros_like(q)
    qq = jnp.concatenate([jnp.where(low, q, zero), jnp.where(low, zero, q)], axis=0)
    s = lax.dot_general(qq, kb, (((1,), (1,)), ((), ())), preferred_element_type=F32) + bias
    m = jnp.max(s, axis=1, keepdims=True)
    if sink is not None:
        m = jnp.maximum(m, sink)
    p = jnp.exp2(s - m).astype(BF16)
    ol = jnp.dot(p, jnp.concatenate([vb, jnp.ones_like(vb)], axis=1),
                 preferred_element_type=F32)
    acc = jnp.where(low, ol[:BLK, :LANES], ol[BLK:, :LANES])
    l = jnp.where(low, ol[:BLK, LANES:], ol[BLK:, LANES:])
    if sink is not None:
        ls = jnp.exp2(sink - m)
        l = l + jnp.where(low, ls[:BLK], ls[BLK:])
    return acc, jnp.where(low, m[:BLK], m[BLK:]), l


def _finish(acc, m, l):
    return acc * (1.0 / l), m + jnp.log2(l)


def _merge(o_s, lse_s, acc, m, l, want_lse=True):
    hi = jnp.maximum(lse_s, m)
    ws = jnp.exp2(lse_s - hi)
    wb = jnp.exp2(m - hi)
    den = ws + wb * l
    o = (ws * o_s + wb * acc) * (1.0 / den)
    return (o, hi + jnp.log2(den)) if want_lse else (o, None)


_CLASSES = 4
_LOOP_UNROLL = 32


def _dilated_kernel(q_ref, k_ref, v_ref, bias_ref, o_ref,
                    q4, k4, v4, tmp, so4, sl4, il_o, il_l, *, seq):
    cls = seq // _CLASSES
    nb4 = cls // BLK
    nb16 = seq // 16 // BLK
    sh4, sh16 = nb4.bit_length() - 1, nb16.bit_length() - 1
    chunk = 512
    shc = (cls // chunk).bit_length() - 1

    for src, dst in ((q_ref, q4), (k_ref, k4), (v_ref, v4)):
        def widen(i, c, src=src):
            r0 = pl.multiple_of(i * chunk, chunk)
            tmp[pl.ds(r0, chunk), :] = src[0, pl.ds(r0, chunk), :].astype(F32)
            return c
        lax.fori_loop(0, seq // chunk, widen, 0)

        def regroup(i, c, dst=dst):
            r = lax.shift_right_logical(i, shc)
            j = i & (cls // chunk - 1)
            d0 = pl.multiple_of(r * cls + j * chunk, chunk)
            dst[pl.ds(d0, chunk), :] = tmp[pl.ds(r + _CLASSES * chunk * j, chunk,
                                                 stride=_CLASSES), :]
            return c
        lax.fori_loop(0, seq // chunk, regroup, 0)

    def dil4(idx, c):
        r = lax.shift_right_logical(idx, sh4)
        n = idx & (nb4 - 1)
        q0 = pl.multiple_of(r * cls + n * BLK, BLK)
        k0 = pl.multiple_of(r * cls + jnp.maximum(n - 1, 0) * BLK, BLK)
        o, lse = _finish(*_band_block(q4[pl.ds(q0, BLK), :].astype(BF16),
                                      k4[pl.ds(k0, 2 * BLK), :].astype(BF16),
                                      v4[pl.ds(k0, 2 * BLK), :].astype(BF16),
                                      bias_ref[jnp.minimum(n, 1)]))
        so4[pl.ds(q0, BLK), :] = o
        sl4[pl.ds(q0, BLK), :] = lse
        return c
    lax.fori_loop(0, _CLASSES * nb4, dil4, 0, unroll=_LOOP_UNROLL)

    def dil16(idx, c):
        n = idx & (nb16 - 1)
        sub = lax.shift_right_logical(idx, sh16) & (_CLASSES - 1)
        r = lax.shift_right_logical(idx, sh16 + 2)
        q0 = r * cls + sub + _CLASSES * BLK * n
        k0 = r * cls + sub + _CLASSES * BLK * jnp.maximum(n - 1, 0)
        qs = pl.ds(q0, BLK, stride=_CLASSES)
        ks = pl.ds(k0, 2 * BLK, stride=_CLASSES)
        o_s, l_s = so4[qs, :], sl4[qs, :]
        o, lse = _merge(o_s, l_s, *_band_block(
            q4[qs, :].astype(BF16), k4[ks, :].astype(BF16), v4[ks, :].astype(BF16),
            bias_ref[jnp.minimum(n, 1)]))
        so4[qs, :] = o
        sl4[qs, :] = lse
        return c
    lax.fori_loop(0, 16 * nb16, dil16, 0, unroll=_LOOP_UNROLL)

    piece = BLK // _CLASSES

    def dil1(n, c):
        q0 = pl.multiple_of(n * BLK, BLK)
        k0 = pl.multiple_of(jnp.maximum(n - 1, 0) * BLK, BLK)
        for r in range(_CLASSES):
            s0 = pl.multiple_of(r * cls + n * piece, piece)
            il_o[pl.ds(r, piece, stride=_CLASSES), :] = so4[pl.ds(s0, piece), :]
            il_l[pl.ds(r, piece, stride=_CLASSES), :] = sl4[pl.ds(s0, piece), :]
        o_s, l_s = il_o[...], il_l[...]
        o, _ = _merge(o_s, l_s, *_band_block(
            q_ref[0, pl.ds(q0, BLK), :], k_ref[0, pl.ds(k0, 2 * BLK), :],
            v_ref[0, pl.ds(k0, 2 * BLK), :], bias_ref[jnp.minimum(n, 1)]), want_lse=False)
        o_ref[0, pl.ds(q0, BLK), :] = o.astype(o_ref.dtype)
        return c
    lax.fori_loop(0, seq // BLK, dil1, 0, unroll=_LOOP_UNROLL)


def _dilated_attention(q, k, v):
    assert DILATED_BRANCHES == ((128, 1), (512, 4), (2048, 16))
    batch, seq, width = q.shape
    assert seq % (16 * BLK) == 0 and (seq & (seq - 1)) == 0
    blk = pl.BlockSpec((1, seq, LANES), lambda b, c: (b, 0, c))
    rows = pltpu.VMEM((seq, LANES), F32)
    small = pltpu.VMEM((BLK, LANES), F32)
    return pl.pallas_call(
        functools.partial(_dilated_kernel, seq=seq),
        grid=(batch, width // LANES),
        in_specs=[blk, blk, blk, pl.BlockSpec((2, 2 * BLK, 2 * BLK), lambda b, c: (0, 0, 0))],
        out_specs=blk,
        out_shape=jax.ShapeDtypeStruct((batch, seq, width), BF16),
        scratch_shapes=[rows] * 6 + [small] * 2,
        compiler_params=_params(("arbitrary", "arbitrary")),
        name="dilated_attn",
    )(q, k, v, _band_bias(BLK))


def _swa_kernel(sink_ref, q_ref, k_ref, v_ref, bias_ref, o_ref, *, n_blocks):
    c = pl.program_id(1)
    row = lax.broadcasted_iota(jnp.int32, (2 * BLK, 1), 0)
    sink = jnp.where(row < BLK, sink_ref[2 * c], sink_ref[2 * c + 1]) * LOG2E

    def body(n, carry):
        q0 = pl.multiple_of(n * BLK, BLK)
        k0 = pl.multiple_of(jnp.maximum(n - 1, 0) * BLK, BLK)
        acc, _, l = _band_block(q_ref[0, pl.ds(q0, BLK), :], k_ref[0, pl.ds(k0, 2 * BLK), :],
                                v_ref[0, pl.ds(k0, 2 * BLK), :], bias_ref[jnp.minimum(n, 1)],
                                sink)
        o_ref[0, pl.ds(q0, BLK), :] = (acc * (1.0 / l)).astype(o_ref.dtype)
        return carry
    lax.fori_loop(0, n_blocks, body, 0, unroll=_LOOP_UNROLL)


def _swa_attention(q, k, v, sinks):
    batch, seq, width = q.shape
    qblk = pl.BlockSpec((1, seq, LANES), lambda b, c, s: (b, 0, c))
    kblk = pl.BlockSpec((1, seq, LANES), lambda b, c, s: (b, 0, c // 2))
    grid_spec = pltpu.PrefetchScalarGridSpec(
        num_scalar_prefetch=1, grid=(batch, width // LANES),
        in_specs=[qblk, kblk, kblk,
                  pl.BlockSpec((2, 2 * BLK, 2 * BLK), lambda b, c, s: (0, 0, 0))],
        out_specs=qblk)
    return pl.pallas_call(
        functools.partial(_swa_kernel, n_blocks=seq // BLK), grid_spec=grid_spec,
        out_shape=jax.ShapeDtypeStruct((batch, seq, width), BF16),
        compiler_params=_params(("arbitrary", "arbitrary")),
        name="swa_sink_attn",
    )(sinks, q, k, v, _band_bias(SWA_WINDOW - 1))


ROW_TM = 512
RUN_ALIGN = SUBLANES
SORT_ROWS = 2 * ROW_TM + N_EXPERTS * RUN_ALIGN
_R_G1, _R_G2, _R_POS1, _R_POS2 = range(4)
_T_COUNT, _T_OFF, _T_BEFORE = range(3)
RUN_TABLE_WORDS = 32


def _route(logits, count_ref):
    tm = logits.shape[0]
    lane = lax.broadcasted_iota(jnp.int32, logits.shape, 1)
    v1 = jnp.max(logits, axis=1, keepdims=True)
    i1 = jnp.min(jnp.where(logits == v1, lane, LANES), axis=1, keepdims=True)
    rest = jnp.where(lane == i1, NEG_BIG, logits)
    v2 = jnp.max(rest, axis=1, keepdims=True)
    i2 = jnp.min(jnp.where(rest == v2, lane, LANES), axis=1, keepdims=True)
    e2 = jnp.exp(v2 - v1)
    g1 = 1.0 / (1.0 + e2)
    g2 = e2 / (1.0 + e2)
    sel = jnp.where((lane == i1) | (lane == i2), 1.0, 0.0)
    r = lax.broadcasted_iota(jnp.int32, (tm, tm), 0)
    c = lax.broadcasted_iota(jnp.int32, (tm, tm), 1)
    tri = jnp.where(c < r, 1.0, 0.0).astype(BF16)
    local = jnp.dot(tri, sel.astype(BF16), preferred_element_type=F32)
    count = jnp.sum(sel, axis=0, keepdims=True)
    padded = jnp.floor((count + (RUN_ALIGN - 1)) * (1.0 / RUN_ALIGN)) * RUN_ALIGN
    run_end = jnp.broadcast_to(padded, (SUBLANES, LANES))
    for shift in (1, 2, 4):
        run_end = run_end + pltpu.roll(run_end, shift, 1)
    offset = run_end[0:1] - padded
    pos = local + offset
    pos1 = jnp.sum(jnp.where(lane == i1, pos, 0.0), axis=1, keepdims=True)
    pos2 = jnp.sum(jnp.where(lane == i2, pos, 0.0), axis=1, keepdims=True)
    before = count_ref[...]
    count_ref[...] = before + padded
    rec = jnp.zeros(logits.shape, F32)
    for ln, col in ((_R_G1, g1), (_R_G2, g2), (_R_POS1, pos1), (_R_POS2, pos2)):
        rec = jnp.where(lane == ln, col, rec)
    row = lax.broadcasted_iota(jnp.int32, (SUBLANES, LANES), 0)
    tile = jnp.zeros((SUBLANES, LANES), F32)
    for rw, val in ((_T_COUNT, padded), (_T_OFF, offset), (_T_BEFORE, before)):
        tile = jnp.where(row == rw, val, tile)
    return rec, tile


def _outproj_kernel(*refs, with_router):
    oa_ref, ob_ref, x_ref, ga_ref, gb_ref, w_ref, gf_ref = refs[:7]
    rest = refs[7:]
    if with_router:
        r_ref, xo_ref, ho_ref, route_ref, tile_ref, count_ref = rest
    else:
        cast_in_ref, xo_ref, ho_ref, cast_out_ref = rest
        cast_out_ref[0] = cast_in_ref[0, 0].astype(BF16)
    na = _rms(oa_ref[...].astype(F32), ga_ref[...]).astype(BF16)
    nb = _rms(ob_ref[...].astype(F32), gb_ref[...]).astype(BF16)
    xn = (x_ref[...] + jnp.dot(na, w_ref[:WIDTH_A, :], preferred_element_type=F32)
          + jnp.dot(nb, w_ref[WIDTH_A:, :], preferred_element_type=F32))
    xo_ref[...] = xn
    hf = _rms(xn, gf_ref[...])
    h_hi = hf.astype(BF16)
    ho_ref[...] = h_hi
    if with_router:
        @pl.when(pl.program_id(0) == 0)
        def _():
            count_ref[...] = jnp.zeros_like(count_ref)
        h_lo = (hf - h_hi.astype(F32)).astype(BF16)
        t = jnp.dot(h_hi, r_ref[...], preferred_element_type=F32)
        logits = (t[:, :LANES] + t[:, LANES:]
                  + jnp.dot(h_lo, r_ref[:, :LANES], preferred_element_type=F32))
        lane = lax.broadcasted_iota(jnp.int32, logits.shape, 1)
        logits = jnp.where(lane < N_EXPERTS, logits, NEG_BIG)
        route_ref[...], tile_ref[...] = _route(logits, count_ref)


def _outproj(oa, ob, x2, ga, gb, w_out, gf, router_pad=None, cast=None):
    tokens = x2.shape[0]
    with_router = router_pad is not None
    assert with_router != (cast is not None)
    tm = ROW_TM if with_router else PROJ_TM
    row = lambda w: pl.BlockSpec((tm, w), lambda i: (i, 0))
    const = lambda a, b: pl.BlockSpec((a, b), lambda i: (0, 0))
    in_specs = [row(WIDTH_A), row(WIDTH_B), row(D_MODEL), const(1, WIDTH_A), const(1, WIDTH_B),
                const(D_MODEL, D_MODEL), const(1, D_MODEL)]
    out_specs = [row(D_MODEL), row(D_MODEL)]
    out_shape = [jax.ShapeDtypeStruct((tokens, D_MODEL), F32),
                 jax.ShapeDtypeStruct((tokens, D_MODEL), BF16)]
    args = [oa, ob, x2, ga, gb, w_out, gf]
    if with_router:
        in_specs.append(const(D_MODEL, 2 * LANES))
        out_specs += [row(LANES), pl.BlockSpec((SUBLANES, LANES), lambda i: (i, 0)),
                      const(1, LANES)]
        out_shape += [jax.ShapeDtypeStruct((tokens, LANES), F32),
                      jax.ShapeDtypeStruct((tokens // tm * SUBLANES, LANES), F32),
                      jax.ShapeDtypeStruct((1, LANES), F32)]
        args.append(router_pad)
    else:
        cast_in, cast_out, cast_shape = _cast_specs(*cast, tokens // tm)
        in_specs.append(cast_in)
        out_specs.append(cast_out)
        out_shape.append(cast_shape)
        args.append(cast[0])
    return pl.pallas_call(
        functools.partial(_outproj_kernel, with_router=with_router),
        grid=(tokens // tm,), in_specs=in_specs, out_specs=out_specs, out_shape=out_shape,
        compiler_params=_params(("arbitrary",)),
        name="mix_outproj_router" if with_router else "mix_outproj",
    )(*args)


def _silu(x):
    return x * (1.0 / (1.0 + jnp.exp(-x)))


FFN_TM = 1024
GROUP_TM = 768
FFN_TF = 1792
FFN_TC = 1024


def _swiglu_step(h, wg_ref, wu_ref, wd_ref, acc_ref):
    for c0 in range(0, FFN_TF, FFN_TC):
        c1 = min(c0 + FFN_TC, FFN_TF)
        g = jnp.dot(h, wg_ref[:, c0:c1], preferred_element_type=F32)
        u = jnp.dot(h, wu_ref[:, c0:c1], preferred_element_type=F32)
        acc_ref[...] += jnp.dot((_silu(g) * u).astype(BF16), wd_ref[c0:c1, :],
                                preferred_element_type=F32)


def _ffn_kernel(*refs, final):
    h_ref, wg_ref, wu_ref, wd_ref, x_ref = refs[:5]
    if final:
        gfin_ref, o_ref = refs[5:]
    else:
        (o_ref,) = refs[5:]
    f = pl.program_id(1)

    @pl.when(f == 0)
    def _():
        o_ref[...] = x_ref[...]

    _swiglu_step(h_ref[...], wg_ref.at[0], wu_ref.at[0], wd_ref.at[0], o_ref)
    if final:
        @pl.when(f == pl.num_programs(1) - 1)
        def _():
            o_ref[...] = _rms(o_ref[...], gfin_ref[...])


def _ffn(h, x2, wg, wu, wd, layer, final_gain=None):
    tokens = x2.shape[0]
    tm, tf = FFN_TM, FFN_TF
    final = final_gain is not None
    tok = lambda w: pl.BlockSpec((tm, w), lambda i, f: (i, 0))
    in_specs = [tok(D_MODEL),
                pl.BlockSpec((1, D_MODEL, tf), lambda i, f: (layer, 0, f)),
                pl.BlockSpec((1, D_MODEL, tf), lambda i, f: (layer, 0, f)),
                pl.BlockSpec((1, tf, D_MODEL), lambda i, f: (layer, f, 0)),
                tok(D_MODEL)]
    args = [h, wg, wu, wd, x2]
    if final:
        in_specs.append(pl.BlockSpec((1, D_MODEL), lambda i, f: (0, 0)))
        args.append(final_gain)
    return pl.pallas_call(
        functools.partial(_ffn_kernel, final=final),
        grid=(tokens // tm, D_FF // tf), in_specs=in_specs, out_specs=tok(D_MODEL),
        out_shape=jax.ShapeDtypeStruct((tokens, D_MODEL), F32),
        compiler_params=_params(("arbitrary", "arbitrary")),
        name="dense_swiglu",
    )(*args)


def _run_copies(act, meta_ref, make_copy):
    for e in range(N_EXPERTS):
        n = pl.multiple_of(meta_ref[0, 0, N_EXPERTS + e], RUN_ALIGN)

        @pl.when(n > 0)
        def _(e=e, n=n):
            off = pl.multiple_of(meta_ref[0, 0, e], RUN_ALIGN)
            dst = pl.multiple_of(meta_ref[0, 0, 2 * N_EXPERTS + e], RUN_ALIGN)
            getattr(make_copy(off, n, dst), act)()


def _sort_matrix(route, n_rows):
    rec_t = route.T
    p1 = rec_t[_R_POS1:_R_POS1 + 1, :].astype(jnp.int32)
    p2 = rec_t[_R_POS2:_R_POS2 + 1, :].astype(jnp.int32)
    ip = lax.broadcasted_iota(jnp.int32, (n_rows, route.shape[0]), 0)
    return jnp.where((ip == p1) | (ip == p2), 1.0, 0.0).astype(BF16)


def _dispatch_kernel(tend_ref, meta_ref, prev_ref, h_ref, route_ref, out_ref, sort_ref, zero_ref,
                     sems, *, max_tail):
    n_tiles = out_ref.shape[0] // GROUP_TM

    @pl.when(pl.program_id(0) == 0)
    def _():
        zero_ref[...] = jnp.zeros_like(zero_ref)

        def zero_tile(t):
            return pltpu.make_async_copy(zero_ref, out_ref.at[pl.ds(t * GROUP_TM, GROUP_TM)],
                                         sems.at[0])

        def has_tiles(e):
            return tend_ref[e] > (tend_ref[e - 1] if e else 0)

        for act in ("start", "wait"):
            for e in range(N_EXPERTS):
                @pl.when(has_tiles(e))
                def _(e=e, act=act):
                    getattr(zero_tile(tend_ref[e] - 1), act)()
            for k in range(max_tail):
                t = tend_ref[N_EXPERTS - 1] + k

                @pl.when(t < n_tiles)
                def _(t=t, act=act):
                    getattr(zero_tile(t), act)()

    i = pl.program_id(0)
    slot = i & 1

    def run_copy(s):
        return lambda off, n, dst: pltpu.make_async_copy(
            sort_ref.at[s, pl.ds(off, n)], out_ref.at[pl.ds(dst, n)], sems.at[s])

    sort_ref[slot] = jnp.dot(_sort_matrix(route_ref[...], SORT_ROWS), h_ref[...],
                             preferred_element_type=F32)
    _run_copies("start", meta_ref, run_copy(slot))

    @pl.when(i > 0)
    def _():
        _run_copies("wait", prev_ref, run_copy(1 - slot))

    @pl.when(i == pl.num_programs(0) - 1)
    def _():
        _run_copies("wait", meta_ref, run_copy(slot))


def _dispatch(h, route, meta, tile_end, n_slots):
    tokens = h.shape[0]
    tm = ROW_TM
    grid_spec = pltpu.PrefetchScalarGridSpec(
        num_scalar_prefetch=1, grid=(tokens // tm,),
        in_specs=[pl.BlockSpec((1, 1, RUN_TABLE_WORDS), lambda i, te: (i, 0, 0), memory_space=pltpu.SMEM),
                  pl.BlockSpec((1, 1, RUN_TABLE_WORDS), lambda i, te: (jnp.maximum(i - 1, 0), 0, 0),
                               memory_space=pltpu.SMEM),
                  pl.BlockSpec((tm, D_MODEL), lambda i, te: (i, 0)),
                  pl.BlockSpec((tm, LANES), lambda i, te: (i, 0))],
        out_specs=pl.BlockSpec(memory_space=pl.ANY),
        scratch_shapes=[pltpu.VMEM((2, SORT_ROWS, D_MODEL), F32),
                        pltpu.VMEM((GROUP_TM, D_MODEL), F32), pltpu.SemaphoreType.DMA((2,))])
    max_tail = n_slots // GROUP_TM - -(-2 * tokens // GROUP_TM)
    return pl.pallas_call(
        functools.partial(_dispatch_kernel, max_tail=max_tail), grid_spec=grid_spec,
        out_shape=jax.ShapeDtypeStruct((n_slots, D_MODEL), F32),
        compiler_params=_params(("arbitrary",)),
        name="moe_dispatch",
    )(tile_end, meta, meta, h, route)


def _grouped_kernel(texp_ref, nt_ref, h_ref, wg_ref, wu_ref, wd_ref, y_ref):
    del texp_ref

    @pl.when(pl.program_id(1) == 0)
    def _():
        y_ref[...] = jnp.zeros_like(y_ref)

    @pl.when(pl.program_id(0) < nt_ref[0])
    def _():
        _swiglu_step(h_ref[...].astype(BF16), wg_ref.at[0], wu_ref.at[0], wd_ref.at[0], y_ref)


def _grouped_swiglu(hs, tile_expert, n_tiles_used, wg, wu, wd):
    n_slots = hs.shape[0]
    tm, tf = GROUP_TM, FFN_TF
    nf = D_FF // tf
    last = lambda i, nt: jnp.minimum(i, nt[0] - 1)
    fcol = lambda i, f, nt: jnp.where(i < nt[0], f, nf - 1)
    grid_spec = pltpu.PrefetchScalarGridSpec(
        num_scalar_prefetch=2, grid=(n_slots // tm, nf),
        in_specs=[pl.BlockSpec((tm, D_MODEL), lambda i, f, te, nt: (last(i, nt), 0)),
                  pl.BlockSpec((1, D_MODEL, tf), lambda i, f, te, nt: (te[i], 0, fcol(i, f, nt))),
                  pl.BlockSpec((1, D_MODEL, tf), lambda i, f, te, nt: (te[i], 0, fcol(i, f, nt))),
                  pl.BlockSpec((1, tf, D_MODEL), lambda i, f, te, nt: (te[i], fcol(i, f, nt), 0))],
        out_specs=pl.BlockSpec((tm, D_MODEL), lambda i, f, te, nt: (i, 0)))
    return pl.pallas_call(
        _grouped_kernel, grid_spec=grid_spec,
        out_shape=jax.ShapeDtypeStruct((n_slots, D_MODEL), F32),
        compiler_params=_params(("arbitrary", "arbitrary")),
        name="moe_grouped_swiglu",
    )(tile_expert, n_tiles_used, hs, wg, wu, wd)


def _combine_kernel(*refs, final):
    meta_ref, next_ref, route_ref, x_ref, y_ref = refs[:5]
    if final:
        gfin_ref, o_ref, sort_ref, sems = refs[5:]
    else:
        o_ref, sort_ref, sems = refs[5:]
    tm = x_ref.shape[0]
    i = pl.program_id(0)
    slot = i & 1

    def fetch(act, m_ref, s):
        if act == "start":
            sort_ref[s, pl.ds(2 * tm, SORT_ROWS - 2 * tm), :] = jnp.zeros(
                (SORT_ROWS - 2 * tm, D_MODEL), F32)
        _run_copies(act, m_ref, lambda off, n, dst: pltpu.make_async_copy(
            y_ref.at[pl.ds(dst, n)], sort_ref.at[s, pl.ds(off, n)], sems.at[s]))

    @pl.when(i == 0)
    def _():
        fetch("start", meta_ref, slot)

    @pl.when(i + 1 < pl.num_programs(0))
    def _():
        fetch("start", next_ref, 1 - slot)

    fetch("wait", meta_ref, slot)
    ys = sort_ref[slot].astype(BF16)
    route = route_ref[...]
    ip = lax.broadcasted_iota(jnp.int32, (tm, SORT_ROWS), 1)

    def pick(col):
        pos = route[:, col:col + 1].astype(jnp.int32)
        return jnp.dot(jnp.where(ip == pos, 1.0, 0.0).astype(BF16), ys,
                       preferred_element_type=F32)

    out = (x_ref[...] + route[:, _R_G1:_R_G1 + 1] * pick(_R_POS1)
           + route[:, _R_G2:_R_G2 + 1] * pick(_R_POS2))
    o_ref[...] = _rms(out, gfin_ref[...]) if final else out


def _combine(y, route, meta, x2, final_gain=None):
    tokens = x2.shape[0]
    tm = ROW_TM
    final = final_gain is not None
    n_steps = tokens // tm
    in_specs = [pl.BlockSpec((1, 1, RUN_TABLE_WORDS), lambda i: (i, 0, 0), memory_space=pltpu.SMEM),
                pl.BlockSpec((1, 1, RUN_TABLE_WORDS), lambda i: (jnp.minimum(i + 1, n_steps - 1), 0, 0),
                             memory_space=pltpu.SMEM),
                pl.BlockSpec((tm, LANES), lambda i: (i, 0)),
                pl.BlockSpec((tm, D_MODEL), lambda i: (i, 0)),
                pl.BlockSpec(memory_space=pl.ANY)]
    args = [meta, meta, route, x2, y]
    if final:
        in_specs.append(pl.BlockSpec((1, D_MODEL), lambda i: (0, 0)))
        args.append(final_gain)
    return pl.pallas_call(
        functools.partial(_combine_kernel, final=final),
        grid=(tokens // tm,), in_specs=in_specs,
        out_specs=pl.BlockSpec((tm, D_MODEL), lambda i: (i, 0)),
        out_shape=jax.ShapeDtypeStruct((tokens, D_MODEL), F32),
        scratch_shapes=[pltpu.VMEM((2, SORT_ROWS, D_MODEL), F32),
                        pltpu.SemaphoreType.DMA((2,))],
        compiler_params=_params(("arbitrary",)),
        name="moe_combine",
    )(*args)


def _routing_tables(tile_rec, counts, tokens):
    tm = GROUP_TM
    n_tok_tiles = tokens // ROW_TM
    max_rows = 2 * tokens + n_tok_tiles * N_EXPERTS * (RUN_ALIGN - 1)
    n_tiles = (max_rows + N_EXPERTS * (tm - 1)) // tm
    cnt = counts[0, :N_EXPERTS].astype(jnp.int32)
    tiles = (cnt + tm - 1) // tm
    tile_end = jnp.cumsum(tiles)
    start = (tile_end - tiles) * tm
    rec = tile_rec.reshape(n_tok_tiles, SUBLANES, LANES)[:, :, :N_EXPERTS].astype(jnp.int32)
    meta = jnp.concatenate([rec[:, _T_OFF], rec[:, _T_COUNT], start[None, :] + rec[:, _T_BEFORE],
                            jnp.zeros((n_tok_tiles, RUN_TABLE_WORDS - 3 * N_EXPERTS), jnp.int32)],
                           axis=1)
    tile_ids = jnp.arange(n_tiles, dtype=jnp.int32)[:, None]
    tile_expert = jnp.minimum(jnp.sum(tile_ids >= tile_end[None, :], axis=1),
                              N_EXPERTS - 1).astype(jnp.int32)
    return (meta.reshape(n_tok_tiles, 1, RUN_TABLE_WORDS), tile_expert, tile_end.astype(jnp.int32),
            n_tiles * tm)


def _moe(h, route, tile_rec, counts, x2, wg, wu, wd, final_gain=None):
    meta, tile_expert, tile_end, n_slots = _routing_tables(tile_rec, counts, x2.shape[0])
    hs = _dispatch(h, route, meta, tile_end, n_slots)
    y = _grouped_swiglu(hs, tile_expert, tile_end[-1:], wg, wu, wd)
    return _combine(y, route, meta, x2, final_gain)


def _widen_w_in(w):
    a = 3 * WIDTH_A + WIDTH_B
    kb = w[:, a:a + KV_WIDTH_B]
    vb = w[:, a + KV_WIDTH_B:]
    dup = lambda t: jnp.concatenate(
        [t[:, h * HEAD_DIM:(h + 1) * HEAD_DIM] for h in range(N_KV_B) for _ in range(2)], axis=1)
    return jnp.concatenate([w[:, :a], dup(kb), dup(vb)], axis=1).astype(BF16)


def kernel(x, positions, attn_norm, w_in, mix_norm_a, mix_norm_b, sinks, w_out, ffn_norm,
           dense_w_gate, dense_w_up, dense_w_down, router, moe_w_gate, moe_w_up, moe_w_down,
           final_norm):
    batch, seq, _ = x.shape
    depth = w_in.shape[0]
    tokens = batch * seq
    cos, sina, sinb = _rope_tables(positions)
    x2 = x.reshape(tokens, D_MODEL)
    seq3 = lambda t: t.reshape(batch, seq, t.shape[-1])
    dense_w = [w.astype(BF16) for w in (dense_w_gate, dense_w_up, dense_w_down)]
    n_moe = moe_w_gate.shape[0]
    for i in range(depth):
        jm = min(i // 2, n_moe - 1)
        qa, ka, va, qb, kb, vb, w_cast = _inproj(
            x2, attn_norm[i][None, :], _widen_w_in(w_in[i]), cos, sina, sinb, seq,
            moe_w_gate if i % 2 == 0 else moe_w_down, jm)
        if i % 2 == 0:
            wg_b = w_cast
        else:
            wd_b = w_cast
        oa = _dilated_attention(seq3(qa), seq3(ka), seq3(va)).reshape(tokens, WIDTH_A)
        ob = _swa_attention(seq3(qb), seq3(kb), seq3(vb), sinks[i]).reshape(tokens, WIDTH_B)
        j = i // 2
        last = i == depth - 1
        norms = (mix_norm_a[i][None, :], mix_norm_b[i][None, :], w_out[i].astype(BF16),
                 ffn_norm[i][None, :])
        if i % 2 == 0:
            x2, h, wu_b = _outproj(oa, ob, x2, *norms, cast=(moe_w_up, jm))
            x2 = _ffn(h, x2, *dense_w, j, final_gain=final_norm[None, :] if last else None)
        else:
            r_f32 = jnp.pad(router[j], ((0, 0), (0, LANES - N_EXPERTS)))
            r_hi = r_f32.astype(BF16)
            r_lo = (r_f32 - r_hi.astype(F32)).astype(BF16)
            router_pad = jnp.concatenate([r_hi, r_lo], axis=1)
            x2, h, route, tile_rec, counts = _outproj(oa, ob, x2, *norms, router_pad)
            x2 = _moe(h, route, tile_rec, counts, x2, wg_b, wu_b, wd_b,
                      final_gain=final_norm[None, :] if last else None)
    return x2.reshape(batch, seq, D_MODEL)
```
